```python
import jax, jax.numpy as jnp
from jax import lax
import numpy as np

D_MODEL = 2048
BATCH = 8
SEQ = 8192
DEPTH = 2

HEAD_DIM = 128
FOX_HEADS = 8
GDN_HEADS = 8
CONV_CH = 1024
MIX_W = 1024
N_BRANCH = 3
GDN_CONV = 4
CONF_KERNEL = 31
FFN_CONV = 3
D_FF = 5632
MEM_LEN = 256
XA_HEADS = 4
FOX_BLOCK = 128
GDN_CHUNK = 64
FOX_W = FOX_HEADS * HEAD_DIM
GDN_W = GDN_HEADS * HEAD_DIM
XA_W = XA_HEADS * HEAD_DIM
IN_SPLITS = (FOX_W, FOX_W, FOX_W, FOX_HEADS, GDN_W, GDN_W, GDN_W, GDN_HEADS, GDN_HEADS, GDN_W, 2 * CONV_CH, N_BRANCH * D_MODEL)
N_IN = 3 * FOX_W + FOX_HEADS + 4 * GDN_W + 2 * GDN_HEADS + 2 * CONV_CH + N_BRANCH * D_MODEL

kernel_name = "fox_gdn_conformer_parallel_hybrid"

F32 = jnp.float32


def _split(t, sizes):
    idx = np.cumsum(np.array(sizes))[:-1].tolist()
    return jnp.split(t, idx, axis=-1)


def rms_norm(x, g, eps=1e-6):
    xf = x.astype(F32)
    y = xf * lax.rsqrt(jnp.mean(xf * xf, axis=-1, keepdims=True) + eps)
    return (y * g.astype(F32)).astype(x.dtype)


def layer_norm(x, g, b, eps=1e-5):
    xf = x.astype(F32)
    mu = jnp.mean(xf, axis=-1, keepdims=True)
    xc = xf - mu
    y = xc * lax.rsqrt(jnp.mean(xc * xc, axis=-1, keepdims=True) + eps)
    return (y * g.astype(F32) + b.astype(F32)).astype(x.dtype)


def l2_norm(x, eps=1e-6):
    xf = x.astype(F32)
    return xf * lax.rsqrt(jnp.sum(xf * xf, axis=-1, keepdims=True) + eps)


def causal_dwconv(x, w):
    K, C = w.shape
    return lax.conv_general_dilated(
        x, w[:, None, :].astype(x.dtype), window_strides=(1,), padding=((K - 1, 0),),
        dimension_numbers=('NWC', 'WIO', 'NWC'), feature_group_count=C)


def forgetting_attention(q, k, v, log_f):
    B, S, H, D = q.shape
    nb = S // FOX_BLOCK
    F = jnp.cumsum(log_f, axis=1)
    F_k = jnp.transpose(F, (0, 2, 1))
    kf = k.astype(F32)
    vf = v.astype(F32)
    qb = jnp.moveaxis(q.reshape(B, nb, FOX_BLOCK, H, D), 1, 0)
    Fb = jnp.moveaxis(F_k.reshape(B, H, nb, FOX_BLOCK), 2, 0)
    kpos = jnp.arange(S)
    scale = D ** -0.5

    def one_block(args):
        i, q_blk, F_blk = args
        qpos = i * FOX_BLOCK + jnp.arange(FOX_BLOCK)
        s = jnp.einsum('bqhd,bkhd->bhqk', q_blk.astype(F32), kf) * scale
        s = s + F_blk[..., :, None] - F_k[..., None, :]
        s = jnp.where(kpos[None, :] <= qpos[:, None], s, -jnp.inf)
        p = jax.nn.softmax(s, axis=-1)
        return jnp.einsum('bhqk,bkhd->bqhd', p, vf)

    o = lax.map(one_block, (jnp.arange(nb), qb, Fb))
    return jnp.moveaxis(o, 0, 1).reshape(B, S, H, D).astype(v.dtype)


def gated_delta_rule(q, k, v, g, beta):
    B, S, H, dk = q.shape
    dv = v.shape[-1]
    C = GDN_CHUNK
    N = S // C

    def chunk(t):
        return jnp.transpose(t.astype(F32).reshape(B, N, C, H, -1), (0, 3, 1, 2, 4))

    q = chunk(q) * (dk ** -0.5)
    k = chunk(k)
    v = chunk(v)
    g = jnp.transpose(g.astype(F32).reshape(B, N, C, H), (0, 3, 1, 2))
    beta = jnp.transpose(beta.astype(F32).reshape(B, N, C, H), (0, 3, 1, 2))
    g = jnp.cumsum(g, axis=-1)
    causal = jnp.tril(jnp.ones((C, C), dtype=bool))
    strict = jnp.tril(jnp.ones((C, C), dtype=bool), -1)
    decay = jnp.exp(jnp.where(causal, g[..., :, None] - g[..., None, :], -jnp.inf))
    kb = k * beta[..., None]
    vb = v * beta[..., None]
    L = jnp.where(strict, jnp.einsum('bhnid,bhnjd->bhnij', kb, k) * decay, 0.0)
    A = L + jnp.eye(C, dtype=F32)
    rhs = jnp.concatenate([vb, kb * jnp.exp(g)[..., None]], axis=-1)
    sol = lax.linalg.triangular_solve(A, rhs, left_side=True, lower=True, unit_diagonal=True)
    u = sol[..., :dv]
    w = sol[..., dv:]
    attn = jnp.where(causal, jnp.einsum('bhnid,bhnjd->bhnij', q, k) * decay, 0.0)

    def step(state, xs):
        q_i, k_i, u_i, w_i, g_i, a_i = xs
        v_new = u_i - jnp.einsum('bhck,bhkv->bhcv', w_i, state)
        o = jnp.einsum('bhck,bhkv->bhcv', q_i * jnp.exp(g_i)[..., None], state) + jnp.einsum('bhij,bhjv->bhiv', a_i, v_new)
        g_last = g_i[..., -1:]
        state = state * jnp.exp(g_last)[..., None] + jnp.einsum('bhck,bhcv->bhkv', k_i * jnp.exp(g_last - g_i)[..., None], v_new)
        return state, o

    xs = tuple(jnp.moveaxis(t, 2, 0) for t in (q, k, u, w, g, attn))
    state0 = jnp.zeros((B, H, dk, dv), dtype=F32)
    _, o = lax.scan(step, state0, xs)
    return jnp.transpose(o, (1, 0, 3, 2, 4)).reshape(B, S, H, dv)


def hybrid_mixer(h, w_in, fox_fb, fox_q_norm_g, fox_k_norm_g, gdn_conv_w, gdn_a_log, gdn_dt_bias,
                 gdn_out_norm_g, conf_dw_w, conf_dw_b, conf_ln_g, conf_ln_b, w_branch, w_out):
    B, S, _ = h.shape
    p = h @ w_in
    fq, fk, fv, ff, gq, gk, gv, ga, gb, gz, cu, gl = _split(p, IN_SPLITS)
    fq = rms_norm(fq.reshape(B, S, FOX_HEADS, HEAD_DIM), fox_q_norm_g)
    fk = rms_norm(fk.reshape(B, S, FOX_HEADS, HEAD_DIM), fox_k_norm_g)
    fv = fv.reshape(B, S, FOX_HEADS, HEAD_DIM)
    log_f = jax.nn.log_sigmoid(ff.astype(F32) + fox_fb.astype(F32))
    out_a = forgetting_attention(fq, fk, fv, log_f).reshape(B, S, FOX_W)
    qkv = jax.nn.silu(causal_dwconv(jnp.concatenate([gq, gk, gv], axis=-1), gdn_conv_w))
    gq, gk, gv = jnp.split(qkv, [GDN_W, 2 * GDN_W], axis=-1)
    gq = l2_norm(gq.reshape(B, S, GDN_HEADS, HEAD_DIM))
    gk = l2_norm(gk.reshape(B, S, GDN_HEADS, HEAD_DIM))
    gv = gv.reshape(B, S, GDN_HEADS, HEAD_DIM)
    g = -jnp.exp(gdn_a_log.astype(F32)) * jax.nn.softplus(ga.astype(F32) + gdn_dt_bias.astype(F32))
    beta = jax.nn.sigmoid(gb.astype(F32))
    o = gated_delta_rule(gq, gk, gv, g, beta)
    o = rms_norm(o, gdn_out_norm_g) * jax.nn.silu(gz.reshape(B, S, GDN_HEADS, HEAD_DIM).astype(F32))
    out_b = o.reshape(B, S, GDN_W).astype(h.dtype)
    cu_a, cu_g = jnp.split(cu, 2, axis=-1)
    c = cu_a * jax.nn.sigmoid(cu_g)
    c = causal_dwconv(c, conf_dw_w) + conf_dw_b
    out_c = jax.nn.silu(layer_norm(c, conf_ln_g, conf_ln_b))
    branches = jnp.stack([out_a, out_b, out_c], axis=0)
    proj = jnp.einsum('nbsc,ncd->nbsd', branches, w_branch)
    gates = jax.nn.sigmoid(jnp.moveaxis(gl.reshape(B, S, N_BRANCH, D_MODEL), 2, 0).astype(F32))
    y = jnp.sum(gates * proj.astype(F32), axis=0).astype(h.dtype)
    return y @ w_out


def memory_cross_attention(h, m, wq, wkv, q_norm_g, k_norm_g, wo):
    B, S, _ = h.shape
    M = m.shape[1]
    q = rms_norm((h @ wq).reshape(B, S, XA_HEADS, HEAD_DIM), q_norm_g)
    k, v = jnp.split(m @ wkv, 2, axis=-1)
    k = rms_norm(k.reshape(B, M, XA_HEADS, HEAD_DIM), k_norm_g)
    v = v.reshape(B, M, XA_HEADS, HEAD_DIM)
    s = jnp.einsum('bshd,bmhd->bhsm', q.astype(F32), k.astype(F32)) * (HEAD_DIM ** -0.5)
    p = jax.nn.softmax(s, axis=-1)
    o = jnp.einsum('bhsm,bmhd->bshd', p, v.astype(F32)).reshape(B, S, XA_W).astype(h.dtype)
    return o @ wo


def conv_ffn(h, w_up, conv_w, conv_b, w_down):
    a, v = jnp.split(h @ w_up, 2, axis=-1)
    a = causal_dwconv(a, conv_w) + conv_b
    return (jax.nn.silu(a) * v) @ w_down


def _fwd_setup_inputs(seed: int = 0) -> dict:
    key = jax.random.key(seed)
    ks = jax.random.split(key, 32)

    def nrm(k, shape, scale):
        return jax.random.normal(k, shape, dtype=F32) * scale

    def gain(k, shape):
        return 1.0 + 0.02 * jax.random.normal(k, shape, dtype=F32)

    dt = jnp.exp(jax.random.uniform(ks[8], (DEPTH, GDN_HEADS), minval=float(np.log(1e-3)), maxval=float(np.log(1e-1))))
    return {
        "x": nrm(ks[0], (BATCH, SEQ, D_MODEL), 1.0),
        "mem": nrm(ks[1], (BATCH, MEM_LEN, D_MODEL), 1.0),
        "mix_norm_g": gain(ks[2], (DEPTH, D_MODEL)),
        "w_in": nrm(ks[3], (DEPTH, D_MODEL, N_IN), D_MODEL ** -0.5),
        "fox_fb": jax.random.uniform(ks[4], (DEPTH, FOX_HEADS), minval=0.5, maxval=4.0),
        "fox_q_norm_g": gain(ks[5], (DEPTH, HEAD_DIM)),
        "fox_k_norm_g": gain(ks[6], (DEPTH, HEAD_DIM)),
        "gdn_conv_w": nrm(ks[7], (DEPTH, GDN_CONV, 3 * GDN_W), GDN_CONV ** -0.5),
        "gdn_a_log": jnp.log(jax.random.uniform(ks[9], (DEPTH, GDN_HEADS), minval=1.0, maxval=16.0)),
        "gdn_dt_bias": dt + jnp.log(-jnp.expm1(-dt)),
        "gdn_out_norm_g": gain(ks[10], (DEPTH, HEAD_DIM)),
        "conf_dw_w": nrm(ks[11], (DEPTH, CONF_KERNEL, CONV_CH), CONF_KERNEL ** -0.5),
        "conf_dw_b": nrm(ks[12], (DEPTH, CONV_CH), 0.02),
        "conf_ln_g": gain(ks[13], (DEPTH, CONV_CH)),
        "conf_ln_b": nrm(ks[14], (DEPTH, CONV_CH), 0.02),
        "w_branch": nrm(ks[15], (DEPTH, N_BRANCH, MIX_W, D_MODEL), MIX_W ** -0.5),
        "w_out": nrm(ks[16], (DEPTH, D_MODEL, D_MODEL), D_MODEL ** -0.5),
        "mem_norm_g": gain(ks[17], (D_MODEL,)),
        "xattn_norm_g": gain(ks[18], (DEPTH, D_MODEL)),
        "xattn_wq": nrm(ks[19], (DEPTH, D_MODEL, XA_W), D_MODEL ** -0.5),
        "xattn_wkv": nrm(ks[20], (DEPTH, D_MODEL, 2 * XA_W), D_MODEL ** -0.5),
        "xattn_q_norm_g": gain(ks[21], (DEPTH, HEAD_DIM)),
        "xattn_k_norm_g": gain(ks[22], (DEPTH, HEAD_DIM)),
        "xattn_wo": nrm(ks[23], (DEPTH, XA_W, D_MODEL), XA_W ** -0.5),
        "ffn_norm_g": gain(ks[24], (DEPTH, D_MODEL)),
        "ffn_w_up": nrm(ks[25], (DEPTH, D_MODEL, 2 * D_FF), D_MODEL ** -0.5),
        "ffn_conv_w": nrm(ks[26], (DEPTH, FFN_CONV, D_FF), FFN_CONV ** -0.5),
        "ffn_conv_b": nrm(ks[27], (DEPTH, D_FF), 0.02),
        "ffn_w_down": nrm(ks[28], (DEPTH, D_FF, D_MODEL), D_FF ** -0.5),
    }


def _fwd_reference(x, mem, mix_norm_g, w_in, fox_fb, fox_q_norm_g, fox_k_norm_g, gdn_conv_w, gdn_a_log,
              gdn_dt_bias, gdn_out_norm_g, conf_dw_w, conf_dw_b, conf_ln_g, conf_ln_b, w_branch, w_out,
              mem_norm_g, xattn_norm_g, xattn_wq, xattn_wkv, xattn_q_norm_g, xattn_k_norm_g, xattn_wo,
              ffn_norm_g, ffn_w_up, ffn_conv_w, ffn_conv_b, ffn_w_down):
    m = rms_norm(mem, mem_norm_g)
    for l in range(DEPTH):
        x = x + hybrid_mixer(rms_norm(x, mix_norm_g[l]), w_in[l], fox_fb[l], fox_q_norm_g[l], fox_k_norm_g[l],
                             gdn_conv_w[l], gdn_a_log[l], gdn_dt_bias[l], gdn_out_norm_g[l], conf_dw_w[l],
                             conf_dw_b[l], conf_ln_g[l], conf_ln_b[l], w_branch[l], w_out[l])
        x = x + memory_cross_attention(rms_norm(x, xattn_norm_g[l]), m, xattn_wq[l], xattn_wkv[l],
                                       xattn_q_norm_g[l], xattn_k_norm_g[l], xattn_wo[l])
        x = x + conv_ffn(rms_norm(x, ffn_norm_g[l]), ffn_w_up[l], ffn_conv_w[l], ffn_conv_b[l], ffn_w_down[l])
    return x


import jax as _jax
import jax.numpy as _jnp

TWIN_FORMAT = 'train_step'
FWD_PARAMS = ['x', 'mem', 'mix_norm_g', 'w_in', 'fox_fb', 'fox_q_norm_g', 'fox_k_norm_g', 'gdn_conv_w', 'gdn_a_log', 'gdn_dt_bias', 'gdn_out_norm_g', 'conf_dw_w', 'conf_dw_b', 'conf_ln_g', 'conf_ln_b', 'w_branch', 'w_out', 'mem_norm_g', 'xattn_norm_g', 'xattn_wq', 'xattn_wkv', 'xattn_q_norm_g', 'xattn_k_norm_g', 'xattn_wo', 'ffn_norm_g', 'ffn_w_up', 'ffn_conv_w', 'ffn_conv_b', 'ffn_w_down']
TWIN_WEIGHTS = ['mix_norm_g', 'w_in', 'fox_fb', 'fox_q_norm_g', 'fox_k_norm_g', 'gdn_conv_w', 'gdn_a_log', 'gdn_dt_bias', 'gdn_out_norm_g', 'conf_dw_w', 'conf_dw_b', 'conf_ln_g', 'conf_ln_b', 'w_branch', 'w_out', 'mem_norm_g', 'xattn_norm_g', 'xattn_wq', 'xattn_wkv', 'xattn_q_norm_g', 'xattn_k_norm_g', 'xattn_wo', 'ffn_norm_g', 'ffn_w_up', 'ffn_conv_w', 'ffn_conv_b', 'ffn_w_down']
TWIN_DIFF_INPUT = 'x'
TWIN_INPUTS = ['x', 'mem', 'mix_norm_g', 'w_in', 'fox_fb', 'fox_q_norm_g', 'fox_k_norm_g', 'gdn_conv_w', 'gdn_a_log', 'gdn_dt_bias', 'gdn_out_norm_g', 'conf_dw_w', 'conf_dw_b', 'conf_ln_g', 'conf_ln_b', 'w_branch', 'w_out', 'mem_norm_g', 'xattn_norm_g', 'xattn_wq', 'xattn_wkv', 'xattn_q_norm_g', 'xattn_k_norm_g', 'xattn_wo', 'ffn_norm_g', 'ffn_w_up', 'ffn_conv_w', 'ffn_conv_b', 'ffn_w_down', 'loss_target', 'm_mix_norm_g', 'm_w_in', 'm_fox_fb', 'm_fox_q_norm_g', 'm_fox_k_norm_g', 'm_gdn_conv_w', 'm_gdn_a_log', 'm_gdn_dt_bias', 'm_gdn_out_norm_g', 'm_conf_dw_w', 'm_conf_dw_b', 'm_conf_ln_g', 'm_conf_ln_b', 'm_w_branch', 'm_w_out', 'm_mem_norm_g', 'm_xattn_norm_g', 'm_xattn_wq', 'm_xattn_wkv', 'm_xattn_q_norm_g', 'm_xattn_k_norm_g', 'm_xattn_wo', 'm_ffn_norm_g', 'm_ffn_w_up', 'm_ffn_conv_w', 'm_ffn_conv_b', 'm_ffn_w_down', 'v_mix_norm_g', 'v_w_in', 'v_fox_fb', 'v_fox_q_norm_g', 'v_fox_k_norm_g', 'v_gdn_conv_w', 'v_gdn_a_log', 'v_gdn_dt_bias', 'v_gdn_out_norm_g', 'v_conf_dw_w', 'v_conf_dw_b', 'v_conf_ln_g', 'v_conf_ln_b', 'v_w_branch', 'v_w_out', 'v_mem_norm_g', 'v_xattn_norm_g', 'v_xattn_wq', 'v_xattn_wkv', 'v_xattn_q_norm_g', 'v_xattn_k_norm_g', 'v_xattn_wo', 'v_ffn_norm_g', 'v_ffn_w_up', 'v_ffn_conv_w', 'v_ffn_conv_b', 'v_ffn_w_down']
TWIN_OUTPUTS = ['loss', 'grad_x', 'grad_mix_norm_g', 'grad_w_in', 'grad_fox_fb', 'grad_fox_q_norm_g', 'grad_fox_k_norm_g', 'grad_gdn_conv_w', 'grad_gdn_a_log', 'grad_gdn_dt_bias', 'grad_gdn_out_norm_g', 'grad_conf_dw_w', 'grad_conf_dw_b', 'grad_conf_ln_g', 'grad_conf_ln_b', 'grad_w_branch', 'grad_w_out', 'grad_mem_norm_g', 'grad_xattn_norm_g', 'grad_xattn_wq', 'grad_xattn_wkv', 'grad_xattn_q_norm_g', 'grad_xattn_k_norm_g', 'grad_xattn_wo', 'grad_ffn_norm_g', 'grad_ffn_w_up', 'grad_ffn_conv_w', 'grad_ffn_conv_b', 'grad_ffn_w_down', 'delta_mix_norm_g', 'delta_w_in', 'delta_fox_fb', 'delta_fox_q_norm_g', 'delta_fox_k_norm_g', 'delta_gdn_conv_w', 'delta_gdn_a_log', 'delta_gdn_dt_bias', 'delta_gdn_out_norm_g', 'delta_conf_dw_w', 'delta_conf_dw_b', 'delta_conf_ln_g', 'delta_conf_ln_b', 'delta_w_branch', 'delta_w_out', 'delta_mem_norm_g', 'delta_xattn_norm_g', 'delta_xattn_wq', 'delta_xattn_wkv', 'delta_xattn_q_norm_g', 'delta_xattn_k_norm_g', 'delta_xattn_wo', 'delta_ffn_norm_g', 'delta_ffn_w_up', 'delta_ffn_conv_w', 'delta_ffn_conv_b', 'delta_ffn_w_down', 'new_m_mix_norm_g', 'new_m_w_in', 'new_m_fox_fb', 'new_m_fox_q_norm_g', 'new_m_fox_k_norm_g', 'new_m_gdn_conv_w', 'new_m_gdn_a_log', 'new_m_gdn_dt_bias', 'new_m_gdn_out_norm_g', 'new_m_conf_dw_w', 'new_m_conf_dw_b', 'new_m_conf_ln_g', 'new_m_conf_ln_b', 'new_m_w_branch', 'new_m_w_out', 'new_m_mem_norm_g', 'new_m_xattn_norm_g', 'new_m_xattn_wq', 'new_m_xattn_wkv', 'new_m_xattn_q_norm_g', 'new_m_xattn_k_norm_g', 'new_m_xattn_wo', 'new_m_ffn_norm_g', 'new_m_ffn_w_up', 'new_m_ffn_conv_w', 'new_m_ffn_conv_b', 'new_m_ffn_w_down', 'new_v_mix_norm_g', 'new_v_w_in', 'new_v_fox_fb', 'new_v_fox_q_norm_g', 'new_v_fox_k_norm_g', 'new_v_gdn_conv_w', 'new_v_gdn_a_log', 'new_v_gdn_dt_bias', 'new_v_gdn_out_norm_g', 'new_v_conf_dw_w', 'new_v_conf_dw_b', 'new_v_conf_ln_g', 'new_v_conf_ln_b', 'new_v_w_branch', 'new_v_w_out', 'new_v_mem_norm_g', 'new_v_xattn_norm_g', 'new_v_xattn_wq', 'new_v_xattn_wkv', 'new_v_xattn_q_norm_g', 'new_v_xattn_k_norm_g', 'new_v_xattn_wo', 'new_v_ffn_norm_g', 'new_v_ffn_w_up', 'new_v_ffn_conv_w', 'new_v_ffn_conv_b', 'new_v_ffn_w_down']
TWIN_LEAF_KINDS = {'loss': 'loss', 'grad_x': 'grad_x', 'grad_mix_norm_g': 'grad_w', 'grad_w_in': 'grad_w', 'grad_fox_fb': 'grad_w', 'grad_fox_q_norm_g': 'grad_w', 'grad_fox_k_norm_g': 'grad_w', 'grad_gdn_conv_w': 'grad_w', 'grad_gdn_a_log': 'grad_w', 'grad_gdn_dt_bias': 'grad_w', 'grad_gdn_out_norm_g': 'grad_w', 'grad_conf_dw_w': 'grad_w', 'grad_conf_dw_b': 'grad_w', 'grad_conf_ln_g': 'grad_w', 'grad_conf_ln_b': 'grad_w', 'grad_w_branch': 'grad_w', 'grad_w_out': 'grad_w', 'grad_mem_norm_g': 'grad_w', 'grad_xattn_norm_g': 'grad_w', 'grad_xattn_wq': 'grad_w', 'grad_xattn_wkv': 'grad_w', 'grad_xattn_q_norm_g': 'grad_w', 'grad_xattn_k_norm_g': 'grad_w', 'grad_xattn_wo': 'grad_w', 'grad_ffn_norm_g': 'grad_w', 'grad_ffn_w_up': 'grad_w', 'grad_ffn_conv_w': 'grad_w', 'grad_ffn_conv_b': 'grad_w', 'grad_ffn_w_down': 'grad_w', 'delta_mix_norm_g': 'delta_w', 'delta_w_in': 'delta_w', 'delta_fox_fb': 'delta_w', 'delta_fox_q_norm_g': 'delta_w', 'delta_fox_k_norm_g': 'delta_w', 'delta_gdn_conv_w': 'delta_w', 'delta_gdn_a_log': 'delta_w', 'delta_gdn_dt_bias': 'delta_w', 'delta_gdn_out_norm_g': 'delta_w', 'delta_conf_dw_w': 'delta_w', 'delta_conf_dw_b': 'delta_w', 'delta_conf_ln_g': 'delta_w', 'delta_conf_ln_b': 'delta_w', 'delta_w_branch': 'delta_w', 'delta_w_out': 'delta_w', 'delta_mem_norm_g': 'delta_w', 'delta_xattn_norm_g': 'delta_w', 'delta_xattn_wq': 'delta_w', 'delta_xattn_wkv': 'delta_w', 'delta_xattn_q_norm_g': 'delta_w', 'delta_xattn_k_norm_g': 'delta_w', 'delta_xattn_wo': 'delta_w', 'delta_ffn_norm_g': 'delta_w', 'delta_ffn_w_up': 'delta_w', 'delta_ffn_conv_w': 'delta_w', 'delta_ffn_conv_b': 'delta_w', 'delta_ffn_w_down': 'delta_w', 'new_m_mix_norm_g': 'new_m', 'new_m_w_in': 'new_m', 'new_m_fox_fb': 'new_m', 'new_m_fox_q_norm_g': 'new_m', 'new_m_fox_k_norm_g': 'new_m', 'new_m_gdn_conv_w': 'new_m', 'new_m_gdn_a_log': 'new_m', 'new_m_gdn_dt_bias': 'new_m', 'new_m_gdn_out_norm_g': 'new_m', 'new_m_conf_dw_w': 'new_m', 'new_m_conf_dw_b': 'new_m', 'new_m_conf_ln_g': 'new_m', 'new_m_conf_ln_b': 'new_m', 'new_m_w_branch': 'new_m', 'new_m_w_out': 'new_m', 'new_m_mem_norm_g': 'new_m', 'new_m_xattn_norm_g': 'new_m', 'new_m_xattn_wq': 'new_m', 'new_m_xattn_wkv': 'new_m', 'new_m_xattn_q_norm_g': 'new_m', 'new_m_xattn_k_norm_g': 'new_m', 'new_m_xattn_wo': 'new_m', 'new_m_ffn_norm_g': 'new_m', 'new_m_ffn_w_up': 'new_m', 'new_m_ffn_conv_w': 'new_m', 'new_m_ffn_conv_b': 'new_m', 'new_m_ffn_w_down': 'new_m', 'new_v_mix_norm_g': 'new_v', 'new_v_w_in': 'new_v', 'new_v_fox_fb': 'new_v', 'new_v_fox_q_norm_g': 'new_v', 'new_v_fox_k_norm_g': 'new_v', 'new_v_gdn_conv_w': 'new_v', 'new_v_gdn_a_log': 'new_v', 'new_v_gdn_dt_bias': 'new_v', 'new_v_gdn_out_norm_g': 'new_v', 'new_v_conf_dw_w': 'new_v', 'new_v_conf_dw_b': 'new_v', 'new_v_conf_ln_g': 'new_v', 'new_v_conf_ln_b': 'new_v', 'new_v_w_branch': 'new_v', 'new_v_w_out': 'new_v', 'new_v_mem_norm_g': 'new_v', 'new_v_xattn_norm_g': 'new_v', 'new_v_xattn_wq': 'new_v', 'new_v_xattn_wkv': 'new_v', 'new_v_xattn_q_norm_g': 'new_v', 'new_v_xattn_k_norm_g': 'new_v', 'new_v_xattn_wo': 'new_v', 'new_v_ffn_norm_g': 'new_v', 'new_v_ffn_w_up': 'new_v', 'new_v_ffn_conv_w': 'new_v', 'new_v_ffn_conv_b': 'new_v', 'new_v_ffn_w_down': 'new_v'}


def _forward(args):
    return _fwd_reference(*[args[k] for k in FWD_PARAMS])


def _output_shape():
    def fwd():
        inp = _fwd_setup_inputs(0)
        return _fwd_reference(*[inp[k] for k in FWD_PARAMS])
    out = _jax.eval_shape(fwd)
    return out.shape, out.dtype

N_MICROBATCH = 1
ADAM_LR = 0.001
ADAM_B1 = 0.9
ADAM_B2 = 0.999
ADAM_EPS = 1e-08
ADAM_WD = 0.01
ADAM_STEP = 10
PER_EXAMPLE_BATCH_AXIS = {'x': 0, 'mem': 0, 'loss_target': 0}
SHARED_INPUTS = []
_WEIGHT_DTYPES = {'mix_norm_g': _jnp.float32, 'w_in': _jnp.float32, 'fox_fb': _jnp.float32, 'fox_q_norm_g': _jnp.float32, 'fox_k_norm_g': _jnp.float32, 'gdn_conv_w': _jnp.float32, 'gdn_a_log': _jnp.float32, 'gdn_dt_bias': _jnp.float32, 'gdn_out_norm_g': _jnp.float32, 'conf_dw_w': _jnp.float32, 'conf_dw_b': _jnp.float32, 'conf_ln_g': _jnp.float32, 'conf_ln_b': _jnp.float32, 'w_branch': _jnp.float32, 'w_out': _jnp.float32, 'mem_norm_g': _jnp.float32, 'xattn_norm_g': _jnp.float32, 'xattn_wq': _jnp.float32, 'xattn_wkv': _jnp.float32, 'xattn_q_norm_g': _jnp.float32, 'xattn_k_norm_g': _jnp.float32, 'xattn_wo': _jnp.float32, 'ffn_norm_g': _jnp.float32, 'ffn_w_up': _jnp.float32, 'ffn_conv_w': _jnp.float32, 'ffn_conv_b': _jnp.float32, 'ffn_w_down': _jnp.float32}
MOMENT_SCALE = {'mix_norm_g': 6.728452e+00, 'w_in': 2.310537e-01, 'fox_fb': 1.256614e+02, 'fox_q_norm_g': 8.152007e+00, 'fox_k_norm_g': 8.127167e+00, 'gdn_conv_w': 4.525819e-01, 'gdn_a_log': 1.489802e+01, 'gdn_dt_bias': 1.407719e+01, 'gdn_out_norm_g': 5.500728e+01, 'conf_dw_w': 4.814748e-01, 'conf_dw_b': 8.357529e+00, 'conf_ln_g': 9.002971e+00, 'conf_ln_b': 7.151548e+00, 'w_branch': 8.727648e-01, 'w_out': 1.426333e+00, 'mem_norm_g': 7.429722e-01, 'xattn_norm_g': 4.835439e-02, 'xattn_wq': 1.036229e-01, 'xattn_wkv': 3.510137e-01, 'xattn_q_norm_g': 4.873382e+00, 'xattn_k_norm_g': 4.877866e+00, 'xattn_wo': 2.374325e-01, 'ffn_norm_g': 2.609950e+01, 'ffn_w_up': 3.702334e-01, 'ffn_conv_w': 2.979779e+00, 'ffn_conv_b': 3.584066e+00, 'ffn_w_down': 3.319077e-01}


def _to_microbatches(a, axis):
    t = _jnp.moveaxis(a, axis, 0)
    t = t.reshape((N_MICROBATCH, t.shape[0] // N_MICROBATCH) + t.shape[1:])
    return _jnp.moveaxis(t, 1, axis + 1)


def setup_inputs(seed: int = 0) -> dict:
    inp = _fwd_setup_inputs(seed)
    key = _jax.random.fold_in(_jax.random.key(seed), 7919)
    shape, _ = _output_shape()
    out = dict(inp)
    out["loss_target"] = _jax.random.normal(_jax.random.fold_in(key, 0), shape, _jnp.float32)
    for i, name in enumerate(TWIN_WEIGHTS):
        w = inp[name].astype(_jnp.float32)
        if MOMENT_SCALE is None:
            s = _jnp.sqrt(_jnp.mean(_jnp.square(w)) + 1e-30)
        else:
            s = MOMENT_SCALE[name]
        km, kv = _jax.random.split(_jax.random.fold_in(key, i + 1))
        out[name] = w
        out["m_" + name] = s * _jax.random.normal(km, w.shape, _jnp.float32)
        out["v_" + name] = (s * s) * _jax.random.uniform(kv, w.shape, _jnp.float32, 0.5, 1.5)
    if N_MICROBATCH > 1:
        for name, axis in PER_EXAMPLE_BATCH_AXIS.items():
            out[name] = _to_microbatches(out[name], axis)
    return {'x': out['x'], 'mem': out['mem'], 'mix_norm_g': out['mix_norm_g'], 'w_in': out['w_in'], 'fox_fb': out['fox_fb'], 'fox_q_norm_g': out['fox_q_norm_g'], 'fox_k_norm_g': out['fox_k_norm_g'], 'gdn_conv_w': out['gdn_conv_w'], 'gdn_a_log': out['gdn_a_log'], 'gdn_dt_bias': out['gdn_dt_bias'], 'gdn_out_norm_g': out['gdn_out_norm_g'], 'conf_dw_w': out['conf_dw_w'], 'conf_dw_b': out['conf_dw_b'], 'conf_ln_g': out['conf_ln_g'], 'conf_ln_b': out['conf_ln_b'], 'w_branch': out['w_branch'], 'w_out': out['w_out'], 'mem_norm_g': out['mem_norm_g'], 'xattn_norm_g': out['xattn_norm_g'], 'xattn_wq': out['xattn_wq'], 'xattn_wkv': out['xattn_wkv'], 'xattn_q_norm_g': out['xattn_q_norm_g'], 'xattn_k_norm_g': out['xattn_k_norm_g'], 'xattn_wo': out['xattn_wo'], 'ffn_norm_g': out['ffn_norm_g'], 'ffn_w_up': out['ffn_w_up'], 'ffn_conv_w': out['ffn_conv_w'], 'ffn_conv_b': out['ffn_conv_b'], 'ffn_w_down': out['ffn_w_down'], 'loss_target': out['loss_target'], 'm_mix_norm_g': out['m_mix_norm_g'], 'm_w_in': out['m_w_in'], 'm_fox_fb': out['m_fox_fb'], 'm_fox_q_norm_g': out['m_fox_q_norm_g'], 'm_fox_k_norm_g': out['m_fox_k_norm_g'], 'm_gdn_conv_w': out['m_gdn_conv_w'], 'm_gdn_a_log': out['m_gdn_a_log'], 'm_gdn_dt_bias': out['m_gdn_dt_bias'], 'm_gdn_out_norm_g': out['m_gdn_out_norm_g'], 'm_conf_dw_w': out['m_conf_dw_w'], 'm_conf_dw_b': out['m_conf_dw_b'], 'm_conf_ln_g': out['m_conf_ln_g'], 'm_conf_ln_b': out['m_conf_ln_b'], 'm_w_branch': out['m_w_branch'], 'm_w_out': out['m_w_out'], 'm_mem_norm_g': out['m_mem_norm_g'], 'm_xattn_norm_g': out['m_xattn_norm_g'], 'm_xattn_wq': out['m_xattn_wq'], 'm_xattn_wkv': out['m_xattn_wkv'], 'm_xattn_q_norm_g': out['m_xattn_q_norm_g'], 'm_xattn_k_norm_g': out['m_xattn_k_norm_g'], 'm_xattn_wo': out['m_xattn_wo'], 'm_ffn_norm_g': out['m_ffn_norm_g'], 'm_ffn_w_up': out['m_ffn_w_up'], 'm_ffn_conv_w': out['m_ffn_conv_w'], 'm_ffn_conv_b': out['m_ffn_conv_b'], 'm_ffn_w_down': out['m_ffn_w_down'], 'v_mix_norm_g': out['v_mix_norm_g'], 'v_w_in': out['v_w_in'], 'v_fox_fb': out['v_fox_fb'], 'v_fox_q_norm_g': out['v_fox_q_norm_g'], 'v_fox_k_norm_g': out['v_fox_k_norm_g'], 'v_gdn_conv_w': out['v_gdn_conv_w'], 'v_gdn_a_log': out['v_gdn_a_log'], 'v_gdn_dt_bias': out['v_gdn_dt_bias'], 'v_gdn_out_norm_g': out['v_gdn_out_norm_g'], 'v_conf_dw_w': out['v_conf_dw_w'], 'v_conf_dw_b': out['v_conf_dw_b'], 'v_conf_ln_g': out['v_conf_ln_g'], 'v_conf_ln_b': out['v_conf_ln_b'], 'v_w_branch': out['v_w_branch'], 'v_w_out': out['v_w_out'], 'v_mem_norm_g': out['v_mem_norm_g'], 'v_xattn_norm_g': out['v_xattn_norm_g'], 'v_xattn_wq': out['v_xattn_wq'], 'v_xattn_wkv': out['v_xattn_wkv'], 'v_xattn_q_norm_g': out['v_xattn_q_norm_g'], 'v_xattn_k_norm_g': out['v_xattn_k_norm_g'], 'v_xattn_wo': out['v_xattn_wo'], 'v_ffn_norm_g': out['v_ffn_norm_g'], 'v_ffn_w_up': out['v_ffn_w_up'], 'v_ffn_conv_w': out['v_ffn_conv_w'], 'v_ffn_conv_b': out['v_ffn_conv_b'], 'v_ffn_w_down': out['v_ffn_w_down']}


def _loss(weights, diff, rest, loss_target):
    with _jax.named_scope("forward"):
        args = {**rest, TWIN_DIFF_INPUT: diff, **{k: w.astype(_WEIGHT_DTYPES[k]) for k, w in weights.items()}}
        y = _forward(args)
    with _jax.named_scope("loss_head"):
        err = _jnp.square(y.astype(_jnp.float32) - loss_target)
        return 0.5 * _jnp.sum(_jnp.mean(err, axis=-1)) if err.ndim else 0.5 * err


def _adamw(w, g, m, v):
    m = ADAM_B1 * m + (1.0 - ADAM_B1) * g
    v = ADAM_B2 * v + (1.0 - ADAM_B2) * _jnp.square(g)
    m_hat = m / (1.0 - ADAM_B1 ** ADAM_STEP)
    v_hat = v / (1.0 - ADAM_B2 ** ADAM_STEP)
    delta = -ADAM_LR * (m_hat / (_jnp.sqrt(v_hat) + ADAM_EPS) + ADAM_WD * w)
    return delta, m, v


def reference(x, mem, mix_norm_g, w_in, fox_fb, fox_q_norm_g, fox_k_norm_g, gdn_conv_w, gdn_a_log, gdn_dt_bias, gdn_out_norm_g, conf_dw_w, conf_dw_b, conf_ln_g, conf_ln_b, w_branch, w_out, mem_norm_g, xattn_norm_g, xattn_wq, xattn_wkv, xattn_q_norm_g, xattn_k_norm_g, xattn_wo, ffn_norm_g, ffn_w_up, ffn_conv_w, ffn_conv_b, ffn_w_down, loss_target, m_mix_norm_g, m_w_in, m_fox_fb, m_fox_q_norm_g, m_fox_k_norm_g, m_gdn_conv_w, m_gdn_a_log, m_gdn_dt_bias, m_gdn_out_norm_g, m_conf_dw_w, m_conf_dw_b, m_conf_ln_g, m_conf_ln_b, m_w_branch, m_w_out, m_mem_norm_g, m_xattn_norm_g, m_xattn_wq, m_xattn_wkv, m_xattn_q_norm_g, m_xattn_k_norm_g, m_xattn_wo, m_ffn_norm_g, m_ffn_w_up, m_ffn_conv_w, m_ffn_conv_b, m_ffn_w_down, v_mix_norm_g, v_w_in, v_fox_fb, v_fox_q_norm_g, v_fox_k_norm_g, v_gdn_conv_w, v_gdn_a_log, v_gdn_dt_bias, v_gdn_out_norm_g, v_conf_dw_w, v_conf_dw_b, v_conf_ln_g, v_conf_ln_b, v_w_branch, v_w_out, v_mem_norm_g, v_xattn_norm_g, v_xattn_wq, v_xattn_wkv, v_xattn_q_norm_g, v_xattn_k_norm_g, v_xattn_wo, v_ffn_norm_g, v_ffn_w_up, v_ffn_conv_w, v_ffn_conv_b, v_ffn_w_down):
    given = dict(x=x, mem=mem, mix_norm_g=mix_norm_g, w_in=w_in, fox_fb=fox_fb, fox_q_norm_g=fox_q_norm_g, fox_k_norm_g=fox_k_norm_g, gdn_conv_w=gdn_conv_w, gdn_a_log=gdn_a_log, gdn_dt_bias=gdn_dt_bias, gdn_out_norm_g=gdn_out_norm_g, conf_dw_w=conf_dw_w, conf_dw_b=conf_dw_b, conf_ln_g=conf_ln_g, conf_ln_b=conf_ln_b, w_branch=w_branch, w_out=w_out, mem_norm_g=mem_norm_g, xattn_norm_g=xattn_norm_g, xattn_wq=xattn_wq, xattn_wkv=xattn_wkv, xattn_q_norm_g=xattn_q_norm_g, xattn_k_norm_g=xattn_k_norm_g, xattn_wo=xattn_wo, ffn_norm_g=ffn_norm_g, ffn_w_up=ffn_w_up, ffn_conv_w=ffn_conv_w, ffn_conv_b=ffn_conv_b, ffn_w_down=ffn_w_down, loss_target=loss_target, m_mix_norm_g=m_mix_norm_g, m_w_in=m_w_in, m_fox_fb=m_fox_fb, m_fox_q_norm_g=m_fox_q_norm_g, m_fox_k_norm_g=m_fox_k_norm_g, m_gdn_conv_w=m_gdn_conv_w, m_gdn_a_log=m_gdn_a_log, m_gdn_dt_bias=m_gdn_dt_bias, m_gdn_out_norm_g=m_gdn_out_norm_g, m_conf_dw_w=m_conf_dw_w, m_conf_dw_b=m_conf_dw_b, m_conf_ln_g=m_conf_ln_g, m_conf_ln_b=m_conf_ln_b, m_w_branch=m_w_branch, m_w_out=m_w_out, m_mem_norm_g=m_mem_norm_g, m_xattn_norm_g=m_xattn_norm_g, m_xattn_wq=m_xattn_wq, m_xattn_wkv=m_xattn_wkv, m_xattn_q_norm_g=m_xattn_q_norm_g, m_xattn_k_norm_g=m_xattn_k_norm_g, m_xattn_wo=m_xattn_wo, m_ffn_norm_g=m_ffn_norm_g, m_ffn_w_up=m_ffn_w_up, m_ffn_conv_w=m_ffn_conv_w, m_ffn_conv_b=m_ffn_conv_b, m_ffn_w_down=m_ffn_w_down, v_mix_norm_g=v_mix_norm_g, v_w_in=v_w_in, v_fox_fb=v_fox_fb, v_fox_q_norm_g=v_fox_q_norm_g, v_fox_k_norm_g=v_fox_k_norm_g, v_gdn_conv_w=v_gdn_conv_w, v_gdn_a_log=v_gdn_a_log, v_gdn_dt_bias=v_gdn_dt_bias, v_gdn_out_norm_g=v_gdn_out_norm_g, v_conf_dw_w=v_conf_dw_w, v_conf_dw_b=v_conf_dw_b, v_conf_ln_g=v_conf_ln_g, v_conf_ln_b=v_conf_ln_b, v_w_branch=v_w_branch, v_w_out=v_w_out, v_mem_norm_g=v_mem_norm_g, v_xattn_norm_g=v_xattn_norm_g, v_xattn_wq=v_xattn_wq, v_xattn_wkv=v_xattn_wkv, v_xattn_q_norm_g=v_xattn_q_norm_g, v_xattn_k_norm_g=v_xattn_k_norm_g, v_xattn_wo=v_xattn_wo, v_ffn_norm_g=v_ffn_norm_g, v_ffn_w_up=v_ffn_w_up, v_ffn_conv_w=v_ffn_conv_w, v_ffn_conv_b=v_ffn_conv_b, v_ffn_w_down=v_ffn_w_down)
    weights = {n: given[n] for n in TWIN_WEIGHTS}
    shared = {n: given[n] for n in SHARED_INPUTS}
    per_example = {n: given[n] for n in ['x', 'mem']}
    grad_fn = _jax.value_and_grad(_loss, argnums=(0, 1))

    def one_microbatch(ex, loss_target):
        ex = dict(ex)
        diff = ex.pop(TWIN_DIFF_INPUT)
        return grad_fn(weights, diff, {**shared, **ex}, loss_target)

    if N_MICROBATCH == 1:
        loss, (grad_w, grad_x) = one_microbatch(per_example, given["loss_target"])
    else:
        def body(carry, xs):
            loss_sum, grad_sum = carry
            l_k, (gw_k, gx_k) = one_microbatch(xs[0], xs[1])
            with _jax.named_scope("update"):
                return (loss_sum + l_k, _jax.tree.map(_jnp.add, grad_sum, gw_k)), gx_k

        init = (_jnp.zeros((), _jnp.float32), _jax.tree.map(_jnp.zeros_like, weights))
        (loss, grad_w), grad_x = _jax.lax.scan(body, init, (per_example, given["loss_target"]))
    with _jax.named_scope("update"):
        delta_w, new_m, new_v = {}, {}, {}
        for n in TWIN_WEIGHTS:
            delta_w[n], new_m[n], new_v[n] = _adamw(weights[n], grad_w[n], given["m_" + n], given["v_" + n])
    return (loss, grad_x, *[grad_w[n] for n in TWIN_WEIGHTS], *[delta_w[n] for n in TWIN_WEIGHTS],
            *[new_m[n] for n in TWIN_WEIGHTS], *[new_v[n] for n in TWIN_WEIGHTS])
```

```python
import functools
import math
from typing import NamedTuple

import jax
import jax.numpy as jnp
from jax import lax
from jax.experimental import pallas as pl
from jax.experimental.pallas import tpu as pltpu

F32 = jnp.float32
BF16 = jnp.bfloat16
N_DEV = 8
LANES = 128
VMEM_LIMIT = 52 << 20
HI = lax.Precision.HIGHEST

ADAM_LR = 0.001
ADAM_B1 = 0.9
ADAM_B2 = 0.999
ADAM_EPS = 1e-08
ADAM_WD = 0.01
ADAM_STEP = 10

GDN_CHUNK = 64
CONF_HALO = 32
SHORT_HALO = 8
ADAMW_BLOCK_ELEMS = 128 * 1024


def _cparams(sem):
    return pltpu.CompilerParams(dimension_semantics=sem, vmem_limit_bytes=VMEM_LIMIT)


def _pick(n, cands):
    for c in cands:
        if c <= n and n % c == 0:
            return c
    return n


def _peer(k):
    x, y, c = lax.axis_index("x"), lax.axis_index("y"), lax.axis_index("c")
    return (x ^ ((k >> 2) & 1), y ^ ((k >> 1) & 1), c ^ (k & 1))


def _me():
    return 4 * lax.axis_index("x") + 2 * lax.axis_index("y") + lax.axis_index("c")


_HBM = pl.BlockSpec(memory_space=pltpu.HBM)


def _exchange(x, name, gather):
    shape = x.shape if not gather else (N_DEV,) + x.shape

    def body(x_ref, out_ref, send_sems, recv_sems, local_sem):
        me = _me()
        own = pltpu.make_async_copy(x_ref if gather else x_ref.at[me], out_ref.at[me], local_sem)
        own.start()
        sends = []
        for k in range(1, N_DEV):
            peer = me ^ k
            cp = pltpu.make_async_remote_copy(
                src_ref=x_ref if gather else x_ref.at[peer], dst_ref=out_ref.at[me],
                send_sem=send_sems.at[k - 1], recv_sem=recv_sems.at[k - 1],
                device_id=_peer(k), device_id_type=pl.DeviceIdType.MESH)
            cp.start()
            sends.append(cp)
        for cp in sends:
            cp.wait_send()
        for k in range(1, N_DEV):
            peer = me ^ k
            pltpu.make_async_remote_copy(
                src_ref=x_ref if gather else x_ref.at[peer], dst_ref=out_ref.at[peer],
                send_sem=send_sems.at[k - 1], recv_sem=recv_sems.at[k - 1],
                device_id=_peer(k), device_id_type=pl.DeviceIdType.MESH).wait_recv()
        own.wait()

    return pl.pallas_call(
        body, name=name, out_shape=jax.ShapeDtypeStruct(shape, x.dtype),
        in_specs=[_HBM], out_specs=_HBM,
        scratch_shapes=[pltpu.SemaphoreType.DMA((N_DEV - 1,)), pltpu.SemaphoreType.DMA((N_DEV - 1,)),
                        pltpu.SemaphoreType.DMA(())],
    )(x)


def all_gather(x, name):
    return _exchange(x, name, True)


def all_to_all(x, name):
    return _exchange(x, name, False)


_DIMS = {"nn": (((1,), (0,)), ((), ())), "nt": (((1,), (1,)), ((), ())), "tn": (((0,), (0,)), ((), ()))}


def matmul(a, b, mode, name, add=None, out_dtype=F32):
    if mode == "nn":
        (M, K), N = a.shape, b.shape[1]
    elif mode == "nt":
        (M, K), N = a.shape, b.shape[0]
    else:
        (K, M), N = a.shape, b.shape[1]
    tn = _pick(N, (1024, 1408, 512, 256, 128))
    tm = _pick(M, (1024, 1408, 512, 256, 128, 64, 32, 16, 8) if tn <= 1024 else (512, 256, 128, 64, 32, 16, 8))
    tk = _pick(K, (512, 1408, 256, 128))
    nk = K // tk
    if mode == "nn":
        a_spec = pl.BlockSpec((tm, tk), lambda i, j, k: (i, k))
        b_spec = pl.BlockSpec((tk, tn), lambda i, j, k: (k, j))
    elif mode == "nt":
        a_spec = pl.BlockSpec((tm, tk), lambda i, j, k: (i, k))
        b_spec = pl.BlockSpec((tn, tk), lambda i, j, k: (j, k))
    else:
        a_spec = pl.BlockSpec((tk, tm), lambda i, j, k: (k, i))
        b_spec = pl.BlockSpec((tk, tn), lambda i, j, k: (k, j))
    o_spec = pl.BlockSpec((tm, tn), lambda i, j, k: (i, j))
    dims = _DIMS[mode]

    def body(*refs):
        if add is None:
            a_ref, b_ref, o_ref, acc_ref = refs
        else:
            a_ref, b_ref, add_ref, o_ref, acc_ref = refs
        k = pl.program_id(2)

        @pl.when(k == 0)
        def _():
            acc_ref[...] = jnp.zeros_like(acc_ref)

        acc_ref[...] += lax.dot_general(a_ref[...].astype(BF16), b_ref[...].astype(BF16), dims,
                                        preferred_element_type=F32)

        @pl.when(k == nk - 1)
        def _():
            r = acc_ref[...]
            if add is not None:
                r = r + add_ref[...]
            o_ref[...] = r.astype(o_ref.dtype)

    ins = [a, b] if add is None else [a, b, add]
    specs = [a_spec, b_spec] if add is None else [a_spec, b_spec, o_spec]
    return pl.pallas_call(
        body, name=name, grid=(M // tm, N // tn, nk), in_specs=specs, out_specs=o_spec,
        out_shape=jax.ShapeDtypeStruct((M, N), out_dtype),
        scratch_shapes=[pltpu.VMEM((tm, tn), F32)],
        compiler_params=_cparams(("parallel", "parallel", "arbitrary")),
    )(*ins)


class Row(NamedTuple):
    arr: jax.Array
    cb: int
    cw: int
    halo: int = 0


def _row_vals(refs, rows, first):
    vals, it = [], iter(refs)
    for r in rows:
        cur = next(it)[...].astype(F32)
        if r.halo:
            prev = next(it)[...].astype(F32)
            prev = jnp.where(first, jnp.zeros_like(prev), prev)
            cur = jnp.concatenate([prev, cur], axis=0)
        vals.append(cur)
    return vals


def _row_specs(rows, tr, rev_nt=None):
    specs = []
    for r in rows:
        def cur_map(c, i, r=r):
            return ((rev_nt - 1 - i) if rev_nt else i, r.cb + c)
        specs.append(pl.BlockSpec((tr, r.cw), cur_map))
        if r.halo:
            q = tr // r.halo

            def prev_map(c, i, r=r, q=q):
                t = (rev_nt - 1 - i) if rev_nt else i
                return (jnp.maximum(t * q - 1, 0), r.cb + c)
            specs.append(pl.BlockSpec((r.halo, r.cw), prev_map))
    return specs


def _row_args(rows):
    args = []
    for r in rows:
        args.append(r.arr)
        if r.halo:
            args.append(r.arr)
    return args


def _param_specs(params, ncb):
    return [pl.BlockSpec((p.shape[0], p.shape[1] // ncb), lambda c, i: (0, c)) for p in params]


def tile_fwd(name, f, rows, params, outs, tr, ncb=1):
    S = rows[0].arr.shape[0]
    nt = S // tr
    n_in = sum(2 if r.halo else 1 for r in rows)

    def body(*refs):
        first = pl.program_id(1) == 0
        rv = _row_vals(refs[:n_in], rows, first)
        pv = [p[...] for p in refs[n_in:n_in + len(params)]]
        res = f(rv, pv)
        for o_ref, o in zip(refs[n_in + len(params):], res):
            o_ref[...] = o.astype(o_ref.dtype)

    return pl.pallas_call(
        body, name=name, grid=(ncb, nt),
        in_specs=_row_specs(rows, tr) + _param_specs(params, ncb),
        out_specs=[pl.BlockSpec((tr, cw), lambda c, i: (i, c)) for cw, _ in outs],
        out_shape=[jax.ShapeDtypeStruct((S, cw * ncb), dt) for cw, dt in outs],
        compiler_params=_cparams(("parallel", "parallel")),
    )(*_row_args(rows), *params)


def tile_bwd(name, f, rows, params, couts, tr, ncb=1, adds=None):
    S = rows[0].arr.shape[0]
    nt = S // tr
    n_in = sum(2 if r.halo else 1 for r in rows)
    adds = adds or [None] * len(rows)
    add_list = [a for a in adds if a is not None]
    n_p, n_c, n_a, n_r = len(params), len(couts), len(add_list), len(rows)
    halos = [r for r in rows if r.halo]

    def body(*refs):
        i = pl.program_id(1)
        first = i == nt - 1
        pos = 0
        in_refs = refs[pos:pos + n_in]; pos += n_in
        p_refs = refs[pos:pos + n_p]; pos += n_p
        c_refs = refs[pos:pos + n_c]; pos += n_c
        a_refs = list(refs[pos:pos + n_a]); pos += n_a
        dr_refs = refs[pos:pos + n_r]; pos += n_r
        dp_refs = refs[pos:pos + n_p]; pos += n_p
        carry_refs = list(refs[pos:])
        rv = _row_vals(in_refs, rows, first)
        pv = [p[...] for p in p_refs]
        _, vjp = jax.vjp(lambda rv_, pv_: f(rv_, pv_), rv, pv)
        drv, dpv = vjp([c[...].astype(F32) for c in c_refs])
        for r, d, d_ref, a in zip(rows, drv, dr_refs, adds):
            a_val = a_refs.pop(0)[...] if a is not None else None
            if r.halo:
                carry = carry_refs.pop(0)
                cur = d[r.halo:]
                if a_val is not None:
                    cur = cur + a_val
                d_ref[...] = cur

                @pl.when(i > 0)
                def _(d_ref=d_ref, carry=carry, r=r):
                    d_ref[pl.ds(tr - r.halo, r.halo), :] += carry[...]

                carry[...] = d[:r.halo]
            else:
                d_ref[...] = d if a_val is None else d + a_val

        for dp_ref, dp in zip(dp_refs, dpv):
            @pl.when(i == 0)
            def _(dp_ref=dp_ref):
                dp_ref[...] = jnp.zeros_like(dp_ref)

            dp_ref[...] += dp

    rev = lambda c, i: (nt - 1 - i, c)
    return_vals = pl.pallas_call(
        body, name=name, grid=(ncb, nt),
        in_specs=(_row_specs(rows, tr, rev_nt=nt) + _param_specs(params, ncb)
                  + [pl.BlockSpec((tr, c.shape[1] // ncb), rev) for c in couts]
                  + [pl.BlockSpec((tr, a.shape[1] // ncb), rev) for a in add_list]),
        out_specs=([pl.BlockSpec((tr, r.cw), rev) for r in rows] + _param_specs(params, ncb)),
        out_shape=([jax.ShapeDtypeStruct((S, r.cw * ncb), F32) for r in rows]
                   + [jax.ShapeDtypeStruct(p.shape, F32) for p in params]),
        scratch_shapes=[pltpu.VMEM((r.halo, r.cw), F32) for r in halos],
        compiler_params=_cparams(("parallel", "arbitrary")),
    )(*_row_args(rows), *params, *couts, *add_list)
    return list(return_vals[:n_r]), list(return_vals[n_r:])


def adamw(parts, w, m, v, name):
    shape = w.shape
    C = shape[-1]
    R = math.prod(shape[:-1])
    parts2, w2, m2, v2 = parts.reshape(N_DEV, R, C), w.reshape(R, C), m.reshape(R, C), v.reshape(R, C)
    lanes = -(-C // LANES) * LANES
    tr = _pick(R, [t for t in (1024, 512, 256, 128, 64, 32, 16, 8) if t * lanes <= ADAMW_BLOCK_ELEMS])

    def body(p_ref, w_ref, m_ref, v_ref, g_out, d_out, m_out, v_out):
        g = p_ref[0]
        for d in range(1, N_DEV):
            g = g + p_ref[d]
        mm = ADAM_B1 * m_ref[...] + (1.0 - ADAM_B1) * g
        vv = ADAM_B2 * v_ref[...] + (1.0 - ADAM_B2) * jnp.square(g)
        m_hat = mm / (1.0 - ADAM_B1 ** ADAM_STEP)
        v_hat = vv / (1.0 - ADAM_B2 ** ADAM_STEP)
        g_out[...] = g
        d_out[...] = -ADAM_LR * (m_hat / (jnp.sqrt(v_hat) + ADAM_EPS) + ADAM_WD * w_ref[...])
        m_out[...] = mm
        v_out[...] = vv

    blk = pl.BlockSpec((tr, C), lambda i: (i, 0))
    outs = pl.pallas_call(
        body, name=name, grid=(R // tr,),
        in_specs=[pl.BlockSpec((N_DEV, tr, C), lambda i: (0, i, 0)), blk, blk, blk],
        out_specs=[blk] * 4, out_shape=[jax.ShapeDtypeStruct((R, C), F32)] * 4,
        compiler_params=_cparams(("parallel",)),
    )(parts2, w2, m2, v2)
    return tuple(o.reshape(shape) for o in outs)


def _dot(a, b, mode):
    return lax.dot_general(a.astype(BF16), b.astype(BF16), _DIMS[mode], preferred_element_type=F32)


@jax.custom_vjp
def mm_nn(a, b):
    return _dot(a, b, "nn")


@jax.custom_vjp
def mm_nt(a, b):
    return _dot(a, b, "nt")


@jax.custom_vjp
def mm_tn(a, b):
    return _dot(a, b, "tn")


mm_nn.defvjp(lambda a, b: (_dot(a, b, "nn"), (a, b)), lambda r, g: (mm_nt(g, r[1]), mm_tn(r[0], g)))
mm_nt.defvjp(lambda a, b: (_dot(a, b, "nt"), (a, b)), lambda r, g: (mm_nn(g, r[1]), mm_tn(g, r[0])))
mm_tn.defvjp(lambda a, b: (_dot(a, b, "tn"), (a, b)), lambda r, g: (mm_nt(r[1], g), mm_nn(r[0], g)))


def _mmh(a, b):
    return jnp.dot(a, b, precision=HI, preferred_element_type=F32)


def _sigmoid(x):
    return jax.nn.sigmoid(x)


def _silu(x):
    return x * jax.nn.sigmoid(x)


def _softplus(x):
    return jnp.maximum(x, 0.0) + jnp.log(1.0 + jnp.exp(-jnp.abs(x)))


def _log_sigmoid(x):
    return jnp.minimum(x, 0.0) - jnp.log(1.0 + jnp.exp(-jnp.abs(x)))


def _rms(x, g, eps=1e-6):
    return x * lax.rsqrt(jnp.mean(x * x, axis=-1, keepdims=True) + eps) * g


def _heads(fn, x, hd):
    return jnp.concatenate([fn(x[:, h * hd:(h + 1) * hd]) for h in range(x.shape[1] // hd)], axis=1)


def _causal_conv(x, w, halo, tr):
    K = w.shape[0]
    acc = jnp.zeros((tr, x.shape[1]), F32)
    for k in range(K):
        o = halo - (K - 1) + k
        acc = acc + w[k:k + 1] * x[o:o + tr]
    return acc


def _lanes(shape):
    return lax.broadcasted_iota(jnp.int32, shape, len(shape) - 1)


def f_rms(rv, pv):
    return [_rms(rv[0], pv[0])]


def make_f_fox_prep(H, hd):
    def f(rv, pv):
        fq, fk, fv, small = rv
        fb, gq, gk = pv
        qn = _heads(lambda t: _rms(t, gq), fq, hd)
        kn = _heads(lambda t: _rms(t, gk), fk, hd)
        logf = jnp.where(_lanes(small.shape) < H, _log_sigmoid(small + fb), 0.0)
        return [qn, kn, fv, logf]
    return f


def make_f_short_conv(tr, act):
    def f(rv, pv):
        y = _causal_conv(rv[0], pv[0], SHORT_HALO, tr)
        return [_silu(y) if act else y]
    return f


def make_f_gdn_prep(H, hd):
    def l2(t):
        return t * lax.rsqrt(jnp.sum(t * t, axis=-1, keepdims=True) + 1e-6)

    def f(rv, pv):
        q, k, small = rv
        alog, dtb = pv
        qn = _heads(l2, q, hd) * (hd ** -0.5)
        kn = _heads(l2, k, hd)
        ln = _lanes(small.shape)
        g = -jnp.exp(alog) * _softplus(small + dtb)
        gates = jnp.where((ln >= H) & (ln < 2 * H), g, jnp.where((ln >= 2 * H) & (ln < 3 * H), _sigmoid(small), 0.0))
        return [qn, kn, gates]
    return f


def _tri(n, strict=False, upper=False):
    ii = lax.broadcasted_iota(jnp.int32, (n, n), 0)
    jj = lax.broadcasted_iota(jnp.int32, (n, n), 1)
    if upper:
        ii, jj = jj, ii
    return ii > jj if strict else ii >= jj


def _decay(gcol, mask):
    C = gcol.shape[0]
    G = jnp.broadcast_to(gcol, (C, C))
    return jnp.where(mask, jnp.exp(jnp.where(mask, G - G.T, 0.0)), 0.0)


def make_f_gdn_solve(H, hd):
    C = GDN_CHUNK

    def f(rv, pv):
        k, v, gates = rv
        gc = _mmh(_tri(C).astype(F32), gates)
        strict = _tri(C, strict=True)
        eye = (lax.broadcasted_iota(jnp.int32, (C, C), 0) == lax.broadcasted_iota(jnp.int32, (C, C), 1)).astype(F32)
        us, ws = [], []
        for h in range(H):
            kh, vh = k[:, h * hd:(h + 1) * hd], v[:, h * hd:(h + 1) * hd]
            gcol, beta = gc[:, H + h:H + h + 1], gates[:, 2 * H + h:2 * H + h + 1]
            kb = kh * beta
            X = -(mm_nt(kb, kh) * _decay(gcol, strict))
            T, P = eye + X, X
            for _ in range(int(math.log2(C)) - 1):
                P = _mmh(P, P)
                T = T + _mmh(T, P)
            us.append(_mmh(T, vh * beta))
            ws.append(_mmh(T, kb * jnp.exp(gcol)))
        return [jnp.concatenate(us, axis=1), jnp.concatenate(ws, axis=1), gc]
    return f


def _gdn_step(S, q, k, u, w, gcol):
    C = q.shape[0]
    causal = _tri(C)
    attn = mm_nt(q, k) * _decay(gcol, causal)
    v_new = u - mm_nn(w, S)
    o = mm_nn(q * jnp.exp(gcol), S) + mm_nn(attn, v_new)
    rows = lax.broadcasted_iota(jnp.int32, gcol.shape, 0)
    glast = jnp.sum(jnp.where(rows == C - 1, gcol, 0.0), axis=0, keepdims=True)
    S_new = S * jnp.exp(glast) + mm_tn(k * jnp.exp(glast - gcol), v_new)
    return o, S_new


def make_f_gdn_out(hd):
    def f(rv, pv):
        o, gz = rv
        return [_heads(lambda t: _rms(t, pv[0]), o, hd) * _silu(gz)]
    return f


def make_f_conf(tr):
    def f(rv, pv):
        a, g = rv
        w, b, lg, lb = pv
        y = _causal_conv(a * _sigmoid(g), w, CONF_HALO, tr) + b
        xc = y - jnp.mean(y, axis=-1, keepdims=True)
        y = xc * lax.rsqrt(jnp.mean(xc * xc, axis=-1, keepdims=True) + 1e-5) * lg + lb
        return [_silu(y)]
    return f


def f_merge(rv, pv):
    n = len(rv) // 2
    y = _sigmoid(rv[0]) * rv[n]
    for j in range(1, n):
        y = y + _sigmoid(rv[j]) * rv[n + j]
    return [y]


def make_f_xattn(XH, hd):
    XW = XH * hd

    def f(rv, pv):
        q = rv[0]
        kv, gq, gk = pv
        outs = []
        for h in range(XH):
            qh = _rms(q[:, h * hd:(h + 1) * hd], gq)
            kh = _rms(kv[:, h * hd:(h + 1) * hd], gk)
            s = mm_nt(qh, kh) * (hd ** -0.5)
            e = jnp.exp(s - jnp.max(s, axis=-1, keepdims=True))
            outs.append(mm_nn(e / jnp.sum(e, axis=-1, keepdims=True), kv[:, XW + h * hd:XW + (h + 1) * hd]))
        return [jnp.concatenate(outs, axis=1)]
    return f


def make_f_ffn_act(tr):
    def f(rv, pv):
        a, v = rv
        return [_silu(_causal_conv(a, pv[0], SHORT_HALO, tr) + pv[1]) * v]
    return f


def make_f_dprep(H, hd):
    def f(rv, pv):
        do, o = rv
        ind = (lax.broadcasted_iota(jnp.int32, (H * hd, LANES), 0) // hd
               == lax.broadcasted_iota(jnp.int32, (H * hd, LANES), 1)).astype(F32)
        return [do, _mmh(do * o, ind)]
    return f


def cumsum_rows(x, name, reverse=False):
    S, C = x.shape
    tr = _pick(S, (256, 128, 64))
    nt = S // tr

    def body(x_ref, o_ref, carry):
        @pl.when(pl.program_id(0) == 0)
        def _():
            carry[...] = jnp.zeros_like(carry)

        ii = lax.broadcasted_iota(jnp.int32, (tr, tr), 0)
        jj = lax.broadcasted_iota(jnp.int32, (tr, tr), 1)
        tri = (ii <= jj) if reverse else (ii >= jj)
        y = _mmh(tri.astype(F32), x_ref[...]) + carry[...]
        o_ref[...] = y
        rows = lax.broadcasted_iota(jnp.int32, y.shape, 0)
        carry[...] = jnp.sum(jnp.where(rows == (0 if reverse else tr - 1), y, 0.0), axis=0, keepdims=True)

    spec = pl.BlockSpec((tr, C), (lambda i: (nt - 1 - i, 0)) if reverse else (lambda i: (i, 0)))
    return pl.pallas_call(
        body, name=name, grid=(nt,), in_specs=[spec], out_specs=spec,
        out_shape=jax.ShapeDtypeStruct((S, C), F32), scratch_shapes=[pltpu.VMEM((1, C), F32)],
        compiler_params=_cparams(("arbitrary",)),
    )(x)


def _fox_block(S):
    return _pick(S, (512, 256, 128))


def fox_fwd(q, k, v, fcol, frow, H, hd, name):
    S = q.shape[0]
    bq = _fox_block(S)
    nq = S // bq
    scale = hd ** -0.5

    def body(q_ref, k_ref, v_ref, fc_ref, fr_ref, o_ref, lse_ref):
        i = pl.program_id(1)
        qv = q_ref[...]
        fq = fc_ref[0]

        def step(j, carry, diag):
            m, l, acc = carry
            kj = k_ref[pl.ds(pl.multiple_of(j * bq, bq), bq), :]
            vj = v_ref[pl.ds(pl.multiple_of(j * bq, bq), bq), :]
            s = lax.dot_general(qv, kj, _DIMS["nt"], preferred_element_type=F32) * scale
            s = s + fq - fr_ref[0, :, pl.ds(pl.multiple_of(j * bq, bq), bq)]
            if diag:
                s = jnp.where(_tri(bq), s, -jnp.inf)
            m_new = jnp.maximum(m, jnp.max(s, axis=-1, keepdims=True))
            alpha = jnp.exp(m - m_new)
            p = jnp.exp(s - m_new)
            l = alpha * l + jnp.sum(p, axis=-1, keepdims=True)
            acc = alpha * acc + lax.dot_general(p.astype(BF16), vj, _DIMS["nn"], preferred_element_type=F32)
            return m_new, l, acc

        init = (jnp.full((bq, 1), -jnp.inf, F32), jnp.zeros((bq, 1), F32), jnp.zeros((bq, hd), F32))
        carry = lax.fori_loop(0, i, lambda j, c: step(j, c, False), init)
        m, l, acc = step(i, carry, True)
        o_ref[...] = acc / l
        lse_ref[0] = m + jnp.log(l)

    return pl.pallas_call(
        body, name=name, grid=(H, nq),
        in_specs=[pl.BlockSpec((bq, hd), lambda h, i: (i, h)),
                  pl.BlockSpec((S, hd), lambda h, i: (0, h)), pl.BlockSpec((S, hd), lambda h, i: (0, h)),
                  pl.BlockSpec((1, bq, 1), lambda h, i: (h, i, 0)), pl.BlockSpec((1, 1, S), lambda h, i: (h, 0, 0))],
        out_specs=[pl.BlockSpec((bq, hd), lambda h, i: (i, h)), pl.BlockSpec((1, bq, 1), lambda h, i: (h, i, 0))],
        out_shape=[jax.ShapeDtypeStruct((S, H * hd), F32), jax.ShapeDtypeStruct((H, S, 1), F32)],
        compiler_params=_cparams(("parallel", "parallel")),
    )(q, k, v, fcol, frow)


def fox_bwd(q, k, v, do, fcol, frow, lse_row, delta_row, H, hd, name):
    S = q.shape[0]
    bk = _fox_block(S)
    nk = S // bk
    scale = hd ** -0.5

    def body(q_ref, do_ref, k_ref, v_ref, fc_ref, fr_ref, lse_ref, dl_ref, dq_ref, dfq_ref, dk_ref, dv_ref, df_ref):
        j = pl.program_id(1)

        @pl.when(j == 0)
        def _():
            dq_ref[...] = jnp.zeros_like(dq_ref)
            dfq_ref[...] = jnp.zeros_like(dfq_ref)

        kj, vj = k_ref[...], v_ref[...]
        fk = fc_ref[0]

        def step(i, carry, diag):
            dk, dv, df = carry
            rows = pl.ds(pl.multiple_of(i * bk, bk), bk)
            qi, doi = q_ref[rows, :], do_ref[rows, :]
            st = lax.dot_general(kj, qi, _DIMS["nt"], preferred_element_type=F32) * scale
            st = st + fr_ref[0, :, rows] - fk - lse_ref[0, :, rows]
            if diag:
                st = jnp.where(_tri(bk, upper=True), st, -jnp.inf)
            pt = jnp.exp(st)
            dv = dv + lax.dot_general(pt.astype(BF16), doi, _DIMS["nn"], preferred_element_type=F32)
            dpt = lax.dot_general(vj, doi, _DIMS["nt"], preferred_element_type=F32)
            dst = pt * (dpt - dl_ref[0, :, rows])
            df = df - jnp.sum(dst, axis=-1, keepdims=True)
            dfq_ref[0, :, rows] += jnp.sum(dst, axis=0, keepdims=True)
            dsb = dst.astype(BF16)
            dk = dk + lax.dot_general(dsb, qi, _DIMS["nn"], preferred_element_type=F32) * scale
            dq_ref[rows, :] += lax.dot_general(dsb, kj, _DIMS["tn"], preferred_element_type=F32) * scale
            return dk, dv, df

        init = (jnp.zeros((bk, hd), F32), jnp.zeros((bk, hd), F32), jnp.zeros((bk, 1), F32))
        carry = step(j, init, True)
        dk, dv, df = lax.fori_loop(j + 1, nk, lambda i, c: step(i, c, False), carry)
        dk_ref[...] = dk
        dv_ref[...] = dv
        df_ref[0] = df

    whole = pl.BlockSpec((S, hd), lambda h, j: (0, h))
    blk = pl.BlockSpec((bk, hd), lambda h, j: (j, h))
    row = pl.BlockSpec((1, 1, S), lambda h, j: (h, 0, 0))
    col = pl.BlockSpec((1, bk, 1), lambda h, j: (h, j, 0))
    return pl.pallas_call(
        body, name=name, grid=(H, nk),
        in_specs=[whole, whole, blk, blk, col, row, row, row],
        out_specs=[whole, row, blk, blk, col],
        out_shape=[jax.ShapeDtypeStruct((S, H * hd), F32), jax.ShapeDtypeStruct((H, 1, S), F32)]
        + [jax.ShapeDtypeStruct((S, H * hd), F32)] * 2 + [jax.ShapeDtypeStruct((H, S, 1), F32)],
        compiler_params=_cparams(("parallel", "arbitrary")),
    )(q, do, k, v, fcol, frow, lse_row, delta_row)


def gdn_scan_fwd(q, k, u, w, gc, H, hd, name):
    S = q.shape[0]
    C = GDN_CHUNK
    NC = S // C

    def body(q_ref, k_ref, u_ref, w_ref, gc_ref, o_ref, sin_ref, state):
        @pl.when(pl.program_id(0) == 0)
        def _():
            state[...] = jnp.zeros_like(state)

        gcv = gc_ref[...]
        outs = []
        for h in range(H):
            sl = slice(h * hd, (h + 1) * hd)
            S_h = state[h]
            sin_ref[0, h] = S_h
            o_h, S_new = _gdn_step(S_h, q_ref[:, sl], k_ref[:, sl], u_ref[:, sl], w_ref[:, sl], gcv[:, H + h:H + h + 1])
            state[h] = S_new
            outs.append(o_h)
        o_ref[...] = jnp.concatenate(outs, axis=1)

    wide = pl.BlockSpec((C, H * hd), lambda i: (i, 0))
    return pl.pallas_call(
        body, name=name, grid=(NC,),
        in_specs=[wide] * 4 + [pl.BlockSpec((C, LANES), lambda i: (i, 0))],
        out_specs=[wide, pl.BlockSpec((1, H, hd, hd), lambda i: (i, 0, 0, 0))],
        out_shape=[jax.ShapeDtypeStruct((S, H * hd), F32), jax.ShapeDtypeStruct((NC, H, hd, hd), F32)],
        scratch_shapes=[pltpu.VMEM((H, hd, hd), F32)],
        compiler_params=_cparams(("arbitrary",)),
    )(q, k, u, w, gc)


def gdn_scan_bwd(q, k, u, w, gc, sin, do, H, hd, name):
    S = q.shape[0]
    C = GDN_CHUNK
    NC = S // C

    def body(q_ref, k_ref, u_ref, w_ref, gc_ref, sin_ref, do_ref, dq_ref, dk_ref, du_ref, dw_ref, dgc_ref, dstate):
        @pl.when(pl.program_id(0) == 0)
        def _():
            dstate[...] = jnp.zeros_like(dstate)

        gcv = gc_ref[...]
        ln = _lanes(gcv.shape)
        dgc = jnp.zeros_like(gcv)
        parts = [[], [], [], []]
        for h in range(H):
            sl = slice(h * hd, (h + 1) * hd)
            _, vjp = jax.vjp(_gdn_step, sin_ref[0, h], q_ref[:, sl], k_ref[:, sl], u_ref[:, sl], w_ref[:, sl],
                             gcv[:, H + h:H + h + 1])
            dS, dq, dk, du, dw, dg = vjp((do_ref[:, sl], dstate[h]))
            dstate[h] = dS
            for lst, val in zip(parts, (dq, dk, du, dw)):
                lst.append(val)
            dgc = dgc + jnp.where(ln == H + h, dg, 0.0)
        for ref, lst in zip((dq_ref, dk_ref, du_ref, dw_ref), parts):
            ref[...] = jnp.concatenate(lst, axis=1)
        dgc_ref[...] = dgc

    wide = pl.BlockSpec((C, H * hd), lambda i: (NC - 1 - i, 0))
    narrow = pl.BlockSpec((C, LANES), lambda i: (NC - 1 - i, 0))
    return pl.pallas_call(
        body, name=name, grid=(NC,),
        in_specs=[wide] * 4 + [narrow, pl.BlockSpec((1, H, hd, hd), lambda i: (NC - 1 - i, 0, 0, 0)), wide],
        out_specs=[wide] * 4 + [narrow],
        out_shape=[jax.ShapeDtypeStruct((S, H * hd), F32)] * 4 + [jax.ShapeDtypeStruct((S, LANES), F32)],
        scratch_shapes=[pltpu.VMEM((H, hd, hd), F32)],
        compiler_params=_cparams(("arbitrary",)),
    )(q, k, u, w, gc, sin, do)


def loss_head(y, t, name):
    S, D = y.shape
    tr = _pick(S, (256, 128, 64))

    def body(y_ref, t_ref, dy_ref, acc_ref):
        @pl.when(pl.program_id(0) == 0)
        def _():
            acc_ref[...] = jnp.zeros_like(acc_ref)

        err = y_ref[...] - t_ref[...]
        dy_ref[...] = err / D
        acc_ref[...] += jnp.sum(jnp.mean(err * err, axis=-1, keepdims=True), axis=0, keepdims=True)

    blk = pl.BlockSpec((tr, D), lambda i: (i, 0))
    dy, acc = pl.pallas_call(
        body, name=name, grid=(S // tr,), in_specs=[blk, blk],
        out_specs=[blk, pl.BlockSpec((8, LANES), lambda i: (0, 0))],
        out_shape=[jax.ShapeDtypeStruct((S, D), F32), jax.ShapeDtypeStruct((8, LANES), F32)],
        compiler_params=_cparams(("arbitrary",)),
    )(y, t)
    return 0.5 * acc[0, 0], dy


class Dims(NamedTuple):
    D: int
    H: int
    hd: int
    MW: int
    XH: int
    FF: int

    @property
    def n_in(self):
        return 9 * self.MW + 3 * self.H + 3 * self.D

    @property
    def n_in_padded(self):
        return 3 * self.D + 9 * self.MW + LANES


def _in_pieces(dm):
    MW, H, D = dm.MW, dm.H, dm.D
    fq, fk, fv, ff = 0, MW, 2 * MW, 3 * MW
    gq = ff + H
    gk, gv = gq + MW, gq + 2 * MW
    ga = gv + MW
    gb, gz = ga + H, ga + 2 * H
    cu = gz + MW
    gl = cu + 2 * MW
    return [(gl, 3 * D), (fq, MW), (fk, MW), (fv, MW), (gq, MW), (gk, MW), (gv, MW), (gz, MW), (cu, 2 * MW),
            (ff, H), (ga, H), (gb, H)]


def permute_in_cols(w, dm):
    parts = [w[:, s:s + n] for s, n in _in_pieces(dm)]
    parts.append(jnp.zeros((w.shape[0], LANES - 3 * dm.H), w.dtype))
    return jnp.concatenate(parts, axis=1)


def unpermute_in_cols(wp, dm):
    pieces = _in_pieces(dm)
    offs, o = [], 0
    for _, n in pieces:
        offs.append(o)
        o += n
    order = sorted(range(len(pieces)), key=lambda i: pieces[i][0])
    return jnp.concatenate([wp[:, offs[i]:offs[i] + pieces[i][1]] for i in order], axis=1)


def _pad_lanes(v, at):
    return jnp.pad(v, (at, LANES - at - v.shape[0]))[None]


def _ops(x, m, P, dm, t):
    S = x.shape[0]
    tr = _pick(S, (256, 128, 64))
    return dict(tr=tr, trm=_pick(S, (128, 64)), trx=_pick(S, (512, 256, 128, 64)),
                cbq=3 * dm.D // dm.MW, cbs=(3 * dm.D + 9 * dm.MW) // LANES,
                cwf=_pick(dm.FF, (512, 256, 128)))


def layer_fwd(x, m, P, dm, t):
    D, H, hd, MW, XH, FF = dm
    XW = XH * hd
    c = _ops(x, m, P, dm, t)
    tr, cbq, cbs = c["tr"], c["cbq"], c["cbs"]
    (h,) = tile_fwd(t + "mix_norm", f_rms, [Row(x, 0, D)], [P["mix_norm_g"][None]], [(D, BF16)], tr)
    p = matmul(h, P["w_in"], "nn", t + "in_proj")
    fox_rows = [Row(p, cbq, MW), Row(p, cbq + 1, MW), Row(p, cbq + 2, MW), Row(p, cbs, LANES)]
    fox_params = [_pad_lanes(P["fox_fb"], 0), P["fox_q_norm_g"][None], P["fox_k_norm_g"][None]]
    qn, kn, vb, logf = tile_fwd(t + "fox_prep", make_f_fox_prep(H, hd), fox_rows, fox_params,
                                [(MW, BF16)] * 3 + [(LANES, F32)], tr)
    Ft = cumsum_rows(logf, t + "fox_cumsum")[:, :H].T
    fcol, frow = Ft[:, :, None], Ft[:, None, :]
    out_a, lse = fox_fwd(qn, kn, vb, fcol, frow, H, hd, t + "fox_attn")
    (qkv,) = tile_fwd(t + "gdn_conv", make_f_short_conv(tr, True), [Row(p, cbq + 3, MW, SHORT_HALO)],
                      [P["gdn_conv_w"]], [(MW, F32)], tr, ncb=3)
    gp_rows = [Row(qkv, 0, MW), Row(qkv, 1, MW), Row(p, cbs, LANES)]
    gp_params = [_pad_lanes(P["gdn_a_log"], H), _pad_lanes(P["gdn_dt_bias"], H)]
    gq, gk, gates = tile_fwd(t + "gdn_prep", make_f_gdn_prep(H, hd), gp_rows, gp_params,
                             [(MW, F32), (MW, F32), (LANES, F32)], tr)
    u, w, gc = tile_fwd(t + "gdn_solve", make_f_gdn_solve(H, hd), [Row(gk, 0, MW), Row(qkv, 2, MW), Row(gates, 0, LANES)],
                        [], [(MW, F32), (MW, F32), (LANES, F32)], GDN_CHUNK)
    o_g, sin = gdn_scan_fwd(gq, gk, u, w, gc, H, hd, t + "gdn_scan")
    (out_b,) = tile_fwd(t + "gdn_out", make_f_gdn_out(hd), [Row(o_g, 0, MW), Row(p, cbq + 6, MW)],
                        [P["gdn_out_norm_g"][None]], [(MW, F32)], tr)
    conf_params = [P["conf_dw_w"], P["conf_dw_b"][None], P["conf_ln_g"][None], P["conf_ln_b"][None]]
    (out_c,) = tile_fwd(t + "conf", make_f_conf(tr), [Row(p, cbq + 7, MW, CONF_HALO), Row(p, cbq + 8, MW, CONF_HALO)],
                        conf_params, [(MW, F32)], tr)
    branches = [out_a, out_b, out_c]
    proj = [matmul(b, P["w_branch"][n], "nn", t + f"branch{n}") for n, b in enumerate(branches)]
    (y,) = tile_fwd(t + "merge", f_merge, [Row(p, n, D) for n in range(3)] + [Row(pr, 0, D) for pr in proj], [],
                    [(D, BF16)], c["trm"])
    x1 = matmul(y, P["w_out"], "nn", t + "out_proj", add=x)
    (h2,) = tile_fwd(t + "xa_norm", f_rms, [Row(x1, 0, D)], [P["xattn_norm_g"][None]], [(D, BF16)], tr)
    q = matmul(h2, P["xattn_wq"], "nn", t + "xa_q")
    kv = matmul(m, P["xattn_wkv"], "nn", t + "xa_kv")
    xa_params = [kv, P["xattn_q_norm_g"][None], P["xattn_k_norm_g"][None]]
    (o_x,) = tile_fwd(t + "xa_attn", make_f_xattn(XH, hd), [Row(q, 0, XW)], xa_params, [(XW, F32)], c["trx"])
    x2 = matmul(o_x, P["xattn_wo"], "nn", t + "xa_o", add=x1)
    (h3,) = tile_fwd(t + "ffn_norm", f_rms, [Row(x2, 0, D)], [P["ffn_norm_g"][None]], [(D, BF16)], tr)
    av = matmul(h3, P["ffn_w_up"], "nn", t + "ffn_up")
    cwf = c["cwf"]
    (uf,) = tile_fwd(t + "ffn_act", make_f_ffn_act(tr), [Row(av, 0, cwf, SHORT_HALO), Row(av, FF // cwf, cwf)],
                     [P["ffn_conv_w"], P["ffn_conv_b"][None]], [(cwf, BF16)], tr, ncb=FF // cwf)
    x3 = matmul(uf, P["ffn_w_down"], "nn", t + "ffn_down", add=x2)
    res = dict(x=x, h=h, p=p, qn=qn, kn=kn, vb=vb, fcol=fcol, frow=frow, out_a=out_a, lse=lse, qkv=qkv, gq=gq, gk=gk,
               gates=gates, u=u, w=w, gc=gc, sin=sin, o_g=o_g, out_b=out_b, out_c=out_c, proj=proj, y=y, x1=x1, h2=h2,
               q=q, kv=kv, o_x=o_x, x2=x2, h3=h3, av=av, uf=uf)
    return x3, res


def layer_bwd(dx3, m, P, R, dm, t):
    D, H, hd, MW, XH, FF = dm
    XW = XH * hd
    c = _ops(R["x"], m, P, dm, t)
    tr, cbq, cbs = c["tr"], c["cbq"], c["cbs"]
    p = R["p"]
    G = {}
    du = matmul(dx3, P["ffn_w_down"], "nt", t + "ffn_down_dx")
    G["ffn_w_down"] = matmul(R["uf"], dx3, "tn", t + "ffn_down_dw")
    cwf = c["cwf"]
    (da, dv), (G["ffn_conv_w"], dcb) = tile_bwd(
        t + "ffn_act_b", make_f_ffn_act(tr), [Row(R["av"], 0, cwf, SHORT_HALO), Row(R["av"], FF // cwf, cwf)],
        [P["ffn_conv_w"], P["ffn_conv_b"][None]], [du], tr, ncb=FF // cwf)
    G["ffn_conv_b"] = dcb[0]
    dav = jnp.concatenate([da, dv], axis=1)
    dh3 = matmul(dav, P["ffn_w_up"], "nt", t + "ffn_up_dx")
    G["ffn_w_up"] = matmul(R["h3"], dav, "tn", t + "ffn_up_dw")
    (dx2,), (dg,) = tile_bwd(t + "ffn_norm_b", f_rms, [Row(R["x2"], 0, D)], [P["ffn_norm_g"][None]], [dh3], tr, adds=[dx3])
    G["ffn_norm_g"] = dg[0]
    do_x = matmul(dx2, P["xattn_wo"], "nt", t + "xa_o_dx")
    G["xattn_wo"] = matmul(R["o_x"], dx2, "tn", t + "xa_o_dw")
    xa_params = [R["kv"], P["xattn_q_norm_g"][None], P["xattn_k_norm_g"][None]]
    (dq,), (dkv, dgq, dgk) = tile_bwd(t + "xa_attn_b", make_f_xattn(XH, hd), [Row(R["q"], 0, XW)], xa_params, [do_x], c["trx"])
    G["xattn_q_norm_g"], G["xattn_k_norm_g"] = dgq[0], dgk[0]
    dh2 = matmul(dq, P["xattn_wq"], "nt", t + "xa_q_dx")
    G["xattn_wq"] = matmul(R["h2"], dq, "tn", t + "xa_q_dw")
    dm_l = matmul(dkv, P["xattn_wkv"], "nt", t + "xa_kv_dx")
    G["xattn_wkv"] = matmul(m, dkv, "tn", t + "xa_kv_dw")
    (dx1,), (dg,) = tile_bwd(t + "xa_norm_b", f_rms, [Row(R["x1"], 0, D)], [P["xattn_norm_g"][None]], [dh2], tr, adds=[dx2])
    G["xattn_norm_g"] = dg[0]
    dy = matmul(dx1, P["w_out"], "nt", t + "out_proj_dx")
    G["w_out"] = matmul(R["y"], dx1, "tn", t + "out_proj_dw")
    merge_rows = [Row(p, n, D) for n in range(3)] + [Row(pr, 0, D) for pr in R["proj"]]
    dmerge, _ = tile_bwd(t + "merge_b", f_merge, merge_rows, [], [dy], c["trm"])
    dgl, dpr = dmerge[:3], dmerge[3:]
    branches = [R["out_a"], R["out_b"], R["out_c"]]
    dbr = [matmul(dpr[n], P["w_branch"][n], "nt", t + f"branch{n}_dx") for n in range(3)]
    G["w_branch"] = jnp.stack([matmul(branches[n], dpr[n], "tn", t + f"branch{n}_dw") for n in range(3)])
    conf_params = [P["conf_dw_w"], P["conf_dw_b"][None], P["conf_ln_g"][None], P["conf_ln_b"][None]]
    (dcu_a, dcu_g), (G["conf_dw_w"], db, dlg, dlb) = tile_bwd(
        t + "conf_b", make_f_conf(tr), [Row(p, cbq + 7, MW, CONF_HALO), Row(p, cbq + 8, MW, CONF_HALO)],
        conf_params, [dbr[2]], tr)
    G["conf_dw_b"], G["conf_ln_g"], G["conf_ln_b"] = db[0], dlg[0], dlb[0]
    (do_g, dgz), (dg,) = tile_bwd(t + "gdn_out_b", make_f_gdn_out(hd), [Row(R["o_g"], 0, MW), Row(p, cbq + 6, MW)],
                                  [P["gdn_out_norm_g"][None]], [dbr[1]], tr)
    G["gdn_out_norm_g"] = dg[0]
    dgq, dgk2, du_, dw_, dgc = gdn_scan_bwd(R["gq"], R["gk"], R["u"], R["w"], R["gc"], R["sin"], do_g, H, hd, t + "gdn_scan_b")
    (dgk1, dgv, dgates), _ = tile_bwd(
        t + "gdn_solve_b", make_f_gdn_solve(H, hd), [Row(R["gk"], 0, MW), Row(R["qkv"], 2, MW), Row(R["gates"], 0, LANES)],
        [], [du_, dw_, dgc], GDN_CHUNK)
    gp_rows = [Row(R["qkv"], 0, MW), Row(R["qkv"], 1, MW), Row(p, cbs, LANES)]
    gp_params = [_pad_lanes(P["gdn_a_log"], H), _pad_lanes(P["gdn_dt_bias"], H)]
    (dqa, dka, dsmall_g), (dal, ddt) = tile_bwd(t + "gdn_prep_b", make_f_gdn_prep(H, hd), gp_rows, gp_params,
                                               [dgq, dgk1 + dgk2, dgates], tr)
    G["gdn_a_log"], G["gdn_dt_bias"] = dal[0, H:2 * H], ddt[0, H:2 * H]
    (dgqkv,), (G["gdn_conv_w"],) = tile_bwd(
        t + "gdn_conv_b", make_f_short_conv(tr, True), [Row(p, cbq + 3, MW, SHORT_HALO)], [P["gdn_conv_w"]],
        [jnp.concatenate([dqa, dka, dgv], axis=1)], tr, ncb=3)
    do_b, delta = tile_fwd(t + "fox_dprep", make_f_dprep(H, hd), [Row(dbr[0], 0, MW), Row(R["out_a"], 0, MW)], [],
                           [(MW, BF16), (LANES, F32)], tr)
    S = p.shape[0]
    delta_row = delta[:, :H].T[:, None, :]
    lse_row = R["lse"].reshape(H, 1, S)
    dqn, dfq_, dkn, dvf, dfk = fox_bwd(R["qn"], R["kn"], R["vb"], do_b, R["fcol"], R["frow"], lse_row, delta_row, H, hd,
                                       t + "fox_attn_b")
    dF = jnp.pad((dfk.reshape(H, S) + dfq_.reshape(H, S)).T, ((0, 0), (0, LANES - H)))
    dlogf = cumsum_rows(dF, t + "fox_cumsum_b", reverse=True)
    fox_rows = [Row(p, cbq, MW), Row(p, cbq + 1, MW), Row(p, cbq + 2, MW), Row(p, cbs, LANES)]
    fox_params = [_pad_lanes(P["fox_fb"], 0), P["fox_q_norm_g"][None], P["fox_k_norm_g"][None]]
    (dfq, dfk_, dfv, dsmall_f), (dfb, dgq_, dgk_) = tile_bwd(t + "fox_prep_b", make_f_fox_prep(H, hd), fox_rows, fox_params,
                                                          [dqn, dkn, dvf, dlogf], tr)
    G["fox_fb"], G["fox_q_norm_g"], G["fox_k_norm_g"] = dfb[0, :H], dgq_[0], dgk_[0]
    dp = jnp.concatenate(dgl + [dfq, dfk_, dfv, dgqkv, dgz, dcu_a, dcu_g, dsmall_f + dsmall_g], axis=1)
    dh = matmul(dp, P["w_in"], "nt", t + "in_proj_dx")
    G["w_in"] = unpermute_in_cols(matmul(R["h"], dp, "tn", t + "in_proj_dw"), dm)
    (dx,), (dg,) = tile_bwd(t + "mix_norm_b", f_rms, [Row(R["x"], 0, D)], [P["mix_norm_g"][None]], [dh], tr, adds=[dx1])
    G["mix_norm_g"] = dg[0]
    return dx, dm_l, G


def local_step(x, mem, target, layers, mem_norm_g, dm):
    trm = _pick(mem.shape[0], (256, 128, 64, 32, 16, 8))
    (m,) = tile_fwd("mem_norm", f_rms, [Row(mem, 0, dm.D)], [mem_norm_g[None]], [(dm.D, F32)], trm)
    res = []
    for l, P in enumerate(layers):
        x, R = layer_fwd(x, m, P, dm, f"l{l}_")
        res.append(R)
    loss, dx = loss_head(x, target, "loss_head")
    grads, dm_sum = [None] * len(layers), None
    for l in reversed(range(len(layers))):
        dx, dm_l, grads[l] = layer_bwd(dx, m, layers[l], res[l], dm, f"l{l}_")
        dm_sum = dm_l if dm_sum is None else dm_sum + dm_l
    _, (dg,) = tile_bwd("mem_norm_b", f_rms, [Row(mem, 0, dm.D)], [mem_norm_g[None]], [dm_sum], trm)
    return loss, dx, grads, dg[0]


ARG_NAMES = ["x", "mem", "mix_norm_g", "w_in", "fox_fb", "fox_q_norm_g", "fox_k_norm_g", "gdn_conv_w", "gdn_a_log",
             "gdn_dt_bias", "gdn_out_norm_g", "conf_dw_w", "conf_dw_b", "conf_ln_g", "conf_ln_b", "w_branch", "w_out",
             "mem_norm_g", "xattn_norm_g", "xattn_wq", "xattn_wkv", "xattn_q_norm_g", "xattn_k_norm_g", "xattn_wo",
             "ffn_norm_g", "ffn_w_up", "ffn_conv_w", "ffn_conv_b", "ffn_w_down"]
WEIGHTS = ARG_NAMES[2:]
COL_SHARDED = ["w_in", "gdn_conv_w", "conf_dw_w", "w_branch", "xattn_wo", "ffn_w_up", "ffn_conv_w"]
ROW_SHARDED = ["w_out", "xattn_wq", "xattn_wkv", "ffn_w_down"]
MATMUL_WEIGHTS = ["w_in", "w_branch", "w_out", "xattn_wq", "xattn_wkv", "xattn_wo", "ffn_w_up", "ffn_w_down"]
REPLICATED = [n for n in WEIGHTS if n not in COL_SHARDED + ROW_SHARDED]


def _gather_weight(name, w):
    g = all_gather(w.astype(BF16) if name in MATMUL_WEIGHTS else w, "gather_" + name)
    if name in COL_SHARDED:
        g = jnp.moveaxis(g, 0, -2)
        return g.reshape(g.shape[:-2] + (g.shape[-2] * g.shape[-1],))
    g = jnp.moveaxis(g, 0, 1)
    return g.reshape(g.shape[:1] + (g.shape[1] * g.shape[2],) + g.shape[3:])


def _scatter_grad(name, g):
    if name in COL_SHARDED:
        g = g.reshape(g.shape[:-1] + (N_DEV, g.shape[-1] // N_DEV))
        return jnp.moveaxis(g, -2, 0)
    g = g.reshape(g.shape[:1] + (N_DEV, g.shape[1] // N_DEV) + g.shape[2:])
    return jnp.moveaxis(g, 1, 0)


def _pack(arrs):
    flat = jnp.concatenate([a.reshape(-1) for a in arrs])
    rows = -(-flat.shape[0] // (8 * LANES)) * 8
    return jnp.pad(flat, (0, rows * LANES - flat.shape[0])).reshape(rows, LANES)


def _unpack(packed, like):
    flat, out, o = packed.reshape(-1), [], 0
    for a in like:
        out.append(flat[o:o + a.size].reshape(a.shape))
        o += a.size
    return out


def kernel(x, mem, mix_norm_g, w_in, fox_fb, fox_q_norm_g, fox_k_norm_g, gdn_conv_w, gdn_a_log, gdn_dt_bias, gdn_out_norm_g, conf_dw_w, conf_dw_b, conf_ln_g, conf_ln_b, w_branch, w_out, mem_norm_g, xattn_norm_g, xattn_wq, xattn_wkv, xattn_q_norm_g, xattn_k_norm_g, xattn_wo, ffn_norm_g, ffn_w_up, ffn_conv_w, ffn_conv_b, ffn_w_down, loss_target, m_mix_norm_g, m_w_in, m_fox_fb, m_fox_q_norm_g, m_fox_k_norm_g, m_gdn_conv_w, m_gdn_a_log, m_gdn_dt_bias, m_gdn_out_norm_g, m_conf_dw_w, m_conf_dw_b, m_conf_ln_g, m_conf_ln_b, m_w_branch, m_w_out, m_mem_norm_g, m_xattn_norm_g, m_xattn_wq, m_xattn_wkv, m_xattn_q_norm_g, m_xattn_k_norm_g, m_xattn_wo, m_ffn_norm_g, m_ffn_w_up, m_ffn_conv_w, m_ffn_conv_b, m_ffn_w_down, v_mix_norm_g, v_w_in, v_fox_fb, v_fox_q_norm_g, v_fox_k_norm_g, v_gdn_conv_w, v_gdn_a_log, v_gdn_dt_bias, v_gdn_out_norm_g, v_conf_dw_w, v_conf_dw_b, v_conf_ln_g, v_conf_ln_b, v_w_branch, v_w_out, v_mem_norm_g, v_xattn_norm_g, v_xattn_wq, v_xattn_wkv, v_xattn_q_norm_g, v_xattn_k_norm_g, v_xattn_wo, v_ffn_norm_g, v_ffn_w_up, v_ffn_conv_w, v_ffn_conv_b, v_ffn_w_down):
    args = locals()
    W = {n: args[n] for n in WEIGHTS}
    Mo = {n: args["m_" + n] for n in WEIGHTS}
    Vo = {n: args["v_" + n] for n in WEIGHTS}
    L = mix_norm_g.shape[0]
    H, hd = fox_fb.shape[1], fox_q_norm_g.shape[1]
    dm = Dims(D=x.shape[-1], H=H, hd=hd, MW=H * hd, XH=xattn_wq.shape[-1] // hd, FF=ffn_conv_b.shape[-1])

    full = {n: (_gather_weight(n, W[n]) if n in COL_SHARDED + ROW_SHARDED else W[n]) for n in WEIGHTS}
    layers = []
    for l in range(L):
        P = {n: full[n][l] for n in WEIGHTS if n != "mem_norm_g"}
        P["w_in"] = permute_in_cols(P["w_in"], dm)
        layers.append(P)

    loss, dx, grads, d_mem_g = local_step(x[0], mem[0], loss_target[0], layers, mem_norm_g, dm)
    loss = lax.psum(loss, ("x", "y", "c"))

    def whole(n):
        return d_mem_g if n == "mem_norm_g" else jnp.stack([g[n] for g in grads])

    out = {}
    for n in COL_SHARDED + ROW_SHARDED:
        parts = all_to_all(_scatter_grad(n, whole(n)), "exchange_" + n)
        out[n] = adamw(parts, W[n], Mo[n], Vo[n], "adamw_" + n)
    rep = [whole(n) for n in REPLICATED]
    parts = all_gather(_pack(rep), "gather_small_grads")
    packed = adamw(parts, _pack([W[n] for n in REPLICATED]), _pack([Mo[n] for n in REPLICATED]),
                   _pack([Vo[n] for n in REPLICATED]), "adamw_small")
    unpacked = [_unpack(pk, rep) for pk in packed]
    for i, n in enumerate(REPLICATED):
        out[n] = tuple(u[i] for u in unpacked)

    return (loss, dx[None], *[out[n][0] for n in WEIGHTS], *[out[n][1] for n in WEIGHTS],
            *[out[n][2] for n in WEIGHTS], *[out[n][3] for n in WEIGHTS])
```

```python
import functools
import math
from typing import NamedTuple

import jax
import jax.numpy as jnp
from jax import lax
from jax.experimental import pallas as pl
from jax.experimental.pallas import tpu as pltpu

F32 = jnp.float32
BF16 = jnp.bfloat16
N_DEV = 8
LANES = 128
VMEM_LIMIT = 52 << 20
HI = lax.Precision.HIGHEST

ADAM_LR = 0.001
ADAM_B1 = 0.9
ADAM_B2 = 0.999
ADAM_EPS = 1e-08
ADAM_WD = 0.01
ADAM_STEP = 10

GDN_CHUNK = 64
CONF_HALO = 32
SHORT_HALO = 8
ADAMW_BLOCK_ELEMS = 128 * 1024


def _cparams(sem):
    return pltpu.CompilerParams(dimension_semantics=sem, vmem_limit_bytes=VMEM_LIMIT)


def _pick(n, cands):
    for c in cands:
        if c <= n and n % c == 0:
            return c
    return n


def _peer(k):
    x, y, c = lax.axis_index("x"), lax.axis_index("y"), lax.axis_index("c")
    return (x ^ ((k >> 2) & 1), y ^ ((k >> 1) & 1), c ^ (k & 1))


def _me():
    return 4 * lax.axis_index("x") + 2 * lax.axis_index("y") + lax.axis_index("c")


_HBM = pl.BlockSpec(memory_space=pltpu.HBM)


def _exchange(x, name, gather):
    shape = x.shape if not gather else (N_DEV,) + x.shape

    def body(x_ref, out_ref, send_sems, recv_sems, local_sem):
        me = _me()
        own = pltpu.make_async_copy(x_ref if gather else x_ref.at[me], out_ref.at[me], local_sem)
        own.start()
        sends = []
        for k in range(1, N_DEV):
            peer = me ^ k
            cp = pltpu.make_async_remote_copy(
                src_ref=x_ref if gather else x_ref.at[peer], dst_ref=out_ref.at[me],
                send_sem=send_sems.at[k - 1], recv_sem=recv_sems.at[k - 1],
                device_id=_peer(k), device_id_type=pl.DeviceIdType.MESH)
            cp.start()
            sends.append(cp)
        for cp in sends:
            cp.wait_send()
        for k in range(1, N_DEV):
            peer = me ^ k
            pltpu.make_async_remote_copy(
                src_ref=x_ref if gather else x_ref.at[peer], dst_ref=out_ref.at[peer],
                send_sem=send_sems.at[k - 1], recv_sem=recv_sems.at[k - 1],
                device_id=_peer(k), device_id_type=pl.DeviceIdType.MESH).wait_recv()
        own.wait()

    return pl.pallas_call(
        body, name=name, out_shape=jax.ShapeDtypeStruct(shape, x.dtype),
        in_specs=[_HBM], out_specs=_HBM,
        scratch_shapes=[pltpu.SemaphoreType.DMA((N_DEV - 1,)), pltpu.SemaphoreType.DMA((N_DEV - 1,)),
                        pltpu.SemaphoreType.DMA(())],
    )(x)


def all_gather(x, name):
    return _exchange(x, name, True)


def all_to_all(x, name):
    return _exchange(x, name, False)


_SEM = pl.BlockSpec(memory_space=pltpu.SEMAPHORE)
_ANY = pl.BlockSpec(memory_space=pl.ANY)
_DATAFLOW = pltpu.SideEffectType.DATAFLOW_SIDE_EFFECTING


def _split_copy(k, gather, x_ref, land_ref, send_sems, recv_sems, incoming):
    me = _me()
    peer = me ^ k
    return pltpu.make_async_remote_copy(
        src_ref=x_ref if gather else x_ref.at[peer], dst_ref=land_ref.at[peer if incoming else me],
        send_sem=send_sems.at[k - 1], recv_sem=recv_sems.at[k - 1],
        device_id=_peer(k), device_id_type=pl.DeviceIdType.MESH)


def exchange_start(x, name, gather, after=None):
    land_shape = ((N_DEV,) + x.shape) if gather else x.shape

    def body(x_ref, land_ref, *rest):
        send_sems, recv_sems, _, _, token = rest[-5:]
        for k in range(1, N_DEV):
            _split_copy(k, gather, x_ref, land_ref, send_sems, recv_sems, False).start()
        token[...] = jnp.zeros_like(token)

    sems = pltpu.SemaphoreType.DMA((N_DEV - 1,))
    operands = [pltpu.with_memory_space_constraint(x, pltpu.HBM),
                pltpu.with_memory_space_constraint(lax.empty(land_shape, x.dtype), pltpu.HBM)]
    outs = pl.pallas_call(
        body, name=name,
        out_shape=(sems, sems, pltpu.HBM(x.shape, x.dtype), pltpu.HBM(land_shape, x.dtype),
                   jax.ShapeDtypeStruct((8, LANES), F32)),
        in_specs=[_HBM, _HBM] + ([_ANY] if after is not None else []),
        out_specs=(_SEM, _SEM, _HBM, _HBM, pl.BlockSpec(memory_space=pltpu.VMEM)),
        input_output_aliases={0: 2, 1: 3},
        compiler_params=pltpu.CompilerParams(has_side_effects=_DATAFLOW),
    )(*operands, *([after] if after is not None else []))
    return tuple(outs[:4]), outs[4]


def place_own(land, src, name, gather):
    def body(land_ref, src_ref, out_ref, sem):
        me = _me()
        cp = pltpu.make_async_copy(src_ref if gather else src_ref.at[me], out_ref.at[me], sem)
        cp.start()
        cp.wait()

    return pl.pallas_call(
        body, name=name, out_shape=jax.ShapeDtypeStruct(land.shape, land.dtype),
        in_specs=[_HBM, _HBM], out_specs=_HBM, scratch_shapes=[pltpu.SemaphoreType.DMA(())],
        input_output_aliases={0: 0},
    )(land, src)


def exchange_wait(handle, name, gather, after):
    send_sems, recv_sems, x_thru, land_thru = handle

    def body(x_ref, land_ref, send_sems, recv_sems, after_ref, x_out, land_out):
        for k in range(1, N_DEV):
            _split_copy(k, gather, x_ref, land_ref, send_sems, recv_sems, False).wait_send()
        for k in range(1, N_DEV):
            _split_copy(k, gather, x_ref, land_ref, send_sems, recv_sems, True).wait_recv()

    return pl.pallas_call(
        body, name=name,
        out_shape=(pltpu.HBM(x_thru.shape, x_thru.dtype), pltpu.HBM(land_thru.shape, land_thru.dtype)),
        in_specs=[_HBM, _HBM, _SEM, _SEM, _ANY], out_specs=(_HBM, _HBM), input_output_aliases={0: 0, 1: 1},
        compiler_params=pltpu.CompilerParams(has_side_effects=_DATAFLOW),
    )(x_thru, land_thru, send_sems, recv_sems, after)[1]


_DIMS = {"nn": (((1,), (0,)), ((), ())), "nt": (((1,), (1,)), ((), ())), "tn": (((0,), (0,)), ((), ()))}


V7X_MXU_FLOPS = 9.0e14
V7X_HBM_BYTES_PER_S = 3.0e12
V7X_VMEM_STORE_BYTES_PER_S = 4.0e12
GRID_STEP_S = 0.35e-6
MATMUL_VMEM_BUDGET = 40 << 20


def _matmul_tiles(M, N, K, a_bytes, b_bytes, has_add, tm_on_lanes):
    def divisors(n, cands):
        got = [c for c in cands if c <= n and n % c == 0]
        return got or [n]

    best = None
    for tn in divisors(N, (2048, 1408, 1024, 512, 256, 128)):
        for tm in divisors(M, (2048, 1408, 1024, 512, 256, 128) + (() if tm_on_lanes else (64, 32, 16, 8))):
            for tk in divisors(K, (2816, 2048, 1408, 1024, 512, 256, 128)):
                nk = K // tk
                vmem = 2 * (tm * tk * a_bytes + tk * tn * b_bytes + tm * tn * 4 * (2 if has_add else 1))
                vmem += tm * tn * 4 * (2 if nk > 1 else 1)
                if vmem > MATMUL_VMEM_BUDGET:
                    continue
                a_reads = (N // tn) if nk > 1 else 1
                hbm = M * K * a_bytes * a_reads + K * N * b_bytes * (M // tm) + M * N * 4 * (2 if has_add else 1)
                t_mxu = 2.0 * M * N * K / V7X_MXU_FLOPS + (M * N * nk * 8 / V7X_VMEM_STORE_BYTES_PER_S if nk > 1 else 0.0)
                t = max(t_mxu, hbm / V7X_HBM_BYTES_PER_S) + GRID_STEP_S * (M // tm) * (N // tn) * nk
                if best is None or t < best[0]:
                    best = (t, tm, tn, tk)
    return best[1:]


def matmul(a, b, mode, name, add=None, out_dtype=F32, deps=()):
    if mode == "nn":
        (M, K), N = a.shape, b.shape[1]
    elif mode == "nt":
        (M, K), N = a.shape, b.shape[0]
    else:
        (K, M), N = a.shape, b.shape[1]
    tm, tn, tk = _matmul_tiles(M, N, K, a.dtype.itemsize, b.dtype.itemsize, add is not None, mode == "tn")
    nk = K // tk
    if mode == "nn":
        a_spec = pl.BlockSpec((tm, tk), lambda i, j, k: (i, k))
        b_spec = pl.BlockSpec((tk, tn), lambda i, j, k: (k, j))
    elif mode == "nt":
        a_spec = pl.BlockSpec((tm, tk), lambda i, j, k: (i, k))
        b_spec = pl.BlockSpec((tn, tk), lambda i, j, k: (j, k))
    else:
        a_spec = pl.BlockSpec((tk, tm), lambda i, j, k: (k, i))
        b_spec = pl.BlockSpec((tk, tn), lambda i, j, k: (k, j))
    o_spec = pl.BlockSpec((tm, tn), lambda i, j, k: (i, j))
    dims = _DIMS[mode]

    def body(*refs):
        a_ref, b_ref = refs[:2]
        add_ref = refs[2] if add is not None else None
        o_ref = refs[(3 if add is not None else 2) + len(deps)]
        part = lax.dot_general(a_ref[...].astype(BF16), b_ref[...].astype(BF16), dims, preferred_element_type=F32)

        def finish(r):
            if add is not None:
                r = r + add_ref[...]
            o_ref[...] = r.astype(o_ref.dtype)

        if nk == 1:
            finish(part)
            return
        acc_ref = refs[-1]
        k = pl.program_id(2)

        @pl.when(k == 0)
        def _():
            acc_ref[...] = part

        @pl.when(k > 0)
        def _():
            acc_ref[...] += part

        @pl.when(k == nk - 1)
        def _():
            finish(acc_ref[...])

    ins = ([a, b] if add is None else [a, b, add]) + list(deps)
    specs = ([a_spec, b_spec] if add is None else [a_spec, b_spec, o_spec]) + [_ANY] * len(deps)
    return pl.pallas_call(
        body, name=name, grid=(M // tm, N // tn, nk), in_specs=specs, out_specs=o_spec,
        out_shape=jax.ShapeDtypeStruct((M, N), out_dtype),
        scratch_shapes=[pltpu.VMEM((tm, tn), F32)] if nk > 1 else [],
        compiler_params=_cparams(("parallel", "parallel", "arbitrary")),
    )(*ins)


class Row(NamedTuple):
    arr: jax.Array
    cb: int
    cw: int
    halo: int = 0


def _row_vals(refs, rows, first):
    vals, it = [], iter(refs)
    for r in rows:
        cur = next(it)[...].astype(F32)
        if r.halo:
            prev = next(it)[...].astype(F32)
            prev = jnp.where(first, jnp.zeros_like(prev), prev)
            cur = jnp.concatenate([prev, cur], axis=0)
        vals.append(cur)
    return vals


def _row_specs(rows, tr, rev_nt=None):
    specs = []
    for r in rows:
        def cur_map(c, i, r=r):
            return ((rev_nt - 1 - i) if rev_nt else i, r.cb + c)
        specs.append(pl.BlockSpec((tr, r.cw), cur_map))
        if r.halo:
            q = tr // r.halo

            def prev_map(c, i, r=r, q=q):
                t = (rev_nt - 1 - i) if rev_nt else i
                return (jnp.maximum(t * q - 1, 0), r.cb + c)
            specs.append(pl.BlockSpec((r.halo, r.cw), prev_map))
    return specs


def _row_args(rows):
    args = []
    for r in rows:
        args.append(r.arr)
        if r.halo:
            args.append(r.arr)
    return args


def _param_specs(params, ncb):
    return [pl.BlockSpec((p.shape[0], p.shape[1] // ncb), lambda c, i: (0, c)) for p in params]


def tile_fwd(name, f, rows, params, outs, tr, ncb=1):
    S = rows[0].arr.shape[0]
    nt = S // tr
    n_in = sum(2 if r.halo else 1 for r in rows)

    def body(*refs):
        first = pl.program_id(1) == 0
        rv = _row_vals(refs[:n_in], rows, first)
        pv = [p[...] for p in refs[n_in:n_in + len(params)]]
        res = f(rv, pv)
        for o_ref, o in zip(refs[n_in + len(params):], res):
            o_ref[...] = o.astype(o_ref.dtype)

    return pl.pallas_call(
        body, name=name, grid=(ncb, nt),
        in_specs=_row_specs(rows, tr) + _param_specs(params, ncb),
        out_specs=[pl.BlockSpec((tr, cw), lambda c, i: (i, c)) for cw, _ in outs],
        out_shape=[jax.ShapeDtypeStruct((S, cw * ncb), dt) for cw, dt in outs],
        compiler_params=_cparams(("parallel", "parallel")),
    )(*_row_args(rows), *params)


def tile_bwd(name, f, rows, params, couts, tr, ncb=1, adds=None):
    S = rows[0].arr.shape[0]
    nt = S // tr
    n_in = sum(2 if r.halo else 1 for r in rows)
    adds = adds or [None] * len(rows)
    add_list = [a for a in adds if a is not None]
    n_p, n_c, n_a, n_r = len(params), len(couts), len(add_list), len(rows)
    halos = [r for r in rows if r.halo]

    def body(*refs):
        i = pl.program_id(1)
        first = i == nt - 1
        pos = 0
        in_refs = refs[pos:pos + n_in]; pos += n_in
        p_refs = refs[pos:pos + n_p]; pos += n_p
        c_refs = refs[pos:pos + n_c]; pos += n_c
        a_refs = list(refs[pos:pos + n_a]); pos += n_a
        dr_refs = refs[pos:pos + n_r]; pos += n_r
        dp_refs = refs[pos:pos + n_p]; pos += n_p
        carry_refs = list(refs[pos:])
        rv = _row_vals(in_refs, rows, first)
        pv = [p[...] for p in p_refs]
        _, vjp = jax.vjp(lambda rv_, pv_: f(rv_, pv_), rv, pv)
        drv, dpv = vjp([c[...].astype(F32) for c in c_refs])
        for r, d, d_ref, a in zip(rows, drv, dr_refs, adds):
            a_val = a_refs.pop(0)[...] if a is not None else None
            if r.halo:
                carry = carry_refs.pop(0)
                cur = d[r.halo:]
                if a_val is not None:
                    cur = cur + a_val
                d_ref[...] = cur

                @pl.when(i > 0)
                def _(d_ref=d_ref, carry=carry, r=r):
                    d_ref[pl.ds(tr - r.halo, r.halo), :] += carry[...]

                carry[...] = d[:r.halo]
            else:
                d_ref[...] = d if a_val is None else d + a_val

        for dp_ref, dp in zip(dp_refs, dpv):
            @pl.when(i == 0)
            def _(dp_ref=dp_ref):
                dp_ref[...] = jnp.zeros_like(dp_ref)

            dp_ref[...] += dp

    rev = lambda c, i: (nt - 1 - i, c)
    return_vals = pl.pallas_call(
        body, name=name, grid=(ncb, nt),
        in_specs=(_row_specs(rows, tr, rev_nt=nt) + _param_specs(params, ncb)
                  + [pl.BlockSpec((tr, c.shape[1] // ncb), rev) for c in couts]
                  + [pl.BlockSpec((tr, a.shape[1] // ncb), rev) for a in add_list]),
        out_specs=([pl.BlockSpec((tr, r.cw), rev) for r in rows] + _param_specs(params, ncb)),
        out_shape=([jax.ShapeDtypeStruct((S, r.cw * ncb), F32) for r in rows]
                   + [jax.ShapeDtypeStruct(p.shape, F32) for p in params]),
        scratch_shapes=[pltpu.VMEM((r.halo, r.cw), F32) for r in halos],
        compiler_params=_cparams(("parallel", "arbitrary")),
    )(*_row_args(rows), *params, *couts, *add_list)
    return list(return_vals[:n_r]), list(return_vals[n_r:])


def adamw(parts, w, m, v, name, layer=None, into=None):
    shape = w.shape
    C = shape[-1]
    R = math.prod(shape[:-1])
    rows = R if layer is None else R // shape[0]
    parts2, w2, m2, v2 = parts.reshape(N_DEV, rows, C), w.reshape(R, C), m.reshape(R, C), v.reshape(R, C)
    lanes = -(-C // LANES) * LANES
    tr = _pick(rows, [t for t in (1024, 512, 256, 128, 64, 32, 16, 8) if t * lanes <= ADAMW_BLOCK_ELEMS])
    first = 0 if layer is None else layer * (rows // tr)

    def body(p_ref, w_ref, m_ref, v_ref, *rest):
        g_out, d_out, m_out, v_out = rest[-4:]
        g = p_ref[0].astype(F32)
        for d in range(1, N_DEV):
            g = g + p_ref[d].astype(F32)
        mm = ADAM_B1 * m_ref[...] + (1.0 - ADAM_B1) * g
        vv = ADAM_B2 * v_ref[...] + (1.0 - ADAM_B2) * jnp.square(g)
        m_hat = mm / (1.0 - ADAM_B1 ** ADAM_STEP)
        v_hat = vv / (1.0 - ADAM_B2 ** ADAM_STEP)
        g_out[...] = g
        d_out[...] = -ADAM_LR * (m_hat / (jnp.sqrt(v_hat) + ADAM_EPS) + ADAM_WD * w_ref[...])
        m_out[...] = mm
        v_out[...] = vv

    blk = pl.BlockSpec((tr, C), lambda i: (first + i, 0))
    prev = [] if into is None else [a.reshape(R, C) for a in into]
    outs = pl.pallas_call(
        body, name=name, grid=(rows // tr,),
        in_specs=[pl.BlockSpec((N_DEV, tr, C), lambda i: (0, i, 0)), blk, blk, blk] + [_ANY] * len(prev),
        out_specs=[blk] * 4, out_shape=[jax.ShapeDtypeStruct((R, C), F32)] * 4,
        input_output_aliases={4 + j: j for j in range(len(prev))},
        compiler_params=_cparams(("parallel",)),
    )(parts2, w2, m2, v2, *prev)
    return tuple(o.reshape(shape) for o in outs)


def _dot(a, b, mode):
    return lax.dot_general(a.astype(BF16), b.astype(BF16), _DIMS[mode], preferred_element_type=F32)


@jax.custom_vjp
def mm_nn(a, b):
    return _dot(a, b, "nn")


@jax.custom_vjp
def mm_nt(a, b):
    return _dot(a, b, "nt")


@jax.custom_vjp
def mm_tn(a, b):
    return _dot(a, b, "tn")


mm_nn.defvjp(lambda a, b: (_dot(a, b, "nn"), (a, b)), lambda r, g: (mm_nt(g, r[1]), mm_tn(r[0], g)))
mm_nt.defvjp(lambda a, b: (_dot(a, b, "nt"), (a, b)), lambda r, g: (mm_nn(g, r[1]), mm_tn(g, r[0])))
mm_tn.defvjp(lambda a, b: (_dot(a, b, "tn"), (a, b)), lambda r, g: (mm_nt(r[1], g), mm_nn(r[0], g)))


def _mmh(a, b):
    return jnp.dot(a, b, precision=HI, preferred_element_type=F32)


def _dot3(a, b, mode):
    ah, bh = a.astype(BF16), b.astype(BF16)
    al, bl = (a - ah.astype(F32)).astype(BF16), (b - bh.astype(F32)).astype(BF16)

    def d(x, y):
        return lax.dot_general(x, y, _DIMS[mode], preferred_element_type=F32)
    return d(ah, bh) + (d(ah, bl) + d(al, bh))


@jax.custom_vjp
def mm3_nn(a, b):
    return _dot3(a, b, "nn")


mm3_nn.defvjp(lambda a, b: (_dot3(a, b, "nn"), (a, b)), lambda r, g: (_dot3(g, r[1], "nt"), _dot3(r[0], g, "tn")))


def _sigmoid(x):
    return jax.nn.sigmoid(x)


def _silu(x):
    return x * jax.nn.sigmoid(x)


def _softplus(x):
    return jnp.maximum(x, 0.0) + jnp.log(1.0 + jnp.exp(-jnp.abs(x)))


def _log_sigmoid(x):
    return jnp.minimum(x, 0.0) - jnp.log(1.0 + jnp.exp(-jnp.abs(x)))


def _rms(x, g, eps=1e-6):
    return x * lax.rsqrt(jnp.mean(x * x, axis=-1, keepdims=True) + eps) * g


def _heads(fn, x, hd):
    return jnp.concatenate([fn(x[:, h * hd:(h + 1) * hd]) for h in range(x.shape[1] // hd)], axis=1)


def _causal_conv(x, w, halo, tr):
    K = w.shape[0]
    acc = jnp.zeros((tr, x.shape[1]), F32)
    for k in range(K):
        o = halo - (K - 1) + k
        acc = acc + w[k:k + 1] * x[o:o + tr]
    return acc


def _lanes(shape):
    return lax.broadcasted_iota(jnp.int32, shape, len(shape) - 1)


def f_rms(rv, pv):
    return [_rms(rv[0], pv[0])]


def make_f_fox_prep(H, hd):
    def f(rv, pv):
        fq, fk, fv, small = rv
        fb, gq, gk = pv
        qn = _heads(lambda t: _rms(t, gq), fq, hd) * (hd ** -0.5)
        kn = _heads(lambda t: _rms(t, gk), fk, hd)
        logf = jnp.where(_lanes(small.shape) < H, _log_sigmoid(small + fb), 0.0)
        return [qn, kn, fv, logf]
    return f


def make_f_short_conv(tr, act):
    def f(rv, pv):
        y = _causal_conv(rv[0], pv[0], SHORT_HALO, tr)
        return [_silu(y) if act else y]
    return f


def make_f_gdn_prep(H, hd):
    def l2(t):
        return t * lax.rsqrt(jnp.sum(t * t, axis=-1, keepdims=True) + 1e-6)

    def f(rv, pv):
        q, k, small = rv
        alog, dtb = pv
        qn = _heads(l2, q, hd) * (hd ** -0.5)
        kn = _heads(l2, k, hd)
        ln = _lanes(small.shape)
        g = -jnp.exp(alog) * _softplus(small + dtb)
        gates = jnp.where((ln >= H) & (ln < 2 * H), g, jnp.where((ln >= 2 * H) & (ln < 3 * H), _sigmoid(small), 0.0))
        return [qn, kn, gates]
    return f


def _tri(n, strict=False, upper=False):
    ii = lax.broadcasted_iota(jnp.int32, (n, n), 0)
    jj = lax.broadcasted_iota(jnp.int32, (n, n), 1)
    if upper:
        ii, jj = jj, ii
    return ii > jj if strict else ii >= jj


def _decay(gcol, mask):
    C = gcol.shape[0]
    G = jnp.broadcast_to(gcol, (C, C))
    return jnp.where(mask, jnp.exp(jnp.where(mask, G - G.T, 0.0)), 0.0)


def _nilpotent_inverses(Xs):
    C = Xs[0].shape[0]
    eye = (lax.broadcasted_iota(jnp.int32, (C, C), 0) == lax.broadcasted_iota(jnp.int32, (C, C), 1)).astype(F32)
    Ts, Ps = [eye + X for X in Xs], list(Xs)
    for _ in range(int(math.log2(C)) - 1):
        Ps = [mm3_nn(P, P) for P in Ps]
        Ts = [T + mm3_nn(T, P) for T, P in zip(Ts, Ps)]
    return Ts


@jax.custom_vjp
def _saved_inverse(X, T):
    return T


_saved_inverse.defvjp(lambda X, T: (T, T),
                      lambda T, g: (_dot3(T, _dot3(g, T, "nt"), "tn"), jnp.zeros_like(T)))


def make_f_gdn_solve(H, hd, saved):
    C = GDN_CHUNK

    def f(rv, pv):
        k, v, gates = rv[:3]
        gc = _mmh(_tri(C).astype(F32), gates)
        strict = _tri(C, strict=True)
        heads = range(H)
        ks = [k[:, h * hd:(h + 1) * hd] for h in heads]
        gcols = [gc[:, H + h:H + h + 1] for h in heads]
        betas = [gates[:, 2 * H + h:2 * H + h + 1] for h in heads]
        kbs = [kh * b for kh, b in zip(ks, betas)]
        vbs = [v[:, h * hd:(h + 1) * hd] * betas[h] for h in heads]
        Xs = [-(mm_nt(kb, kh) * _decay(g, strict)) for kb, kh, g in zip(kbs, ks, gcols)]
        if saved:
            Ts = [_saved_inverse(X, rv[3][:, h * C:(h + 1) * C]) for h, X in enumerate(Xs)]
        else:
            Ts = _nilpotent_inverses(Xs)
        us = [mm3_nn(T, vb) for T, vb in zip(Ts, vbs)]
        ws = [mm3_nn(T, kb * jnp.exp(g)) for T, kb, g in zip(Ts, kbs, gcols)]
        out = [jnp.concatenate(us, axis=1), jnp.concatenate(ws, axis=1), gc]
        return out if saved else out + [jnp.concatenate(Ts, axis=1)]
    return f


def _gdn_steps(Ss, qs, ks, us, ws, gcols):
    C = qs[0].shape[0]
    causal = _tri(C)
    rows = lax.broadcasted_iota(jnp.int32, gcols[0].shape, 0)
    attn = [mm_nt(q, k) * _decay(g, causal) for q, k, g in zip(qs, ks, gcols)]
    glast = [jnp.sum(jnp.where(rows == C - 1, g, 0.0), axis=0, keepdims=True) for g in gcols]
    v_new = [u - mm_nn(w, S) for u, w, S in zip(us, ws, Ss)]
    o_state = [mm_nn(q * jnp.exp(g), S) for q, g, S in zip(qs, gcols, Ss)]
    o_chunk = [mm_nn(a, vn) for a, vn in zip(attn, v_new)]
    update = [mm_tn(k * jnp.exp(gl - g), vn) for k, gl, g, vn in zip(ks, glast, gcols, v_new)]
    S_new = [S * jnp.exp(gl) + d for S, gl, d in zip(Ss, glast, update)]
    return [a + b for a, b in zip(o_state, o_chunk)], S_new


def make_f_gdn_out(hd):
    def f(rv, pv):
        o, gz = rv
        return [_heads(lambda t: _rms(t, pv[0]), o, hd) * _silu(gz)]
    return f


def make_f_conf(tr):
    def f(rv, pv):
        a, g = rv
        w, b, lg, lb = pv
        y = _causal_conv(a * _sigmoid(g), w, CONF_HALO, tr) + b
        xc = y - jnp.mean(y, axis=-1, keepdims=True)
        y = xc * lax.rsqrt(jnp.mean(xc * xc, axis=-1, keepdims=True) + 1e-5) * lg + lb
        return [_silu(y)]
    return f


def f_merge(rv, pv):
    n = len(rv) // 2
    y = _sigmoid(rv[0]) * rv[n]
    for j in range(1, n):
        y = y + _sigmoid(rv[j]) * rv[n + j]
    return [y]


def make_f_xattn(XH, hd):
    XW = XH * hd

    def f(rv, pv):
        q = rv[0]
        kv, gq, gk = pv
        outs = []
        for h in range(XH):
            qh = _rms(q[:, h * hd:(h + 1) * hd], gq)
            kh = _rms(kv[:, h * hd:(h + 1) * hd], gk)
            s = mm_nt(qh, kh) * (hd ** -0.5)
            e = jnp.exp(s - jnp.max(s, axis=-1, keepdims=True))
            outs.append(mm_nn(e / jnp.sum(e, axis=-1, keepdims=True), kv[:, XW + h * hd:XW + (h + 1) * hd]))
        return [jnp.concatenate(outs, axis=1)]
    return f


def make_f_ffn_act(tr):
    def f(rv, pv):
        a, v = rv
        return [_silu(_causal_conv(a, pv[0], SHORT_HALO, tr) + pv[1]) * v]
    return f


def make_f_dprep(H, hd):
    def f(rv, pv):
        do, o = rv
        ind = (lax.broadcasted_iota(jnp.int32, (H * hd, LANES), 0) // hd
               == lax.broadcasted_iota(jnp.int32, (H * hd, LANES), 1)).astype(F32)
        return [do, _mmh(do * o, ind)]
    return f


def cumsum_rows(x, name, reverse=False):
    S, C = x.shape
    tr = _pick(S, (256, 128, 64))
    nt = S // tr

    def body(x_ref, o_ref, carry):
        @pl.when(pl.program_id(0) == 0)
        def _():
            carry[...] = jnp.zeros_like(carry)

        ii = lax.broadcasted_iota(jnp.int32, (tr, tr), 0)
        jj = lax.broadcasted_iota(jnp.int32, (tr, tr), 1)
        tri = (ii <= jj) if reverse else (ii >= jj)
        y = _mmh(tri.astype(F32), x_ref[...]) + carry[...]
        o_ref[...] = y
        rows = lax.broadcasted_iota(jnp.int32, y.shape, 0)
        carry[...] = jnp.sum(jnp.where(rows == (0 if reverse else tr - 1), y, 0.0), axis=0, keepdims=True)

    spec = pl.BlockSpec((tr, C), (lambda i: (nt - 1 - i, 0)) if reverse else (lambda i: (i, 0)))
    return pl.pallas_call(
        body, name=name, grid=(nt,), in_specs=[spec], out_specs=spec,
        out_shape=jax.ShapeDtypeStruct((S, C), F32), scratch_shapes=[pltpu.VMEM((1, C), F32)],
        compiler_params=_cparams(("arbitrary",)),
    )(x)


def _fox_block(S):
    return _pick(S, (1024, 512, 256, 128))


def fox_fwd(q, k, v, frow, H, hd, name):
    S = q.shape[0]
    bq = _fox_block(S)
    nq = S // bq

    def body(q_ref, k_ref, v_ref, fr_ref, o_ref, lse_ref):
        i = pl.program_id(1)
        qv = q_ref[...]

        def step(j, carry, diag):
            m, l, acc = carry
            cols = pl.ds(pl.multiple_of(j * bq, bq), bq)
            kj, vj = k_ref[cols, :], v_ref[cols, :]
            s = lax.dot_general(qv, kj, _DIMS["nt"], preferred_element_type=F32) - fr_ref[0, :, cols]
            if diag:
                s = jnp.where(_tri(bq), s, -jnp.inf)
            m_new = jnp.maximum(m, jnp.max(s, axis=-1, keepdims=True))
            alpha = jnp.exp(m - m_new)
            p = jnp.exp(s - m_new)
            l = alpha * l + jnp.sum(p, axis=-1, keepdims=True)
            acc = alpha * acc + lax.dot_general(p.astype(BF16), vj, _DIMS["nn"], preferred_element_type=F32)
            return m_new, l, acc

        init = (jnp.full((bq, 1), -jnp.inf, F32), jnp.zeros((bq, 1), F32), jnp.zeros((bq, hd), F32))
        carry = lax.fori_loop(0, i, lambda j, c: step(j, c, False), init)
        m, l, acc = step(i, carry, True)
        o_ref[...] = acc / l
        lse_ref[0] = m + jnp.log(l)

    return pl.pallas_call(
        body, name=name, grid=(H, nq),
        in_specs=[pl.BlockSpec((bq, hd), lambda h, i: (i, h)),
                  pl.BlockSpec((S, hd), lambda h, i: (0, h)), pl.BlockSpec((S, hd), lambda h, i: (0, h)),
                  pl.BlockSpec((1, 1, S), lambda h, i: (h, 0, 0))],
        out_specs=[pl.BlockSpec((bq, hd), lambda h, i: (i, h)), pl.BlockSpec((1, bq, 1), lambda h, i: (h, i, 0))],
        out_shape=[jax.ShapeDtypeStruct((S, H * hd), F32), jax.ShapeDtypeStruct((H, S, 1), F32)],
        compiler_params=_cparams(("parallel", "parallel")),
    )(q, k, v, frow)


def fox_bwd(q, k, v, do, fcol, lse_row, delta_row, H, hd, name):
    S = q.shape[0]
    bk = _fox_block(S)
    nk = S // bk

    def body(q_ref, do_ref, k_ref, v_ref, fc_ref, lse_ref, dl_ref, dq_ref, dfq_ref, dk_ref, dv_ref, df_ref):
        j = pl.program_id(1)

        @pl.when(j == 0)
        def _():
            dq_ref[...] = jnp.zeros_like(dq_ref)
            dfq_ref[...] = jnp.zeros_like(dfq_ref)

        kj, vj = k_ref[...], v_ref[...]
        fk = fc_ref[0]

        def step(i, carry, diag):
            dk, dv, df = carry
            rows = pl.ds(pl.multiple_of(i * bk, bk), bk)
            qi, doi = q_ref[rows, :], do_ref[rows, :]
            st = lax.dot_general(kj, qi, _DIMS["nt"], preferred_element_type=F32) - fk - lse_ref[0, :, rows]
            if diag:
                st = jnp.where(_tri(bk, upper=True), st, -jnp.inf)
            pt = jnp.exp(st)
            dv = dv + lax.dot_general(pt.astype(BF16), doi, _DIMS["nn"], preferred_element_type=F32)
            dpt = lax.dot_general(vj, doi, _DIMS["nt"], preferred_element_type=F32)
            dst = pt * (dpt - dl_ref[0, :, rows])
            df = df - jnp.sum(dst, axis=-1, keepdims=True)
            dfq_ref[0, :, rows] += jnp.sum(dst, axis=0, keepdims=True)
            dsb = dst.astype(BF16)
            dk = dk + lax.dot_general(dsb, qi, _DIMS["nn"], preferred_element_type=F32)
            dq_ref[rows, :] += lax.dot_general(dsb, kj, _DIMS["tn"], preferred_element_type=F32)
            return dk, dv, df

        init = (jnp.zeros((bk, hd), F32), jnp.zeros((bk, hd), F32), jnp.zeros((bk, 1), F32))
        carry = step(j, init, True)
        dk, dv, df = lax.fori_loop(j + 1, nk, lambda i, c: step(i, c, False), carry)
        dk_ref[...] = dk
        dv_ref[...] = dv
        df_ref[0] = df

    whole = pl.BlockSpec((S, hd), lambda h, j: (0, h))
    blk = pl.BlockSpec((bk, hd), lambda h, j: (j, h))
    row = pl.BlockSpec((1, 1, S), lambda h, j: (h, 0, 0))
    col = pl.BlockSpec((1, bk, 1), lambda h, j: (h, j, 0))
    return pl.pallas_call(
        body, name=name, grid=(H, nk),
        in_specs=[whole, whole, blk, blk, col, row, row],
        out_specs=[whole, row, blk, blk, col],
        out_shape=[jax.ShapeDtypeStruct((S, H * hd), F32), jax.ShapeDtypeStruct((H, 1, S), F32)]
        + [jax.ShapeDtypeStruct((S, H * hd), F32)] * 2 + [jax.ShapeDtypeStruct((H, S, 1), F32)],
        compiler_params=_cparams(("parallel", "arbitrary")),
    )(q, do, k, v, fcol, lse_row, delta_row)


def gdn_scan_fwd(q, k, u, w, gc, H, hd, name):
    S = q.shape[0]
    C = GDN_CHUNK
    NC = S // C

    def body(q_ref, k_ref, u_ref, w_ref, gc_ref, o_ref, sin_ref, state):
        @pl.when(pl.program_id(0) == 0)
        def _():
            state[...] = jnp.zeros_like(state)

        gcv = gc_ref[...]
        sl = [slice(h * hd, (h + 1) * hd) for h in range(H)]
        Ss = [state[h] for h in range(H)]
        for h in range(H):
            sin_ref[0, h] = Ss[h]
        outs, S_new = _gdn_steps(Ss, [q_ref[:, s] for s in sl], [k_ref[:, s] for s in sl], [u_ref[:, s] for s in sl],
                                 [w_ref[:, s] for s in sl], [gcv[:, H + h:H + h + 1] for h in range(H)])
        for h in range(H):
            state[h] = S_new[h]
        o_ref[...] = jnp.concatenate(outs, axis=1)

    wide = pl.BlockSpec((C, H * hd), lambda i: (i, 0))
    return pl.pallas_call(
        body, name=name, grid=(NC,),
        in_specs=[wide] * 4 + [pl.BlockSpec((C, LANES), lambda i: (i, 0))],
        out_specs=[wide, pl.BlockSpec((1, H, hd, hd), lambda i: (i, 0, 0, 0))],
        out_shape=[jax.ShapeDtypeStruct((S, H * hd), F32), jax.ShapeDtypeStruct((NC, H, hd, hd), F32)],
        scratch_shapes=[pltpu.VMEM((H, hd, hd), F32)],
        compiler_params=_cparams(("arbitrary",)),
    )(q, k, u, w, gc)


def gdn_scan_bwd(q, k, u, w, gc, sin, do, H, hd, name):
    S = q.shape[0]
    C = GDN_CHUNK
    NC = S // C

    def body(q_ref, k_ref, u_ref, w_ref, gc_ref, sin_ref, do_ref, dq_ref, dk_ref, du_ref, dw_ref, dgc_ref, dstate):
        @pl.when(pl.program_id(0) == 0)
        def _():
            dstate[...] = jnp.zeros_like(dstate)

        gcv = gc_ref[...]
        ln = _lanes(gcv.shape)
        dgc = jnp.zeros_like(gcv)
        sl = [slice(h * hd, (h + 1) * hd) for h in range(H)]
        _, vjp = jax.vjp(_gdn_steps, [sin_ref[0, h] for h in range(H)], [q_ref[:, s] for s in sl],
                         [k_ref[:, s] for s in sl], [u_ref[:, s] for s in sl], [w_ref[:, s] for s in sl],
                         [gcv[:, H + h:H + h + 1] for h in range(H)])
        dS, dq, dk, du, dw, dg = vjp(([do_ref[:, s] for s in sl], [dstate[h] for h in range(H)]))
        for h in range(H):
            dstate[h] = dS[h]
            dgc = dgc + jnp.where(ln == H + h, dg[h], 0.0)
        for ref, lst in zip((dq_ref, dk_ref, du_ref, dw_ref), (dq, dk, du, dw)):
            ref[...] = jnp.concatenate(lst, axis=1)
        dgc_ref[...] = dgc

    wide = pl.BlockSpec((C, H * hd), lambda i: (NC - 1 - i, 0))
    narrow = pl.BlockSpec((C, LANES), lambda i: (NC - 1 - i, 0))
    return pl.pallas_call(
        body, name=name, grid=(NC,),
        in_specs=[wide] * 4 + [narrow, pl.BlockSpec((1, H, hd, hd), lambda i: (NC - 1 - i, 0, 0, 0)), wide],
        out_specs=[wide] * 4 + [narrow],
        out_shape=[jax.ShapeDtypeStruct((S, H * hd), F32)] * 4 + [jax.ShapeDtypeStruct((S, LANES), F32)],
        scratch_shapes=[pltpu.VMEM((H, hd, hd), F32)],
        compiler_params=_cparams(("arbitrary",)),
    )(q, k, u, w, gc, sin, do)


def loss_head(y, t, name):
    S, D = y.shape
    tr = _pick(S, (256, 128, 64))

    def body(y_ref, t_ref, dy_ref, acc_ref):
        @pl.when(pl.program_id(0) == 0)
        def _():
            acc_ref[...] = jnp.zeros_like(acc_ref)

        err = y_ref[...] - t_ref[...]
        dy_ref[...] = err / D
        acc_ref[...] += jnp.sum(jnp.mean(err * err, axis=-1, keepdims=True), axis=0, keepdims=True)

    blk = pl.BlockSpec((tr, D), lambda i: (i, 0))
    dy, acc = pl.pallas_call(
        body, name=name, grid=(S // tr,), in_specs=[blk, blk],
        out_specs=[blk, pl.BlockSpec((8, LANES), lambda i: (0, 0))],
        out_shape=[jax.ShapeDtypeStruct((S, D), F32), jax.ShapeDtypeStruct((8, LANES), F32)],
        compiler_params=_cparams(("arbitrary",)),
    )(y, t)
    return 0.5 * acc[0, 0], dy


class Dims(NamedTuple):
    D: int
    H: int
    hd: int
    MW: int
    XH: int
    FF: int

    @property
    def n_in(self):
        return 9 * self.MW + 3 * self.H + 3 * self.D

    @property
    def n_in_padded(self):
        return 3 * self.D + 9 * self.MW + LANES


def _in_pieces(dm):
    MW, H, D = dm.MW, dm.H, dm.D
    fq, fk, fv, ff = 0, MW, 2 * MW, 3 * MW
    gq = ff + H
    gk, gv = gq + MW, gq + 2 * MW
    ga = gv + MW
    gb, gz = ga + H, ga + 2 * H
    cu = gz + MW
    gl = cu + 2 * MW
    return [(gl, 3 * D), (fq, MW), (fk, MW), (fv, MW), (gq, MW), (gk, MW), (gv, MW), (gz, MW), (cu, 2 * MW),
            (ff, H), (ga, H), (gb, H)]


def permute_in_cols(w, dm):
    parts = [w[:, s:s + n] for s, n in _in_pieces(dm)]
    parts.append(jnp.zeros((w.shape[0], LANES - 3 * dm.H), w.dtype))
    return jnp.concatenate(parts, axis=1)


def unpermute_in_cols(wp, dm):
    pieces = _in_pieces(dm)
    offs, o = [], 0
    for _, n in pieces:
        offs.append(o)
        o += n
    order = sorted(range(len(pieces)), key=lambda i: pieces[i][0])
    return jnp.concatenate([wp[:, offs[i]:offs[i] + pieces[i][1]] for i in order], axis=1)


def _pad_lanes(v, at):
    return jnp.pad(v, (at, LANES - at - v.shape[0]))[None]


def _ops(x, m, P, dm, t):
    S = x.shape[0]
    tr = _pick(S, (256, 128, 64))
    return dict(tr=tr, trm=_pick(S, (128, 64)), trx=_pick(S, (512, 256, 128, 64)),
                cbq=3 * dm.D // dm.MW, cbs=(3 * dm.D + 9 * dm.MW) // LANES,
                cwf=_pick(dm.FF, (512, 256, 128)))


def layer_fwd(x, m, P, dm, t):
    D, H, hd, MW, XH, FF = dm
    XW = XH * hd
    c = _ops(x, m, P, dm, t)
    tr, cbq, cbs = c["tr"], c["cbq"], c["cbs"]
    (h,) = tile_fwd(t + "mix_norm", f_rms, [Row(x, 0, D)], [P["mix_norm_g"][None]], [(D, BF16)], tr)
    p = matmul(h, P.big("w_in", h), "nn", t + "in_proj", deps=P.deps())
    fox_rows = [Row(p, cbq, MW), Row(p, cbq + 1, MW), Row(p, cbq + 2, MW), Row(p, cbs, LANES)]
    fox_params = [_pad_lanes(P["fox_fb"], 0), P["fox_q_norm_g"][None], P["fox_k_norm_g"][None]]
    qn, kn, vb, logf = tile_fwd(t + "fox_prep", make_f_fox_prep(H, hd), fox_rows, fox_params,
                                [(MW, BF16)] * 3 + [(LANES, F32)], tr)
    Ft = cumsum_rows(logf, t + "fox_cumsum")[:, :H].T
    fcol, frow = Ft[:, :, None], Ft[:, None, :]
    out_a, lse = fox_fwd(qn, kn, vb, frow, H, hd, t + "fox_attn")
    (qkv,) = tile_fwd(t + "gdn_conv", make_f_short_conv(tr, True), [Row(p, cbq + 3, MW, SHORT_HALO)],
                      [P["gdn_conv_w"]], [(MW, F32)], tr, ncb=3)
    gp_rows = [Row(qkv, 0, MW), Row(qkv, 1, MW), Row(p, cbs, LANES)]
    gp_params = [_pad_lanes(P["gdn_a_log"], H), _pad_lanes(P["gdn_dt_bias"], H)]
    gq, gk, gates = tile_fwd(t + "gdn_prep", make_f_gdn_prep(H, hd), gp_rows, gp_params,
                             [(MW, F32), (MW, F32), (LANES, F32)], tr)
    u, w, gc, tinv = tile_fwd(t + "gdn_solve", make_f_gdn_solve(H, hd, False),
                              [Row(gk, 0, MW), Row(qkv, 2, MW), Row(gates, 0, LANES)], [],
                              [(MW, F32), (MW, F32), (LANES, F32), (H * GDN_CHUNK, F32)], GDN_CHUNK)
    o_g, sin = gdn_scan_fwd(gq, gk, u, w, gc, H, hd, t + "gdn_scan")
    (out_b,) = tile_fwd(t + "gdn_out", make_f_gdn_out(hd), [Row(o_g, 0, MW), Row(p, cbq + 6, MW)],
                        [P["gdn_out_norm_g"][None]], [(MW, F32)], tr)
    conf_params = [P["conf_dw_w"], P["conf_dw_b"][None], P["conf_ln_g"][None], P["conf_ln_b"][None]]
    (out_c,) = tile_fwd(t + "conf", make_f_conf(tr), [Row(p, cbq + 7, MW, CONF_HALO), Row(p, cbq + 8, MW, CONF_HALO)],
                        conf_params, [(MW, F32)], tr)
    branches = [out_a, out_b, out_c]
    proj = [matmul(b, P.big("w_branch", out_a)[n], "nn", t + f"branch{n}") for n, b in enumerate(branches)]
    (y,) = tile_fwd(t + "merge", f_merge, [Row(p, n, D) for n in range(3)] + [Row(pr, 0, D) for pr in proj], [],
                    [(D, BF16)], c["trm"])
    x1 = matmul(y, P.big("w_out", y), "nn", t + "out_proj", add=x)
    (h2,) = tile_fwd(t + "xa_norm", f_rms, [Row(x1, 0, D)], [P["xattn_norm_g"][None]], [(D, BF16)], tr)
    q = matmul(h2, P.big("xattn_wq", h2), "nn", t + "xa_q")
    kv = matmul(m, P.big("xattn_wkv", h2), "nn", t + "xa_kv")
    xa_params = [kv, P["xattn_q_norm_g"][None], P["xattn_k_norm_g"][None]]
    (o_x,) = tile_fwd(t + "xa_attn", make_f_xattn(XH, hd), [Row(q, 0, XW)], xa_params, [(XW, F32)], c["trx"])
    x2 = matmul(o_x, P.big("xattn_wo", o_x), "nn", t + "xa_o", add=x1)
    (h3,) = tile_fwd(t + "ffn_norm", f_rms, [Row(x2, 0, D)], [P["ffn_norm_g"][None]], [(D, BF16)], tr)
    av = matmul(h3, P.big("ffn_w_up", h3), "nn", t + "ffn_up")
    cwf = c["cwf"]
    (uf,) = tile_fwd(t + "ffn_act", make_f_ffn_act(tr), [Row(av, 0, cwf, SHORT_HALO), Row(av, FF // cwf, cwf)],
                     [P["ffn_conv_w"], P["ffn_conv_b"][None]], [(cwf, BF16)], tr, ncb=FF // cwf)
    x3 = matmul(uf, P.big("ffn_w_down", uf), "nn", t + "ffn_down", add=x2)
    res = dict(x=x, h=h, p=p, qn=qn, kn=kn, vb=vb, fcol=fcol, frow=frow, out_a=out_a, lse=lse, qkv=qkv, gq=gq, gk=gk,
               gates=gates, u=u, w=w, gc=gc, tinv=tinv, sin=sin, o_g=o_g, out_b=out_b, out_c=out_c, proj=proj, y=y, x1=x1, h2=h2,
               q=q, kv=kv, o_x=o_x, x2=x2, h3=h3, av=av, uf=uf)
    return x3, res


def layer_bwd(dx3, m, P, R, dm, t):
    D, H, hd, MW, XH, FF = dm
    XW = XH * hd
    c = _ops(R["x"], m, P, dm, t)
    tr, cbq, cbs = c["tr"], c["cbq"], c["cbs"]
    p = R["p"]
    G = {}
    P.emit("ffn_w_down", matmul(R["uf"], dx3, "tn", t + "ffn_down_dw"))
    du = matmul(dx3, P.big("ffn_w_down", None), "nt", t + "ffn_down_dx", deps=P.deps())
    cwf = c["cwf"]
    (da, dv), (G["ffn_conv_w"], dcb) = tile_bwd(
        t + "ffn_act_b", make_f_ffn_act(tr), [Row(R["av"], 0, cwf, SHORT_HALO), Row(R["av"], FF // cwf, cwf)],
        [P["ffn_conv_w"], P["ffn_conv_b"][None]], [du], tr, ncb=FF // cwf)
    G["ffn_conv_b"] = dcb[0]
    dav = jnp.concatenate([da, dv], axis=1)
    P.emit("ffn_w_up", matmul(R["h3"], dav, "tn", t + "ffn_up_dw"))
    dh3 = matmul(dav, P.big("ffn_w_up", None), "nt", t + "ffn_up_dx", deps=P.deps())
    (dx2,), (dg,) = tile_bwd(t + "ffn_norm_b", f_rms, [Row(R["x2"], 0, D)], [P["ffn_norm_g"][None]], [dh3], tr, adds=[dx3])
    G["ffn_norm_g"] = dg[0]
    P.emit("xattn_wo", matmul(R["o_x"], dx2, "tn", t + "xa_o_dw"))
    do_x = matmul(dx2, P.big("xattn_wo", None), "nt", t + "xa_o_dx", deps=P.deps())
    xa_params = [R["kv"], P["xattn_q_norm_g"][None], P["xattn_k_norm_g"][None]]
    (dq,), (dkv, dgq, dgk) = tile_bwd(t + "xa_attn_b", make_f_xattn(XH, hd), [Row(R["q"], 0, XW)], xa_params, [do_x], c["trx"])
    G["xattn_q_norm_g"], G["xattn_k_norm_g"] = dgq[0], dgk[0]
    P.emit("xattn_wq", matmul(R["h2"], dq, "tn", t + "xa_q_dw"))
    P.emit("xattn_wkv", matmul(m, dkv, "tn", t + "xa_kv_dw"))
    dh2 = matmul(dq, P.big("xattn_wq", None), "nt", t + "xa_q_dx", deps=P.deps())
    dm_l = matmul(dkv, P.big("xattn_wkv", None), "nt", t + "xa_kv_dx")
    (dx1,), (dg,) = tile_bwd(t + "xa_norm_b", f_rms, [Row(R["x1"], 0, D)], [P["xattn_norm_g"][None]], [dh2], tr, adds=[dx2])
    G["xattn_norm_g"] = dg[0]
    P.emit("w_out", matmul(R["y"], dx1, "tn", t + "out_proj_dw"))
    dy = matmul(dx1, P.big("w_out", None), "nt", t + "out_proj_dx", deps=P.deps())
    merge_rows = [Row(p, n, D) for n in range(3)] + [Row(pr, 0, D) for pr in R["proj"]]
    dmerge, _ = tile_bwd(t + "merge_b", f_merge, merge_rows, [], [dy], c["trm"])
    dgl, dpr = dmerge[:3], dmerge[3:]
    branches = [R["out_a"], R["out_b"], R["out_c"]]
    P.emit("w_branch", jnp.stack([matmul(branches[n], dpr[n], "tn", t + f"branch{n}_dw") for n in range(3)]))
    dbr = [matmul(dpr[n], P.big("w_branch", None)[n], "nt", t + f"branch{n}_dx", deps=P.deps()) for n in range(3)]
    conf_params = [P["conf_dw_w"], P["conf_dw_b"][None], P["conf_ln_g"][None], P["conf_ln_b"][None]]
    (dcu_a, dcu_g), (G["conf_dw_w"], db, dlg, dlb) = tile_bwd(
        t + "conf_b", make_f_conf(tr), [Row(p, cbq + 7, MW, CONF_HALO), Row(p, cbq + 8, MW, CONF_HALO)],
        conf_params, [dbr[2]], tr)
    G["conf_dw_b"], G["conf_ln_g"], G["conf_ln_b"] = db[0], dlg[0], dlb[0]
    (do_g, dgz), (dg,) = tile_bwd(t + "gdn_out_b", make_f_gdn_out(hd), [Row(R["o_g"], 0, MW), Row(p, cbq + 6, MW)],
                                  [P["gdn_out_norm_g"][None]], [dbr[1]], tr)
    G["gdn_out_norm_g"] = dg[0]
    dgq, dgk2, du_, dw_, dgc = gdn_scan_bwd(R["gq"], R["gk"], R["u"], R["w"], R["gc"], R["sin"], do_g, H, hd, t + "gdn_scan_b")
    (dgk1, dgv, dgates, _), _ = tile_bwd(
        t + "gdn_solve_b", make_f_gdn_solve(H, hd, True),
        [Row(R["gk"], 0, MW), Row(R["qkv"], 2, MW), Row(R["gates"], 0, LANES), Row(R["tinv"], 0, H * GDN_CHUNK)],
        [], [du_, dw_, dgc], GDN_CHUNK)
    gp_rows = [Row(R["qkv"], 0, MW), Row(R["qkv"], 1, MW), Row(p, cbs, LANES)]
    gp_params = [_pad_lanes(P["gdn_a_log"], H), _pad_lanes(P["gdn_dt_bias"], H)]
    (dqa, dka, dsmall_g), (dal, ddt) = tile_bwd(t + "gdn_prep_b", make_f_gdn_prep(H, hd), gp_rows, gp_params,
                                               [dgq, dgk1 + dgk2, dgates], tr)
    G["gdn_a_log"], G["gdn_dt_bias"] = dal[0, H:2 * H], ddt[0, H:2 * H]
    (dgqkv,), (G["gdn_conv_w"],) = tile_bwd(
        t + "gdn_conv_b", make_f_short_conv(tr, True), [Row(p, cbq + 3, MW, SHORT_HALO)], [P["gdn_conv_w"]],
        [jnp.concatenate([dqa, dka, dgv], axis=1)], tr, ncb=3)
    do_b, delta = tile_fwd(t + "fox_dprep", make_f_dprep(H, hd), [Row(dbr[0], 0, MW), Row(R["out_a"], 0, MW)], [],
                           [(MW, BF16), (LANES, F32)], tr)
    S = p.shape[0]
    delta_row = delta[:, :H].T[:, None, :]
    lse_row = R["lse"].reshape(H, 1, S)
    dqn, dfq_, dkn, dvf, dfk = fox_bwd(R["qn"], R["kn"], R["vb"], do_b, R["fcol"], lse_row, delta_row, H, hd,
                                       t + "fox_attn_b")
    dF = jnp.pad((dfk.reshape(H, S) + dfq_.reshape(H, S)).T, ((0, 0), (0, LANES - H)))
    dlogf = cumsum_rows(dF, t + "fox_cumsum_b", reverse=True)
    fox_rows = [Row(p, cbq, MW), Row(p, cbq + 1, MW), Row(p, cbq + 2, MW), Row(p, cbs, LANES)]
    fox_params = [_pad_lanes(P["fox_fb"], 0), P["fox_q_norm_g"][None], P["fox_k_norm_g"][None]]
    (dfq, dfk_, dfv, dsmall_f), (dfb, dgq_, dgk_) = tile_bwd(t + "fox_prep_b", make_f_fox_prep(H, hd), fox_rows, fox_params,
                                                          [dqn, dkn, dvf, dlogf], tr)
    G["fox_fb"], G["fox_q_norm_g"], G["fox_k_norm_g"] = dfb[0, :H], dgq_[0], dgk_[0]
    dp = jnp.concatenate(dgl + [dfq, dfk_, dfv, dgqkv, dgz, dcu_a, dcu_g, dsmall_f + dsmall_g], axis=1)
    P.emit("w_in", unpermute_in_cols(matmul(R["h"], dp, "tn", t + "in_proj_dw"), dm))
    dh = matmul(dp, P.big("w_in", None), "nt", t + "in_proj_dx", deps=P.deps())
    (dx,), (dg,) = tile_bwd(t + "mix_norm_b", f_rms, [Row(R["x"], 0, D)], [P["mix_norm_g"][None]], [dh], tr, adds=[dx1])
    G["mix_norm_g"] = dg[0]
    return dx, dm_l, G


def local_step(x, mem, target, layers, mem_norm_g, dm):
    trm = _pick(mem.shape[0], (256, 128, 64, 32, 16, 8))
    (m,) = tile_fwd("mem_norm", f_rms, [Row(mem, 0, dm.D)], [mem_norm_g[None]], [(dm.D, F32)], trm)
    res = []
    for l, P in enumerate(layers):
        x, R = layer_fwd(x, m, P, dm, f"l{l}_")
        res.append(R)
    loss, dx = loss_head(x, target, "loss_head")
    grads, dm_sum = [None] * len(layers), None
    for l in reversed(range(len(layers))):
        dx, dm_l, grads[l] = layer_bwd(dx, m, layers[l], res[l], dm, f"l{l}_")
        dm_sum = dm_l if dm_sum is None else dm_sum + dm_l
    _, (dg,) = tile_bwd("mem_norm_b", f_rms, [Row(mem, 0, dm.D)], [mem_norm_g[None]], [dm_sum], trm)
    return loss, dx, grads, dg[0]


ARG_NAMES = ["x", "mem", "mix_norm_g", "w_in", "fox_fb", "fox_q_norm_g", "fox_k_norm_g", "gdn_conv_w", "gdn_a_log",
             "gdn_dt_bias", "gdn_out_norm_g", "conf_dw_w", "conf_dw_b", "conf_ln_g", "conf_ln_b", "w_branch", "w_out",
             "mem_norm_g", "xattn_norm_g", "xattn_wq", "xattn_wkv", "xattn_q_norm_g", "xattn_k_norm_g", "xattn_wo",
             "ffn_norm_g", "ffn_w_up", "ffn_conv_w", "ffn_conv_b", "ffn_w_down"]
WEIGHTS = ARG_NAMES[2:]
COL_SHARDED = ["w_in", "gdn_conv_w", "conf_dw_w", "w_branch", "xattn_wo", "ffn_w_up", "ffn_conv_w"]
ROW_SHARDED = ["w_out", "xattn_wq", "xattn_wkv", "ffn_w_down"]
MATMUL_WEIGHTS = ["w_in", "w_branch", "w_out", "xattn_wq", "xattn_wkv", "xattn_wo", "ffn_w_up", "ffn_w_down"]
REPLICATED = [n for n in WEIGHTS if n not in COL_SHARDED + ROW_SHARDED]
SMALL_SHARDED = [n for n in COL_SHARDED + ROW_SHARDED if n not in MATMUL_WEIGHTS]


class LayerWeights:
    def __init__(self, small, fetch, emit, deps):
        self.small, self._fetch, self.emit, self.deps, self._cache = small, fetch, emit, deps, {}

    def __getitem__(self, name):
        return self.small[name]

    def preset(self, name, value):
        self._cache[name] = value

    def big(self, name, after):
        if name not in self._cache:
            self._cache[name] = self._fetch(name, after)
        return self._cache[name]


def _assemble(name, blocks):
    if name in COL_SHARDED:
        g = jnp.moveaxis(blocks, 0, -2)
        return g.reshape(g.shape[:-2] + (g.shape[-2] * g.shape[-1],))
    return blocks.reshape((blocks.shape[0] * blocks.shape[1],) + blocks.shape[2:])


def _split(name, g):
    if name in COL_SHARDED:
        return jnp.moveaxis(g.reshape(g.shape[:-1] + (N_DEV, g.shape[-1] // N_DEV)), -2, 0)
    return g.reshape((N_DEV, g.shape[0] // N_DEV) + g.shape[1:])


def _gather_weight(name, w):
    g = all_gather(w, "gather_" + name)
    if name in COL_SHARDED:
        g = jnp.moveaxis(g, 0, -2)
        return g.reshape(g.shape[:-2] + (g.shape[-2] * g.shape[-1],))
    g = jnp.moveaxis(g, 0, 1)
    return g.reshape(g.shape[:1] + (g.shape[1] * g.shape[2],) + g.shape[3:])


def _scatter_grad(name, g):
    if name in COL_SHARDED:
        g = g.reshape(g.shape[:-1] + (N_DEV, g.shape[-1] // N_DEV))
        return jnp.moveaxis(g, -2, 0)
    g = g.reshape(g.shape[:1] + (N_DEV, g.shape[1] // N_DEV) + g.shape[2:])
    return jnp.moveaxis(g, 1, 0)


def _pack(arrs):
    flat = jnp.concatenate([a.reshape(-1) for a in arrs])
    rows = -(-flat.shape[0] // (8 * LANES)) * 8
    return jnp.pad(flat, (0, rows * LANES - flat.shape[0])).reshape(rows, LANES)


def _unpack(packed, like):
    flat, out, o = packed.reshape(-1), [], 0
    for a in like:
        out.append(flat[o:o + a.size].reshape(a.shape))
        o += a.size
    return out


def kernel(x, mem, mix_norm_g, w_in, fox_fb, fox_q_norm_g, fox_k_norm_g, gdn_conv_w, gdn_a_log, gdn_dt_bias, gdn_out_norm_g, conf_dw_w, conf_dw_b, conf_ln_g, conf_ln_b, w_branch, w_out, mem_norm_g, xattn_norm_g, xattn_wq, xattn_wkv, xattn_q_norm_g, xattn_k_norm_g, xattn_wo, ffn_norm_g, ffn_w_up, ffn_conv_w, ffn_conv_b, ffn_w_down, loss_target, m_mix_norm_g, m_w_in, m_fox_fb, m_fox_q_norm_g, m_fox_k_norm_g, m_gdn_conv_w, m_gdn_a_log, m_gdn_dt_bias, m_gdn_out_norm_g, m_conf_dw_w, m_conf_dw_b, m_conf_ln_g, m_conf_ln_b, m_w_branch, m_w_out, m_mem_norm_g, m_xattn_norm_g, m_xattn_wq, m_xattn_wkv, m_xattn_q_norm_g, m_xattn_k_norm_g, m_xattn_wo, m_ffn_norm_g, m_ffn_w_up, m_ffn_conv_w, m_ffn_conv_b, m_ffn_w_down, v_mix_norm_g, v_w_in, v_fox_fb, v_fox_q_norm_g, v_fox_k_norm_g, v_gdn_conv_w, v_gdn_a_log, v_gdn_dt_bias, v_gdn_out_norm_g, v_conf_dw_w, v_conf_dw_b, v_conf_ln_g, v_conf_ln_b, v_w_branch, v_w_out, v_mem_norm_g, v_xattn_norm_g, v_xattn_wq, v_xattn_wkv, v_xattn_q_norm_g, v_xattn_k_norm_g, v_xattn_wo, v_ffn_norm_g, v_ffn_w_up, v_ffn_conv_w, v_ffn_conv_b, v_ffn_w_down):
    args = locals()
    W = {n: args[n] for n in WEIGHTS}
    Mo = {n: args["m_" + n] for n in WEIGHTS}
    Vo = {n: args["v_" + n] for n in WEIGHTS}
    L = mix_norm_g.shape[0]
    H, hd = fox_fb.shape[1], fox_q_norm_g.shape[1]
    dm = Dims(D=x.shape[-1], H=H, hd=hd, MW=H * hd, XH=xattn_wq.shape[-1] // hd, FF=ffn_conv_b.shape[-1])

    sent = {}

    def fetch(l, n, after):
        whole_w = _assemble(n, all_gather(W[n][l].astype(BF16), f"gather{l}_{n}"))
        return permute_in_cols(whole_w, dm) if n == "w_in" else whole_w

    def emit(l, n, g):
        sent[(l, n)] = all_to_all(_split(n, g).astype(BF16), f"grad{l}_{n}")

    small_full = {n: _gather_weight(n, W[n]) for n in SMALL_SHARDED}
    layers = []
    for l in range(L):
        small = {n: (small_full[n][l] if n in SMALL_SHARDED else W[n][l]) for n in WEIGHTS
                 if n not in MATMUL_WEIGHTS and n != "mem_norm_g"}
        layers.append(LayerWeights(small, functools.partial(fetch, l), functools.partial(emit, l), list))

    loss, dx, grads, d_mem_g = local_step(x[0], mem[0], loss_target[0], layers, mem_norm_g, dm)
    loss = lax.psum(loss, ("x", "y", "c"))

    def whole(n):
        return d_mem_g if n == "mem_norm_g" else jnp.stack([g[n] for g in grads])

    out = {}
    for l in reversed(range(L)):
        for n in reversed(MATMUL_WEIGHTS):
            out[n] = adamw(sent.pop((l, n)), W[n], Mo[n], Vo[n], f"adamw{l}_{n}", layer=l, into=out.get(n))
    for n in SMALL_SHARDED:
        parts = all_to_all(_scatter_grad(n, whole(n)), "exchange_" + n)
        out[n] = adamw(parts, W[n], Mo[n], Vo[n], "adamw_" + n)
    rep = [whole(n) for n in REPLICATED]
    parts = all_gather(_pack(rep), "gather_small_grads")
    packed = adamw(parts, _pack([W[n] for n in REPLICATED]), _pack([Mo[n] for n in REPLICATED]),
                   _pack([Vo[n] for n in REPLICATED]), "adamw_small")
    unpacked = [_unpack(pk, rep) for pk in packed]
    for i, n in enumerate(REPLICATED):
        out[n] = tuple(u[i] for u in unpacked)

    return (loss, dx[None], *[out[n][0] for n in WEIGHTS], *[out[n][1] for n in WEIGHTS],
            *[out[n][2] for n in WEIGHTS], *[out[n][3] for n in WEIGHTS])
```

```python
import functools
import math
from typing import NamedTuple

import jax
import jax.numpy as jnp
from jax import lax
from jax.experimental import pallas as pl
from jax.experimental.pallas import tpu as pltpu

F32 = jnp.float32
BF16 = jnp.bfloat16
N_DEV = 8
LANES = 128
VMEM_LIMIT = 52 << 20
HI = lax.Precision.HIGHEST

ADAM_LR = 0.001
ADAM_B1 = 0.9
ADAM_B2 = 0.999
ADAM_EPS = 1e-08
ADAM_WD = 0.01
ADAM_STEP = 10

GDN_CHUNK = 64
CONF_HALO = 32
SHORT_HALO = 8
ADAMW_BLOCK_ELEMS = 128 * 1024


def _cparams(sem):
    return pltpu.CompilerParams(dimension_semantics=sem, vmem_limit_bytes=VMEM_LIMIT)


def _pick(n, cands):
    for c in cands:
        if c <= n and n % c == 0:
            return c
    return n


def _peer(k):
    x, y, c = lax.axis_index("x"), lax.axis_index("y"), lax.axis_index("c")
    return (x ^ ((k >> 2) & 1), y ^ ((k >> 1) & 1), c ^ (k & 1))


def _me():
    return 4 * lax.axis_index("x") + 2 * lax.axis_index("y") + lax.axis_index("c")


_HBM = pl.BlockSpec(memory_space=pltpu.HBM)


def _exchange(x, name, gather):
    shape = x.shape if not gather else (N_DEV,) + x.shape

    def body(x_ref, out_ref, send_sems, recv_sems, local_sem):
        me = _me()
        own = pltpu.make_async_copy(x_ref if gather else x_ref.at[me], out_ref.at[me], local_sem)
        own.start()
        sends = []
        for k in range(1, N_DEV):
            peer = me ^ k
            cp = pltpu.make_async_remote_copy(
                src_ref=x_ref if gather else x_ref.at[peer], dst_ref=out_ref.at[me],
                send_sem=send_sems.at[k - 1], recv_sem=recv_sems.at[k - 1],
                device_id=_peer(k), device_id_type=pl.DeviceIdType.MESH)
            cp.start()
            sends.append(cp)
        for cp in sends:
            cp.wait_send()
        for k in range(1, N_DEV):
            peer = me ^ k
            pltpu.make_async_remote_copy(
                src_ref=x_ref if gather else x_ref.at[peer], dst_ref=out_ref.at[peer],
                send_sem=send_sems.at[k - 1], recv_sem=recv_sems.at[k - 1],
                device_id=_peer(k), device_id_type=pl.DeviceIdType.MESH).wait_recv()
        own.wait()

    return pl.pallas_call(
        body, name=name, out_shape=jax.ShapeDtypeStruct(shape, x.dtype),
        in_specs=[_HBM], out_specs=_HBM,
        scratch_shapes=[pltpu.SemaphoreType.DMA((N_DEV - 1,)), pltpu.SemaphoreType.DMA((N_DEV - 1,)),
                        pltpu.SemaphoreType.DMA(())],
    )(x)


def all_gather(x, name):
    return _exchange(x, name, True)


def all_to_all(x, name):
    return _exchange(x, name, False)


_SEM = pl.BlockSpec(memory_space=pltpu.SEMAPHORE)
_ANY = pl.BlockSpec(memory_space=pl.ANY)
_DATAFLOW = pltpu.SideEffectType.DATAFLOW_SIDE_EFFECTING


def _split_copy(k, gather, x_ref, land_ref, send_sems, recv_sems, incoming):
    me = _me()
    peer = me ^ k
    return pltpu.make_async_remote_copy(
        src_ref=x_ref if gather else x_ref.at[peer], dst_ref=land_ref.at[peer if incoming else me],
        send_sem=send_sems.at[k - 1], recv_sem=recv_sems.at[k - 1],
        device_id=_peer(k), device_id_type=pl.DeviceIdType.MESH)


def exchange_start(x, name, gather, after=None):
    land_shape = ((N_DEV,) + x.shape) if gather else x.shape

    def body(x_ref, land_ref, *rest):
        send_sems, recv_sems, _, _, token = rest[-5:]
        for k in range(1, N_DEV):
            _split_copy(k, gather, x_ref, land_ref, send_sems, recv_sems, False).start()
        token[...] = jnp.zeros_like(token)

    sems = pltpu.SemaphoreType.DMA((N_DEV - 1,))
    operands = [pltpu.with_memory_space_constraint(x, pltpu.HBM),
                pltpu.with_memory_space_constraint(lax.empty(land_shape, x.dtype), pltpu.HBM)]
    outs = pl.pallas_call(
        body, name=name,
        out_shape=(sems, sems, pltpu.HBM(x.shape, x.dtype), pltpu.HBM(land_shape, x.dtype),
                   jax.ShapeDtypeStruct((8, LANES), F32)),
        in_specs=[_HBM, _HBM] + ([_ANY] if after is not None else []),
        out_specs=(_SEM, _SEM, _HBM, _HBM, pl.BlockSpec(memory_space=pltpu.VMEM)),
        input_output_aliases={0: 2, 1: 3},
        compiler_params=pltpu.CompilerParams(has_side_effects=_DATAFLOW),
    )(*operands, *([after] if after is not None else []))
    return tuple(outs[:4]), outs[4]


def place_own(land, src, name, gather):
    def body(land_ref, src_ref, out_ref, sem):
        me = _me()
        cp = pltpu.make_async_copy(src_ref if gather else src_ref.at[me], out_ref.at[me], sem)
        cp.start()
        cp.wait()

    return pl.pallas_call(
        body, name=name, out_shape=jax.ShapeDtypeStruct(land.shape, land.dtype),
        in_specs=[_HBM, _HBM], out_specs=_HBM, scratch_shapes=[pltpu.SemaphoreType.DMA(())],
        input_output_aliases={0: 0},
    )(land, src)


def exchange_wait(handle, name, gather, after):
    send_sems, recv_sems, x_thru, land_thru = handle

    def body(x_ref, land_ref, send_sems, recv_sems, after_ref, x_out, land_out):
        for k in range(1, N_DEV):
            _split_copy(k, gather, x_ref, land_ref, send_sems, recv_sems, False).wait_send()
        for k in range(1, N_DEV):
            _split_copy(k, gather, x_ref, land_ref, send_sems, recv_sems, True).wait_recv()

    return pl.pallas_call(
        body, name=name,
        out_shape=(pltpu.HBM(x_thru.shape, x_thru.dtype), pltpu.HBM(land_thru.shape, land_thru.dtype)),
        in_specs=[_HBM, _HBM, _SEM, _SEM, _ANY], out_specs=(_HBM, _HBM), input_output_aliases={0: 0, 1: 1},
        compiler_params=pltpu.CompilerParams(has_side_effects=_DATAFLOW),
    )(x_thru, land_thru, send_sems, recv_sems, after)[1]


_DIMS = {"nn": (((1,), (0,)), ((), ())), "nt": (((1,), (1,)), ((), ())), "tn": (((0,), (0,)), ((), ()))}


V7X_MXU_FLOPS = 9.0e14
V7X_HBM_BYTES_PER_S = 3.0e12
V7X_VMEM_STORE_BYTES_PER_S = 4.0e12
GRID_STEP_S = 0.35e-6
MATMUL_VMEM_BUDGET = 40 << 20


def _matmul_tiles(M, N, K, a_bytes, b_bytes, has_add, tm_on_lanes):
    def divisors(n, cands):
        got = [c for c in cands if c <= n and n % c == 0]
        return got or [n]

    best = None
    for tn in divisors(N, (2048, 1408, 1024, 512, 256, 128)):
        for tm in divisors(M, (2048, 1408, 1024, 512, 256, 128) + (() if tm_on_lanes else (64, 32, 16, 8))):
            for tk in divisors(K, (2816, 2048, 1408, 1024, 512, 256, 128)):
                nk = K // tk
                vmem = 2 * (tm * tk * a_bytes + tk * tn * b_bytes + tm * tn * 4 * (2 if has_add else 1))
                vmem += tm * tn * 4 * (2 if nk > 1 else 1)
                if vmem > MATMUL_VMEM_BUDGET:
                    continue
                a_reads = (N // tn) if nk > 1 else 1
                hbm = M * K * a_bytes * a_reads + K * N * b_bytes * (M // tm) + M * N * 4 * (2 if has_add else 1)
                t_mxu = 2.0 * M * N * K / V7X_MXU_FLOPS + (M * N * nk * 8 / V7X_VMEM_STORE_BYTES_PER_S if nk > 1 else 0.0)
                t = max(t_mxu, hbm / V7X_HBM_BYTES_PER_S) + GRID_STEP_S * (M // tm) * (N // tn) * nk
                if best is None or t < best[0]:
                    best = (t, tm, tn, tk)
    return best[1:]


def matmul(a, b, mode, name, add=None, out_dtype=F32, deps=()):
    if mode == "nn":
        (M, K), N = a.shape, b.shape[1]
    elif mode == "nt":
        (M, K), N = a.shape, b.shape[0]
    else:
        (K, M), N = a.shape, b.shape[1]
    tm, tn, tk = _matmul_tiles(M, N, K, a.dtype.itemsize, b.dtype.itemsize, add is not None, mode == "tn")
    nk = K // tk
    if mode == "nn":
        a_spec = pl.BlockSpec((tm, tk), lambda i, j, k: (i, k))
        b_spec = pl.BlockSpec((tk, tn), lambda i, j, k: (k, j))
    elif mode == "nt":
        a_spec = pl.BlockSpec((tm, tk), lambda i, j, k: (i, k))
        b_spec = pl.BlockSpec((tn, tk), lambda i, j, k: (j, k))
    else:
        a_spec = pl.BlockSpec((tk, tm), lambda i, j, k: (k, i))
        b_spec = pl.BlockSpec((tk, tn), lambda i, j, k: (k, j))
    o_spec = pl.BlockSpec((tm, tn), lambda i, j, k: (i, j))
    dims = _DIMS[mode]

    def body(*refs):
        a_ref, b_ref = refs[:2]
        add_ref = refs[2] if add is not None else None
        o_ref = refs[(3 if add is not None else 2) + len(deps)]
        part = lax.dot_general(a_ref[...].astype(BF16), b_ref[...].astype(BF16), dims, preferred_element_type=F32)

        def finish(r):
            if add is not None:
                r = r + add_ref[...]
            o_ref[...] = r.astype(o_ref.dtype)

        if nk == 1:
            finish(part)
            return
        acc_ref = refs[-1]
        k = pl.program_id(2)

        @pl.when(k == 0)
        def _():
            acc_ref[...] = part

        @pl.when(k > 0)
        def _():
            acc_ref[...] += part

        @pl.when(k == nk - 1)
        def _():
            finish(acc_ref[...])

    ins = ([a, b] if add is None else [a, b, add]) + list(deps)
    specs = ([a_spec, b_spec] if add is None else [a_spec, b_spec, o_spec]) + [_ANY] * len(deps)
    return pl.pallas_call(
        body, name=name, grid=(M // tm, N // tn, nk), in_specs=specs, out_specs=o_spec,
        out_shape=jax.ShapeDtypeStruct((M, N), out_dtype),
        scratch_shapes=[pltpu.VMEM((tm, tn), F32)] if nk > 1 else [],
        compiler_params=_cparams(("parallel", "parallel", "arbitrary")),
    )(*ins)


class Row(NamedTuple):
    arr: jax.Array
    cb: int
    cw: int
    halo: int = 0


def _row_vals(refs, rows, first):
    vals, it = [], iter(refs)
    for r in rows:
        cur = next(it)[...].astype(F32)
        if r.halo:
            prev = next(it)[...].astype(F32)
            prev = jnp.where(first, jnp.zeros_like(prev), prev)
            cur = jnp.concatenate([prev, cur], axis=0)
        vals.append(cur)
    return vals


def _row_specs(rows, tr, rev_nt=None):
    specs = []
    for r in rows:
        def cur_map(c, i, r=r):
            return ((rev_nt - 1 - i) if rev_nt else i, r.cb + c)
        specs.append(pl.BlockSpec((tr, r.cw), cur_map))
        if r.halo:
            q = tr // r.halo

            def prev_map(c, i, r=r, q=q):
                t = (rev_nt - 1 - i) if rev_nt else i
                return (jnp.maximum(t * q - 1, 0), r.cb + c)
            specs.append(pl.BlockSpec((r.halo, r.cw), prev_map))
    return specs


def _row_args(rows):
    args = []
    for r in rows:
        args.append(r.arr)
        if r.halo:
            args.append(r.arr)
    return args


def _param_specs(params, ncb):
    return [pl.BlockSpec((p.shape[0], p.shape[1] // ncb), lambda c, i: (0, c)) for p in params]


def tile_fwd(name, f, rows, params, outs, tr, ncb=1):
    S = rows[0].arr.shape[0]
    nt = S // tr
    n_in = sum(2 if r.halo else 1 for r in rows)

    def body(*refs):
        first = pl.program_id(1) == 0
        rv = _row_vals(refs[:n_in], rows, first)
        pv = [p[...] for p in refs[n_in:n_in + len(params)]]
        res = f(rv, pv)
        for o_ref, o in zip(refs[n_in + len(params):], res):
            o_ref[...] = o.astype(o_ref.dtype)

    return pl.pallas_call(
        body, name=name, grid=(ncb, nt),
        in_specs=_row_specs(rows, tr) + _param_specs(params, ncb),
        out_specs=[pl.BlockSpec((tr, cw), lambda c, i: (i, c)) for cw, _ in outs],
        out_shape=[jax.ShapeDtypeStruct((S, cw * ncb), dt) for cw, dt in outs],
        compiler_params=_cparams(("parallel", "parallel")),
    )(*_row_args(rows), *params)


def tile_bwd(name, f, rows, params, couts, tr, ncb=1, adds=None):
    S = rows[0].arr.shape[0]
    nt = S // tr
    n_in = sum(2 if r.halo else 1 for r in rows)
    adds = adds or [None] * len(rows)
    add_list = [a for a in adds if a is not None]
    n_p, n_c, n_a, n_r = len(params), len(couts), len(add_list), len(rows)
    halos = [r for r in rows if r.halo]

    def body(*refs):
        i = pl.program_id(1)
        first = i == nt - 1
        pos = 0
        in_refs = refs[pos:pos + n_in]; pos += n_in
        p_refs = refs[pos:pos + n_p]; pos += n_p
        c_refs = refs[pos:pos + n_c]; pos += n_c
        a_refs = list(refs[pos:pos + n_a]); pos += n_a
        dr_refs = refs[pos:pos + n_r]; pos += n_r
        dp_refs = refs[pos:pos + n_p]; pos += n_p
        carry_refs = list(refs[pos:])
        rv = _row_vals(in_refs, rows, first)
        pv = [p[...] for p in p_refs]
        _, vjp = jax.vjp(lambda rv_, pv_: f(rv_, pv_), rv, pv)
        drv, dpv = vjp([c[...].astype(F32) for c in c_refs])
        for r, d, d_ref, a in zip(rows, drv, dr_refs, adds):
            a_val = a_refs.pop(0)[...] if a is not None else None
            if r.halo:
                carry = carry_refs.pop(0)
                cur = d[r.halo:]
                if a_val is not None:
                    cur = cur + a_val
                d_ref[...] = cur

                @pl.when(i > 0)
                def _(d_ref=d_ref, carry=carry, r=r):
                    d_ref[pl.ds(tr - r.halo, r.halo), :] += carry[...]

                carry[...] = d[:r.halo]
            else:
                d_ref[...] = d if a_val is None else d + a_val

        for dp_ref, dp in zip(dp_refs, dpv):
            @pl.when(i == 0)
            def _(dp_ref=dp_ref):
                dp_ref[...] = jnp.zeros_like(dp_ref)

            dp_ref[...] += dp

    rev = lambda c, i: (nt - 1 - i, c)
    return_vals = pl.pallas_call(
        body, name=name, grid=(ncb, nt),
        in_specs=(_row_specs(rows, tr, rev_nt=nt) + _param_specs(params, ncb)
                  + [pl.BlockSpec((tr, c.shape[1] // ncb), rev) for c in couts]
                  + [pl.BlockSpec((tr, a.shape[1] // ncb), rev) for a in add_list]),
        out_specs=([pl.BlockSpec((tr, r.cw), rev) for r in rows] + _param_specs(params, ncb)),
        out_shape=([jax.ShapeDtypeStruct((S, r.cw * ncb), F32) for r in rows]
                   + [jax.ShapeDtypeStruct(p.shape, F32) for p in params]),
        scratch_shapes=[pltpu.VMEM((r.halo, r.cw), F32) for r in halos],
        compiler_params=_cparams(("parallel", "arbitrary")),
    )(*_row_args(rows), *params, *couts, *add_list)
    return list(return_vals[:n_r]), list(return_vals[n_r:])


def adamw(parts, w, m, v, name, layer=None, into=None):
    shape = w.shape
    C = shape[-1]
    R = math.prod(shape[:-1])
    rows = R if layer is None else R // shape[0]
    parts2, w2, m2, v2 = parts.reshape(N_DEV, rows, C), w.reshape(R, C), m.reshape(R, C), v.reshape(R, C)
    lanes = -(-C // LANES) * LANES
    tr = _pick(rows, [t for t in (1024, 512, 256, 128, 64, 32, 16, 8) if t * lanes <= ADAMW_BLOCK_ELEMS])
    first = 0 if layer is None else layer * (rows // tr)

    def body(p_ref, w_ref, m_ref, v_ref, *rest):
        g_out, d_out, m_out, v_out = rest[-4:]
        g = p_ref[0].astype(F32)
        for d in range(1, N_DEV):
            g = g + p_ref[d].astype(F32)
        mm = ADAM_B1 * m_ref[...] + (1.0 - ADAM_B1) * g
        vv = ADAM_B2 * v_ref[...] + (1.0 - ADAM_B2) * jnp.square(g)
        m_hat = mm / (1.0 - ADAM_B1 ** ADAM_STEP)
        v_hat = vv / (1.0 - ADAM_B2 ** ADAM_STEP)
        g_out[...] = g
        d_out[...] = -ADAM_LR * (m_hat / (jnp.sqrt(v_hat) + ADAM_EPS) + ADAM_WD * w_ref[...])
        m_out[...] = mm
        v_out[...] = vv

    blk = pl.BlockSpec((tr, C), lambda i: (first + i, 0))
    prev = [] if into is None else [a.reshape(R, C) for a in into]
    outs = pl.pallas_call(
        body, name=name, grid=(rows // tr,),
        in_specs=[pl.BlockSpec((N_DEV, tr, C), lambda i: (0, i, 0)), blk, blk, blk] + [_ANY] * len(prev),
        out_specs=[blk] * 4, out_shape=[jax.ShapeDtypeStruct((R, C), F32)] * 4,
        input_output_aliases={4 + j: j for j in range(len(prev))},
        compiler_params=_cparams(("parallel",)),
    )(parts2, w2, m2, v2, *prev)
    return tuple(o.reshape(shape) for o in outs)


def _dot(a, b, mode):
    return lax.dot_general(a.astype(BF16), b.astype(BF16), _DIMS[mode], preferred_element_type=F32)


@jax.custom_vjp
def mm_nn(a, b):
    return _dot(a, b, "nn")


@jax.custom_vjp
def mm_nt(a, b):
    return _dot(a, b, "nt")


@jax.custom_vjp
def mm_tn(a, b):
    return _dot(a, b, "tn")


mm_nn.defvjp(lambda a, b: (_dot(a, b, "nn"), (a, b)), lambda r, g: (mm_nt(g, r[1]), mm_tn(r[0], g)))
mm_nt.defvjp(lambda a, b: (_dot(a, b, "nt"), (a, b)), lambda r, g: (mm_nn(g, r[1]), mm_tn(g, r[0])))
mm_tn.defvjp(lambda a, b: (_dot(a, b, "tn"), (a, b)), lambda r, g: (mm_nt(r[1], g), mm_nn(r[0], g)))


def _mmh(a, b):
    return jnp.dot(a, b, precision=HI, preferred_element_type=F32)


def _dot3(a, b, mode):
    ah, bh = a.astype(BF16), b.astype(BF16)
    al, bl = (a - ah.astype(F32)).astype(BF16), (b - bh.astype(F32)).astype(BF16)

    def d(x, y):
        return lax.dot_general(x, y, _DIMS[mode], preferred_element_type=F32)
    return d(ah, bh) + (d(ah, bl) + d(al, bh))


@jax.custom_vjp
def mm3_nn(a, b):
    return _dot3(a, b, "nn")


mm3_nn.defvjp(lambda a, b: (_dot3(a, b, "nn"), (a, b)), lambda r, g: (_dot3(g, r[1], "nt"), _dot3(r[0], g, "tn")))


def _sigmoid(x):
    return jax.nn.sigmoid(x)


def _silu(x):
    return x * jax.nn.sigmoid(x)


def _softplus(x):
    return jnp.maximum(x, 0.0) + jnp.log(1.0 + jnp.exp(-jnp.abs(x)))


def _log_sigmoid(x):
    return jnp.minimum(x, 0.0) - jnp.log(1.0 + jnp.exp(-jnp.abs(x)))


def _rms(x, g, eps=1e-6):
    return x * lax.rsqrt(jnp.mean(x * x, axis=-1, keepdims=True) + eps) * g


def _heads(fn, x, hd):
    return jnp.concatenate([fn(x[:, h * hd:(h + 1) * hd]) for h in range(x.shape[1] // hd)], axis=1)


def _causal_conv(x, w, halo, tr):
    K = w.shape[0]
    acc = jnp.zeros((tr, x.shape[1]), F32)
    for k in range(K):
        o = halo - (K - 1) + k
        acc = acc + w[k:k + 1] * x[o:o + tr]
    return acc


def _lanes(shape):
    return lax.broadcasted_iota(jnp.int32, shape, len(shape) - 1)


def f_rms(rv, pv):
    return [_rms(rv[0], pv[0])]


def make_f_fox_prep(H, hd):
    def f(rv, pv):
        fq, fk, fv, small = rv
        fb, gq, gk = pv
        qn = _heads(lambda t: _rms(t, gq), fq, hd) * (hd ** -0.5)
        kn = _heads(lambda t: _rms(t, gk), fk, hd)
        logf = jnp.where(_lanes(small.shape) < H, _log_sigmoid(small + fb), 0.0)
        return [qn, kn, fv, logf]
    return f


def make_f_short_conv(tr, act):
    def f(rv, pv):
        y = _causal_conv(rv[0], pv[0], SHORT_HALO, tr)
        return [_silu(y) if act else y]
    return f


def make_f_gdn_prep(H, hd):
    def l2(t):
        return t * lax.rsqrt(jnp.sum(t * t, axis=-1, keepdims=True) + 1e-6)

    def f(rv, pv):
        q, k, small = rv
        alog, dtb = pv
        qn = _heads(l2, q, hd) * (hd ** -0.5)
        kn = _heads(l2, k, hd)
        ln = _lanes(small.shape)
        g = -jnp.exp(alog) * _softplus(small + dtb)
        gates = jnp.where((ln >= H) & (ln < 2 * H), g, jnp.where((ln >= 2 * H) & (ln < 3 * H), _sigmoid(small), 0.0))
        return [qn, kn, gates]
    return f


def _tri(n, strict=False, upper=False):
    ii = lax.broadcasted_iota(jnp.int32, (n, n), 0)
    jj = lax.broadcasted_iota(jnp.int32, (n, n), 1)
    if upper:
        ii, jj = jj, ii
    return ii > jj if strict else ii >= jj


def _decay(gcol, mask):
    C = gcol.shape[0]
    G = jnp.broadcast_to(gcol, (C, C))
    return jnp.where(mask, jnp.exp(jnp.where(mask, G - G.T, 0.0)), 0.0)


def _nilpotent_inverses(Xs):
    C = Xs[0].shape[0]
    eye = (lax.broadcasted_iota(jnp.int32, (C, C), 0) == lax.broadcasted_iota(jnp.int32, (C, C), 1)).astype(F32)
    Ts, Ps = [eye + X for X in Xs], list(Xs)
    for _ in range(int(math.log2(C)) - 1):
        Ps = [mm3_nn(P, P) for P in Ps]
        Ts = [T + mm3_nn(T, P) for T, P in zip(Ts, Ps)]
    return Ts


@jax.custom_vjp
def _saved_inverse(X, T):
    return T


_saved_inverse.defvjp(lambda X, T: (T, T),
                      lambda T, g: (_dot3(T, _dot3(g, T, "nt"), "tn"), jnp.zeros_like(T)))


def make_f_gdn_solve(H, hd, saved):
    C = GDN_CHUNK

    def f(rv, pv):
        k, v, gates = rv[:3]
        gc = _mmh(_tri(C).astype(F32), gates)
        strict = _tri(C, strict=True)
        heads = range(H)
        ks = [k[:, h * hd:(h + 1) * hd] for h in heads]
        gcols = [gc[:, H + h:H + h + 1] for h in heads]
        betas = [gates[:, 2 * H + h:2 * H + h + 1] for h in heads]
        kbs = [kh * b for kh, b in zip(ks, betas)]
        vbs = [v[:, h * hd:(h + 1) * hd] * betas[h] for h in heads]
        Xs = [-(mm_nt(kb, kh) * _decay(g, strict)) for kb, kh, g in zip(kbs, ks, gcols)]
        if saved:
            Ts = [_saved_inverse(X, rv[3][:, h * C:(h + 1) * C]) for h, X in enumerate(Xs)]
        else:
            Ts = _nilpotent_inverses(Xs)
        us = [mm3_nn(T, vb) for T, vb in zip(Ts, vbs)]
        ws = [mm3_nn(T, kb * jnp.exp(g)) for T, kb, g in zip(Ts, kbs, gcols)]
        out = [jnp.concatenate(us, axis=1), jnp.concatenate(ws, axis=1), gc]
        return out if saved else out + [jnp.concatenate(Ts, axis=1)]
    return f


def _gdn_steps(Ss, qs, ks, us, ws, gcols):
    C = qs[0].shape[0]
    causal = _tri(C)
    rows = lax.broadcasted_iota(jnp.int32, gcols[0].shape, 0)
    attn = [mm_nt(q, k) * _decay(g, causal) for q, k, g in zip(qs, ks, gcols)]
    glast = [jnp.sum(jnp.where(rows == C - 1, g, 0.0), axis=0, keepdims=True) for g in gcols]
    v_new = [u - mm_nn(w, S) for u, w, S in zip(us, ws, Ss)]
    o_state = [mm_nn(q * jnp.exp(g), S) for q, g, S in zip(qs, gcols, Ss)]
    o_chunk = [mm_nn(a, vn) for a, vn in zip(attn, v_new)]
    update = [mm_tn(k * jnp.exp(gl - g), vn) for k, gl, g, vn in zip(ks, glast, gcols, v_new)]
    S_new = [S * jnp.exp(gl) + d for S, gl, d in zip(Ss, glast, update)]
    return [a + b for a, b in zip(o_state, o_chunk)], S_new


def make_f_gdn_out(hd):
    def f(rv, pv):
        o, gz = rv
        return [_heads(lambda t: _rms(t, pv[0]), o, hd) * _silu(gz)]
    return f


def make_f_conf(tr):
    def f(rv, pv):
        a, g = rv
        w, b, lg, lb = pv
        y = _causal_conv(a * _sigmoid(g), w, CONF_HALO, tr) + b
        xc = y - jnp.mean(y, axis=-1, keepdims=True)
        y = xc * lax.rsqrt(jnp.mean(xc * xc, axis=-1, keepdims=True) + 1e-5) * lg + lb
        return [_silu(y)]
    return f


def f_merge(rv, pv):
    n = len(rv) // 2
    y = _sigmoid(rv[0]) * rv[n]
    for j in range(1, n):
        y = y + _sigmoid(rv[j]) * rv[n + j]
    return [y]


def make_f_xattn(XH, hd):
    XW = XH * hd

    def f(rv, pv):
        q = rv[0]
        kv, gq, gk = pv
        outs = []
        for h in range(XH):
            qh = _rms(q[:, h * hd:(h + 1) * hd], gq)
            kh = _rms(kv[:, h * hd:(h + 1) * hd], gk)
            s = mm_nt(qh, kh) * (hd ** -0.5)
            e = jnp.exp(s - jnp.max(s, axis=-1, keepdims=True))
            outs.append(mm_nn(e / jnp.sum(e, axis=-1, keepdims=True), kv[:, XW + h * hd:XW + (h + 1) * hd]))
        return [jnp.concatenate(outs, axis=1)]
    return f


def make_f_ffn_act(tr):
    def f(rv, pv):
        a, v = rv
        return [_silu(_causal_conv(a, pv[0], SHORT_HALO, tr) + pv[1]) * v]
    return f


def make_f_dprep(H, hd):
    def f(rv, pv):
        do, o = rv
        ind = (lax.broadcasted_iota(jnp.int32, (H * hd, LANES), 0) // hd
               == lax.broadcasted_iota(jnp.int32, (H * hd, LANES), 1)).astype(F32)
        return [do, _mmh(do * o, ind)]
    return f


def cumsum_rows(x, name, reverse=False):
    S, C = x.shape
    tr = _pick(S, (256, 128, 64))
    nt = S // tr

    def body(x_ref, o_ref, carry):
        @pl.when(pl.program_id(0) == 0)
        def _():
            carry[...] = jnp.zeros_like(carry)

        ii = lax.broadcasted_iota(jnp.int32, (tr, tr), 0)
        jj = lax.broadcasted_iota(jnp.int32, (tr, tr), 1)
        tri = (ii <= jj) if reverse else (ii >= jj)
        y = _mmh(tri.astype(F32), x_ref[...]) + carry[...]
        o_ref[...] = y
        rows = lax.broadcasted_iota(jnp.int32, y.shape, 0)
        carry[...] = jnp.sum(jnp.where(rows == (0 if reverse else tr - 1), y, 0.0), axis=0, keepdims=True)

    spec = pl.BlockSpec((tr, C), (lambda i: (nt - 1 - i, 0)) if reverse else (lambda i: (i, 0)))
    return pl.pallas_call(
        body, name=name, grid=(nt,), in_specs=[spec], out_specs=spec,
        out_shape=jax.ShapeDtypeStruct((S, C), F32), scratch_shapes=[pltpu.VMEM((1, C), F32)],
        compiler_params=_cparams(("arbitrary",)),
    )(x)


def _fox_block(S):
    return _pick(S, (1024, 512, 256, 128))


def fox_fwd(q, k, v, frow, H, hd, name):
    S = q.shape[0]
    bq = _fox_block(S)
    nq = S // bq

    def body(q_ref, k_ref, v_ref, fr_ref, o_ref, lse_ref):
        i = pl.program_id(1)
        qv = q_ref[...]

        def step(j, carry, diag):
            m, l, acc = carry
            cols = pl.ds(pl.multiple_of(j * bq, bq), bq)
            kj, vj = k_ref[cols, :], v_ref[cols, :]
            s = lax.dot_general(qv, kj, _DIMS["nt"], preferred_element_type=F32) - fr_ref[0, :, cols]
            if diag:
                s = jnp.where(_tri(bq), s, -jnp.inf)
            m_new = jnp.maximum(m, jnp.max(s, axis=-1, keepdims=True))
            alpha = jnp.exp(m - m_new)
            p = jnp.exp(s - m_new)
            l = alpha * l + jnp.sum(p, axis=-1, keepdims=True)
            acc = alpha * acc + lax.dot_general(p.astype(BF16), vj, _DIMS["nn"], preferred_element_type=F32)
            return m_new, l, acc

        init = (jnp.full((bq, 1), -jnp.inf, F32), jnp.zeros((bq, 1), F32), jnp.zeros((bq, hd), F32))
        carry = lax.fori_loop(0, i, lambda j, c: step(j, c, False), init)
        m, l, acc = step(i, carry, True)
        o_ref[...] = acc / l
        lse_ref[0] = m + jnp.log(l)

    return pl.pallas_call(
        body, name=name, grid=(H, nq),
        in_specs=[pl.BlockSpec((bq, hd), lambda h, i: (i, h)),
                  pl.BlockSpec((S, hd), lambda h, i: (0, h)), pl.BlockSpec((S, hd), lambda h, i: (0, h)),
                  pl.BlockSpec((1, 1, S), lambda h, i: (h, 0, 0))],
        out_specs=[pl.BlockSpec((bq, hd), lambda h, i: (i, h)), pl.BlockSpec((1, bq, 1), lambda h, i: (h, i, 0))],
        out_shape=[jax.ShapeDtypeStruct((S, H * hd), F32), jax.ShapeDtypeStruct((H, S, 1), F32)],
        compiler_params=_cparams(("parallel", "parallel")),
    )(q, k, v, frow)


def fox_bwd(q, k, v, do, fcol, lse_row, delta_row, H, hd, name):
    S = q.shape[0]
    bk = _fox_block(S)
    nk = S // bk

    def body(q_ref, do_ref, k_ref, v_ref, fc_ref, lse_ref, dl_ref, dq_ref, dfq_ref, dk_ref, dv_ref, df_ref):
        j = pl.program_id(1)

        @pl.when(j == 0)
        def _():
            dq_ref[...] = jnp.zeros_like(dq_ref)
            dfq_ref[...] = jnp.zeros_like(dfq_ref)

        kj, vj = k_ref[...], v_ref[...]
        fk = fc_ref[0]

        def step(i, carry, diag):
            dk, dv, df = carry
            rows = pl.ds(pl.multiple_of(i * bk, bk), bk)
            qi, doi = q_ref[rows, :], do_ref[rows, :]
            st = lax.dot_general(kj, qi, _DIMS["nt"], preferred_element_type=F32) - fk - lse_ref[0, :, rows]
            if diag:
                st = jnp.where(_tri(bk, upper=True), st, -jnp.inf)
            pt = jnp.exp(st)
            dv = dv + lax.dot_general(pt.astype(BF16), doi, _DIMS["nn"], preferred_element_type=F32)
            dpt = lax.dot_general(vj, doi, _DIMS["nt"], preferred_element_type=F32)
            dst = pt * (dpt - dl_ref[0, :, rows])
            df = df - jnp.sum(dst, axis=-1, keepdims=True)
            dfq_ref[0, :, rows] += jnp.sum(dst, axis=0, keepdims=True)
            dsb = dst.astype(BF16)
            dk = dk + lax.dot_general(dsb, qi, _DIMS["nn"], preferred_element_type=F32)
            dq_ref[rows, :] += lax.dot_general(dsb, kj, _DIMS["tn"], preferred_element_type=F32)
            return dk, dv, df

        init = (jnp.zeros((bk, hd), F32), jnp.zeros((bk, hd), F32), jnp.zeros((bk, 1), F32))
        carry = step(j, init, True)
        dk, dv, df = lax.fori_loop(j + 1, nk, lambda i, c: step(i, c, False), carry)
        dk_ref[...] = dk
        dv_ref[...] = dv
        df_ref[0] = df

    whole = pl.BlockSpec((S, hd), lambda h, j: (0, h))
    blk = pl.BlockSpec((bk, hd), lambda h, j: (j, h))
    row = pl.BlockSpec((1, 1, S), lambda h, j: (h, 0, 0))
    col = pl.BlockSpec((1, bk, 1), lambda h, j: (h, j, 0))
    return pl.pallas_call(
        body, name=name, grid=(H, nk),
        in_specs=[whole, whole, blk, blk, col, row, row],
        out_specs=[whole, row, blk, blk, col],
        out_shape=[jax.ShapeDtypeStruct((S, H * hd), F32), jax.ShapeDtypeStruct((H, 1, S), F32)]
        + [jax.ShapeDtypeStruct((S, H * hd), F32)] * 2 + [jax.ShapeDtypeStruct((H, S, 1), F32)],
        compiler_params=_cparams(("parallel", "arbitrary")),
    )(q, do, k, v, fcol, lse_row, delta_row)


def gdn_scan_fwd(q, k, u, w, gc, H, hd, name):
    S = q.shape[0]
    C = GDN_CHUNK
    NC = S // C

    def body(q_ref, k_ref, u_ref, w_ref, gc_ref, o_ref, sin_ref, state):
        @pl.when(pl.program_id(0) == 0)
        def _():
            state[...] = jnp.zeros_like(state)

        gcv = gc_ref[...]
        sl = [slice(h * hd, (h + 1) * hd) for h in range(H)]
        Ss = [state[h] for h in range(H)]
        for h in range(H):
            sin_ref[0, h] = Ss[h]
        outs, S_new = _gdn_steps(Ss, [q_ref[:, s] for s in sl], [k_ref[:, s] for s in sl], [u_ref[:, s] for s in sl],
                                 [w_ref[:, s] for s in sl], [gcv[:, H + h:H + h + 1] for h in range(H)])
        for h in range(H):
            state[h] = S_new[h]
        o_ref[...] = jnp.concatenate(outs, axis=1)

    wide = pl.BlockSpec((C, H * hd), lambda i: (i, 0))
    return pl.pallas_call(
        body, name=name, grid=(NC,),
        in_specs=[wide] * 4 + [pl.BlockSpec((C, LANES), lambda i: (i, 0))],
        out_specs=[wide, pl.BlockSpec((1, H, hd, hd), lambda i: (i, 0, 0, 0))],
        out_shape=[jax.ShapeDtypeStruct((S, H * hd), F32), jax.ShapeDtypeStruct((NC, H, hd, hd), F32)],
        scratch_shapes=[pltpu.VMEM((H, hd, hd), F32)],
        compiler_params=_cparams(("arbitrary",)),
    )(q, k, u, w, gc)


def gdn_scan_bwd(q, k, u, w, gc, sin, do, H, hd, name):
    S = q.shape[0]
    C = GDN_CHUNK
    NC = S // C

    def body(q_ref, k_ref, u_ref, w_ref, gc_ref, sin_ref, do_ref, dq_ref, dk_ref, du_ref, dw_ref, dgc_ref, dstate):
        @pl.when(pl.program_id(0) == 0)
        def _():
            dstate[...] = jnp.zeros_like(dstate)

        gcv = gc_ref[...]
        ln = _lanes(gcv.shape)
        dgc = jnp.zeros_like(gcv)
        sl = [slice(h * hd, (h + 1) * hd) for h in range(H)]
        _, vjp = jax.vjp(_gdn_steps, [sin_ref[0, h] for h in range(H)], [q_ref[:, s] for s in sl],
                         [k_ref[:, s] for s in sl], [u_ref[:, s] for s in sl], [w_ref[:, s] for s in sl],
                         [gcv[:, H + h:H + h + 1] for h in range(H)])
        dS, dq, dk, du, dw, dg = vjp(([do_ref[:, s] for s in sl], [dstate[h] for h in range(H)]))
        for h in range(H):
            dstate[h] = dS[h]
            dgc = dgc + jnp.where(ln == H + h, dg[h], 0.0)
        for ref, lst in zip((dq_ref, dk_ref, du_ref, dw_ref), (dq, dk, du, dw)):
            ref[...] = jnp.concatenate(lst, axis=1)
        dgc_ref[...] = dgc

    wide = pl.BlockSpec((C, H * hd), lambda i: (NC - 1 - i, 0))
    narrow = pl.BlockSpec((C, LANES), lambda i: (NC - 1 - i, 0))
    return pl.pallas_call(
        body, name=name, grid=(NC,),
        in_specs=[wide] * 4 + [narrow, pl.BlockSpec((1, H, hd, hd), lambda i: (NC - 1 - i, 0, 0, 0)), wide],
        out_specs=[wide] * 4 + [narrow],
        out_shape=[jax.ShapeDtypeStruct((S, H * hd), F32)] * 4 + [jax.ShapeDtypeStruct((S, LANES), F32)],
        scratch_shapes=[pltpu.VMEM((H, hd, hd), F32)],
        compiler_params=_cparams(("arbitrary",)),
    )(q, k, u, w, gc, sin, do)


def loss_head(y, t, name):
    S, D = y.shape
    tr = _pick(S, (256, 128, 64))

    def body(y_ref, t_ref, dy_ref, acc_ref):
        @pl.when(pl.program_id(0) == 0)
        def _():
            acc_ref[...] = jnp.zeros_like(acc_ref)

        err = y_ref[...] - t_ref[...]
        dy_ref[...] = err / D
        acc_ref[...] += jnp.sum(jnp.mean(err * err, axis=-1, keepdims=True), axis=0, keepdims=True)

    blk = pl.BlockSpec((tr, D), lambda i: (i, 0))
    dy, acc = pl.pallas_call(
        body, name=name, grid=(S // tr,), in_specs=[blk, blk],
        out_specs=[blk, pl.BlockSpec((8, LANES), lambda i: (0, 0))],
        out_shape=[jax.ShapeDtypeStruct((S, D), F32), jax.ShapeDtypeStruct((8, LANES), F32)],
        compiler_params=_cparams(("arbitrary",)),
    )(y, t)
    return 0.5 * acc[0, 0], dy


class Dims(NamedTuple):
    D: int
    H: int
    hd: int
    MW: int
    XH: int
    FF: int

    @property
    def n_in(self):
        return 9 * self.MW + 3 * self.H + 3 * self.D

    @property
    def n_in_padded(self):
        return 3 * self.D + 9 * self.MW + LANES


def _in_pieces(dm):
    MW, H, D = dm.MW, dm.H, dm.D
    fq, fk, fv, ff = 0, MW, 2 * MW, 3 * MW
    gq = ff + H
    gk, gv = gq + MW, gq + 2 * MW
    ga = gv + MW
    gb, gz = ga + H, ga + 2 * H
    cu = gz + MW
    gl = cu + 2 * MW
    return [(gl, 3 * D), (fq, MW), (fk, MW), (fv, MW), (gq, MW), (gk, MW), (gv, MW), (gz, MW), (cu, 2 * MW),
            (ff, H), (ga, H), (gb, H)]


def permute_in_blocks(blocks, dm):
    n = blocks.shape[-1]
    parts = []
    for s, size in _in_pieces(dm):
        e = s + size
        while s < e:
            d = s // n
            hi = min(e, (d + 1) * n)
            parts.append(blocks[d][:, s - d * n:hi - d * n])
            s = hi
    parts.append(jnp.zeros((blocks.shape[1], LANES - 3 * dm.H), blocks.dtype))
    return jnp.concatenate(parts, axis=1)


def unpermute_to_blocks(wp, dm):
    segs, off = [], 0
    for s, size in _in_pieces(dm):
        segs.append((s, size, off))
        off += size
    segs.sort()
    n = dm.n_in // N_DEV
    chunks = []
    for d in range(N_DEV):
        lo, hi = d * n, (d + 1) * n
        parts = [wp[:, off + max(s, lo) - s:off + min(s + size, hi) - s] for s, size, off in segs
                 if max(s, lo) < min(s + size, hi)]
        chunks.append(jnp.concatenate(parts, axis=1))
    return jnp.stack(chunks, axis=0)


def _pad_lanes(v, at):
    return jnp.pad(v, (at, LANES - at - v.shape[0]))[None]


def _ops(x, m, P, dm, t):
    S = x.shape[0]
    tr = _pick(S, (256, 128, 64))
    return dict(tr=tr, trm=_pick(S, (128, 64)), trx=_pick(S, (512, 256, 128, 64)),
                cbq=3 * dm.D // dm.MW, cbs=(3 * dm.D + 9 * dm.MW) // LANES,
                cwf=_pick(dm.FF, (512, 256, 128)))


def layer_fwd(x, m, P, dm, t):
    D, H, hd, MW, XH, FF = dm
    XW = XH * hd
    c = _ops(x, m, P, dm, t)
    tr, cbq, cbs = c["tr"], c["cbq"], c["cbs"]
    (h,) = tile_fwd(t + "mix_norm", f_rms, [Row(x, 0, D)], [P["mix_norm_g"][None]], [(D, BF16)], tr)
    p = matmul(h, P.big("w_in", h), "nn", t + "in_proj", deps=P.deps())
    fox_rows = [Row(p, cbq, MW), Row(p, cbq + 1, MW), Row(p, cbq + 2, MW), Row(p, cbs, LANES)]
    fox_params = [_pad_lanes(P["fox_fb"], 0), P["fox_q_norm_g"][None], P["fox_k_norm_g"][None]]
    qn, kn, vb, logf = tile_fwd(t + "fox_prep", make_f_fox_prep(H, hd), fox_rows, fox_params,
                                [(MW, BF16)] * 3 + [(LANES, F32)], tr)
    Ft = cumsum_rows(logf, t + "fox_cumsum")[:, :H].T
    fcol, frow = Ft[:, :, None], Ft[:, None, :]
    out_a, lse = fox_fwd(qn, kn, vb, frow, H, hd, t + "fox_attn")
    (qkv,) = tile_fwd(t + "gdn_conv", make_f_short_conv(tr, True), [Row(p, cbq + 3, MW, SHORT_HALO)],
                      [P["gdn_conv_w"]], [(MW, F32)], tr, ncb=3)
    gp_rows = [Row(qkv, 0, MW), Row(qkv, 1, MW), Row(p, cbs, LANES)]
    gp_params = [_pad_lanes(P["gdn_a_log"], H), _pad_lanes(P["gdn_dt_bias"], H)]
    gq, gk, gates = tile_fwd(t + "gdn_prep", make_f_gdn_prep(H, hd), gp_rows, gp_params,
                             [(MW, F32), (MW, F32), (LANES, F32)], tr)
    u, w, gc, tinv = tile_fwd(t + "gdn_solve", make_f_gdn_solve(H, hd, False),
                              [Row(gk, 0, MW), Row(qkv, 2, MW), Row(gates, 0, LANES)], [],
                              [(MW, F32), (MW, F32), (LANES, F32), (H * GDN_CHUNK, F32)], GDN_CHUNK)
    o_g, sin = gdn_scan_fwd(gq, gk, u, w, gc, H, hd, t + "gdn_scan")
    (out_b,) = tile_fwd(t + "gdn_out", make_f_gdn_out(hd), [Row(o_g, 0, MW), Row(p, cbq + 6, MW)],
                        [P["gdn_out_norm_g"][None]], [(MW, F32)], tr)
    conf_params = [P["conf_dw_w"], P["conf_dw_b"][None], P["conf_ln_g"][None], P["conf_ln_b"][None]]
    (out_c,) = tile_fwd(t + "conf", make_f_conf(tr), [Row(p, cbq + 7, MW, CONF_HALO), Row(p, cbq + 8, MW, CONF_HALO)],
                        conf_params, [(MW, F32)], tr)
    branches = [out_a, out_b, out_c]
    proj = [matmul(b, P.big("w_branch", out_a)[n], "nn", t + f"branch{n}") for n, b in enumerate(branches)]
    (y,) = tile_fwd(t + "merge", f_merge, [Row(p, n, D) for n in range(3)] + [Row(pr, 0, D) for pr in proj], [],
                    [(D, BF16)], c["trm"])
    x1 = matmul(y, P.big("w_out", y), "nn", t + "out_proj", add=x)
    (h2,) = tile_fwd(t + "xa_norm", f_rms, [Row(x1, 0, D)], [P["xattn_norm_g"][None]], [(D, BF16)], tr)
    q = matmul(h2, P.big("xattn_wq", h2), "nn", t + "xa_q")
    kv = matmul(m, P.big("xattn_wkv", h2), "nn", t + "xa_kv")
    xa_params = [kv, P["xattn_q_norm_g"][None], P["xattn_k_norm_g"][None]]
    (o_x,) = tile_fwd(t + "xa_attn", make_f_xattn(XH, hd), [Row(q, 0, XW)], xa_params, [(XW, F32)], c["trx"])
    x2 = matmul(o_x, P.big("xattn_wo", o_x), "nn", t + "xa_o", add=x1)
    (h3,) = tile_fwd(t + "ffn_norm", f_rms, [Row(x2, 0, D)], [P["ffn_norm_g"][None]], [(D, BF16)], tr)
    av = matmul(h3, P.big("ffn_w_up", h3), "nn", t + "ffn_up")
    cwf = c["cwf"]
    (uf,) = tile_fwd(t + "ffn_act", make_f_ffn_act(tr), [Row(av, 0, cwf, SHORT_HALO), Row(av, FF // cwf, cwf)],
                     [P["ffn_conv_w"], P["ffn_conv_b"][None]], [(cwf, BF16)], tr, ncb=FF // cwf)
    x3 = matmul(uf, P.big("ffn_w_down", uf), "nn", t + "ffn_down", add=x2)
    res = dict(x=x, h=h, p=p, qn=qn, kn=kn, vb=vb, fcol=fcol, frow=frow, out_a=out_a, lse=lse, qkv=qkv, gq=gq, gk=gk,
               gates=gates, u=u, w=w, gc=gc, tinv=tinv, sin=sin, o_g=o_g, out_b=out_b, out_c=out_c, proj=proj, y=y, x1=x1, h2=h2,
               q=q, kv=kv, o_x=o_x, x2=x2, h3=h3, av=av, uf=uf)
    return x3, res


def layer_bwd(dx3, m, P, R, dm, t):
    D, H, hd, MW, XH, FF = dm
    XW = XH * hd
    c = _ops(R["x"], m, P, dm, t)
    tr, cbq, cbs = c["tr"], c["cbq"], c["cbs"]
    p = R["p"]
    G = {}
    P.emit("ffn_w_down", matmul(R["uf"], dx3, "tn", t + "ffn_down_dw", out_dtype=BF16))
    du = matmul(dx3, P.big("ffn_w_down", None), "nt", t + "ffn_down_dx", deps=P.deps())
    cwf = c["cwf"]
    (da, dv), (G["ffn_conv_w"], dcb) = tile_bwd(
        t + "ffn_act_b", make_f_ffn_act(tr), [Row(R["av"], 0, cwf, SHORT_HALO), Row(R["av"], FF // cwf, cwf)],
        [P["ffn_conv_w"], P["ffn_conv_b"][None]], [du], tr, ncb=FF // cwf)
    G["ffn_conv_b"] = dcb[0]
    dav = jnp.concatenate([da, dv], axis=1)
    P.emit("ffn_w_up", matmul(R["h3"], dav, "tn", t + "ffn_up_dw", out_dtype=BF16))
    dh3 = matmul(dav, P.big("ffn_w_up", None), "nt", t + "ffn_up_dx", deps=P.deps())
    (dx2,), (dg,) = tile_bwd(t + "ffn_norm_b", f_rms, [Row(R["x2"], 0, D)], [P["ffn_norm_g"][None]], [dh3], tr, adds=[dx3])
    G["ffn_norm_g"] = dg[0]
    P.emit("xattn_wo", matmul(R["o_x"], dx2, "tn", t + "xa_o_dw", out_dtype=BF16))
    do_x = matmul(dx2, P.big("xattn_wo", None), "nt", t + "xa_o_dx", deps=P.deps())
    xa_params = [R["kv"], P["xattn_q_norm_g"][None], P["xattn_k_norm_g"][None]]
    (dq,), (dkv, dgq, dgk) = tile_bwd(t + "xa_attn_b", make_f_xattn(XH, hd), [Row(R["q"], 0, XW)], xa_params, [do_x], c["trx"])
    G["xattn_q_norm_g"], G["xattn_k_norm_g"] = dgq[0], dgk[0]
    P.emit("xattn_wq", matmul(R["h2"], dq, "tn", t + "xa_q_dw", out_dtype=BF16))
    P.emit("xattn_wkv", matmul(m, dkv, "tn", t + "xa_kv_dw", out_dtype=BF16))
    dh2 = matmul(dq, P.big("xattn_wq", None), "nt", t + "xa_q_dx", deps=P.deps())
    dm_l = matmul(dkv, P.big("xattn_wkv", None), "nt", t + "xa_kv_dx")
    (dx1,), (dg,) = tile_bwd(t + "xa_norm_b", f_rms, [Row(R["x1"], 0, D)], [P["xattn_norm_g"][None]], [dh2], tr, adds=[dx2])
    G["xattn_norm_g"] = dg[0]
    P.emit("w_out", matmul(R["y"], dx1, "tn", t + "out_proj_dw", out_dtype=BF16))
    dy = matmul(dx1, P.big("w_out", None), "nt", t + "out_proj_dx", deps=P.deps())
    merge_rows = [Row(p, n, D) for n in range(3)] + [Row(pr, 0, D) for pr in R["proj"]]
    dmerge, _ = tile_bwd(t + "merge_b", f_merge, merge_rows, [], [dy], c["trm"])
    dgl, dpr = dmerge[:3], dmerge[3:]
    branches = [R["out_a"], R["out_b"], R["out_c"]]
    P.emit("w_branch", jnp.stack([matmul(branches[n], dpr[n], "tn", t + f"branch{n}_dw", out_dtype=BF16)
                                  for n in range(3)]))
    dbr = [matmul(dpr[n], P.big("w_branch", None)[n], "nt", t + f"branch{n}_dx", deps=P.deps()) for n in range(3)]
    conf_params = [P["conf_dw_w"], P["conf_dw_b"][None], P["conf_ln_g"][None], P["conf_ln_b"][None]]
    (dcu_a, dcu_g), (G["conf_dw_w"], db, dlg, dlb) = tile_bwd(
        t + "conf_b", make_f_conf(tr), [Row(p, cbq + 7, MW, CONF_HALO), Row(p, cbq + 8, MW, CONF_HALO)],
        conf_params, [dbr[2]], tr)
    G["conf_dw_b"], G["conf_ln_g"], G["conf_ln_b"] = db[0], dlg[0], dlb[0]
    (do_g, dgz), (dg,) = tile_bwd(t + "gdn_out_b", make_f_gdn_out(hd), [Row(R["o_g"], 0, MW), Row(p, cbq + 6, MW)],
                                  [P["gdn_out_norm_g"][None]], [dbr[1]], tr)
    G["gdn_out_norm_g"] = dg[0]
    dgq, dgk2, du_, dw_, dgc = gdn_scan_bwd(R["gq"], R["gk"], R["u"], R["w"], R["gc"], R["sin"], do_g, H, hd, t + "gdn_scan_b")
    (dgk1, dgv, dgates, _), _ = tile_bwd(
        t + "gdn_solve_b", make_f_gdn_solve(H, hd, True),
        [Row(R["gk"], 0, MW), Row(R["qkv"], 2, MW), Row(R["gates"], 0, LANES), Row(R["tinv"], 0, H * GDN_CHUNK)],
        [], [du_, dw_, dgc], GDN_CHUNK)
    gp_rows = [Row(R["qkv"], 0, MW), Row(R["qkv"], 1, MW), Row(p, cbs, LANES)]
    gp_params = [_pad_lanes(P["gdn_a_log"], H), _pad_lanes(P["gdn_dt_bias"], H)]
    (dqa, dka, dsmall_g), (dal, ddt) = tile_bwd(t + "gdn_prep_b", make_f_gdn_prep(H, hd), gp_rows, gp_params,
                                               [dgq, dgk1 + dgk2, dgates], tr)
    G["gdn_a_log"], G["gdn_dt_bias"] = dal[0, H:2 * H], ddt[0, H:2 * H]
    (dgqkv,), (G["gdn_conv_w"],) = tile_bwd(
        t + "gdn_conv_b", make_f_short_conv(tr, True), [Row(p, cbq + 3, MW, SHORT_HALO)], [P["gdn_conv_w"]],
        [jnp.concatenate([dqa, dka, dgv], axis=1)], tr, ncb=3)
    do_b, delta = tile_fwd(t + "fox_dprep", make_f_dprep(H, hd), [Row(dbr[0], 0, MW), Row(R["out_a"], 0, MW)], [],
                           [(MW, BF16), (LANES, F32)], tr)
    S = p.shape[0]
    delta_row = delta[:, :H].T[:, None, :]
    lse_row = R["lse"].reshape(H, 1, S)
    dqn, dfq_, dkn, dvf, dfk = fox_bwd(R["qn"], R["kn"], R["vb"], do_b, R["fcol"], lse_row, delta_row, H, hd,
                                       t + "fox_attn_b")
    dF = jnp.pad((dfk.reshape(H, S) + dfq_.reshape(H, S)).T, ((0, 0), (0, LANES - H)))
    dlogf = cumsum_rows(dF, t + "fox_cumsum_b", reverse=True)
    fox_rows = [Row(p, cbq, MW), Row(p, cbq + 1, MW), Row(p, cbq + 2, MW), Row(p, cbs, LANES)]
    fox_params = [_pad_lanes(P["fox_fb"], 0), P["fox_q_norm_g"][None], P["fox_k_norm_g"][None]]
    (dfq, dfk_, dfv, dsmall_f), (dfb, dgq_, dgk_) = tile_bwd(t + "fox_prep_b", make_f_fox_prep(H, hd), fox_rows, fox_params,
                                                          [dqn, dkn, dvf, dlogf], tr)
    G["fox_fb"], G["fox_q_norm_g"], G["fox_k_norm_g"] = dfb[0, :H], dgq_[0], dgk_[0]
    dp = jnp.concatenate(dgl + [dfq, dfk_, dfv, dgqkv, dgz, dcu_a, dcu_g, dsmall_f + dsmall_g], axis=1)
    P.emit("w_in", matmul(R["h"], dp, "tn", t + "in_proj_dw", out_dtype=BF16))
    dh = matmul(dp, P.big("w_in", None), "nt", t + "in_proj_dx", deps=P.deps())
    (dx,), (dg,) = tile_bwd(t + "mix_norm_b", f_rms, [Row(R["x"], 0, D)], [P["mix_norm_g"][None]], [dh], tr, adds=[dx1])
    G["mix_norm_g"] = dg[0]
    return dx, dm_l, G


def local_step(x, mem, target, layers, mem_norm_g, dm):
    trm = _pick(mem.shape[0], (256, 128, 64, 32, 16, 8))
    (m,) = tile_fwd("mem_norm", f_rms, [Row(mem, 0, dm.D)], [mem_norm_g[None]], [(dm.D, F32)], trm)
    res = []
    for l, P in enumerate(layers):
        x, R = layer_fwd(x, m, P, dm, f"l{l}_")
        res.append(R)
    loss, dx = loss_head(x, target, "loss_head")
    grads, dm_sum = [None] * len(layers), None
    for l in reversed(range(len(layers))):
        dx, dm_l, grads[l] = layer_bwd(dx, m, layers[l], res[l], dm, f"l{l}_")
        dm_sum = dm_l if dm_sum is None else dm_sum + dm_l
    _, (dg,) = tile_bwd("mem_norm_b", f_rms, [Row(mem, 0, dm.D)], [mem_norm_g[None]], [dm_sum], trm)
    return loss, dx, grads, dg[0]


ARG_NAMES = ["x", "mem", "mix_norm_g", "w_in", "fox_fb", "fox_q_norm_g", "fox_k_norm_g", "gdn_conv_w", "gdn_a_log",
             "gdn_dt_bias", "gdn_out_norm_g", "conf_dw_w", "conf_dw_b", "conf_ln_g", "conf_ln_b", "w_branch", "w_out",
             "mem_norm_g", "xattn_norm_g", "xattn_wq", "xattn_wkv", "xattn_q_norm_g", "xattn_k_norm_g", "xattn_wo",
             "ffn_norm_g", "ffn_w_up", "ffn_conv_w", "ffn_conv_b", "ffn_w_down"]
WEIGHTS = ARG_NAMES[2:]
COL_SHARDED = ["w_in", "gdn_conv_w", "conf_dw_w", "w_branch", "xattn_wo", "ffn_w_up", "ffn_conv_w"]
ROW_SHARDED = ["w_out", "xattn_wq", "xattn_wkv", "ffn_w_down"]
MATMUL_WEIGHTS = ["w_in", "w_branch", "w_out", "xattn_wq", "xattn_wkv", "xattn_wo", "ffn_w_up", "ffn_w_down"]
REPLICATED = [n for n in WEIGHTS if n not in COL_SHARDED + ROW_SHARDED]
SMALL_SHARDED = [n for n in COL_SHARDED + ROW_SHARDED if n not in MATMUL_WEIGHTS]


class LayerWeights:
    def __init__(self, small, fetch, emit, deps):
        self.small, self._fetch, self.emit, self.deps, self._cache = small, fetch, emit, deps, {}

    def __getitem__(self, name):
        return self.small[name]

    def preset(self, name, value):
        self._cache[name] = value

    def big(self, name, after):
        if name not in self._cache:
            self._cache[name] = self._fetch(name, after)
        return self._cache[name]


def regroup(x, blocked_in, name):
    if blocked_in:
        _, R, n = x.shape
    else:
        R, n = x.shape[0], x.shape[1] // N_DEV
    tr = _pick(R, (512, 256, 128, 64, 32, 16))
    blocked = pl.BlockSpec((None, tr, n), lambda d, i: (d, i, 0))
    flat = pl.BlockSpec((tr, n), lambda d, i: (i, d))

    def body(x_ref, o_ref):
        o_ref[...] = x_ref[...]

    return pl.pallas_call(
        body, name=name, grid=(N_DEV, R // tr),
        in_specs=[blocked if blocked_in else flat], out_specs=flat if blocked_in else blocked,
        out_shape=jax.ShapeDtypeStruct((R, N_DEV * n) if blocked_in else (N_DEV, R, n), x.dtype),
        compiler_params=_cparams(("parallel", "parallel")),
    )(x)


def _assemble(name, blocks, dm, tag):
    if name == "w_in":
        return permute_in_blocks(blocks, dm)
    if name in COL_SHARDED:
        n = blocks.shape[-1]
        whole = regroup(blocks.reshape(N_DEV, -1, n), True, tag + "_regroup")
        return whole.reshape(blocks.shape[1:-1] + (N_DEV * n,))
    return blocks.reshape((blocks.shape[0] * blocks.shape[1],) + blocks.shape[2:])


def _split(name, g, dm, tag):
    if name == "w_in":
        return unpermute_to_blocks(g, dm)
    if name in COL_SHARDED:
        n = g.shape[-1] // N_DEV
        return regroup(g.reshape(-1, g.shape[-1]), False, tag + "_regroup").reshape((N_DEV,) + g.shape[:-1] + (n,))
    return g.reshape((N_DEV, g.shape[0] // N_DEV) + g.shape[1:])


def _gather_weight(name, w):
    g = all_gather(w, "gather_" + name)
    if name in COL_SHARDED:
        g = jnp.moveaxis(g, 0, -2)
        return g.reshape(g.shape[:-2] + (g.shape[-2] * g.shape[-1],))
    g = jnp.moveaxis(g, 0, 1)
    return g.reshape(g.shape[:1] + (g.shape[1] * g.shape[2],) + g.shape[3:])


def _scatter_grad(name, g):
    if name in COL_SHARDED:
        g = g.reshape(g.shape[:-1] + (N_DEV, g.shape[-1] // N_DEV))
        return jnp.moveaxis(g, -2, 0)
    g = g.reshape(g.shape[:1] + (N_DEV, g.shape[1] // N_DEV) + g.shape[2:])
    return jnp.moveaxis(g, 1, 0)


def _pack(arrs):
    flat = jnp.concatenate([a.reshape(-1) for a in arrs])
    rows = -(-flat.shape[0] // (8 * LANES)) * 8
    return jnp.pad(flat, (0, rows * LANES - flat.shape[0])).reshape(rows, LANES)


def _unpack(packed, like):
    flat, out, o = packed.reshape(-1), [], 0
    for a in like:
        out.append(flat[o:o + a.size].reshape(a.shape))
        o += a.size
    return out


def kernel(x, mem, mix_norm_g, w_in, fox_fb, fox_q_norm_g, fox_k_norm_g, gdn_conv_w, gdn_a_log, gdn_dt_bias, gdn_out_norm_g, conf_dw_w, conf_dw_b, conf_ln_g, conf_ln_b, w_branch, w_out, mem_norm_g, xattn_norm_g, xattn_wq, xattn_wkv, xattn_q_norm_g, xattn_k_norm_g, xattn_wo, ffn_norm_g, ffn_w_up, ffn_conv_w, ffn_conv_b, ffn_w_down, loss_target, m_mix_norm_g, m_w_in, m_fox_fb, m_fox_q_norm_g, m_fox_k_norm_g, m_gdn_conv_w, m_gdn_a_log, m_gdn_dt_bias, m_gdn_out_norm_g, m_conf_dw_w, m_conf_dw_b, m_conf_ln_g, m_conf_ln_b, m_w_branch, m_w_out, m_mem_norm_g, m_xattn_norm_g, m_xattn_wq, m_xattn_wkv, m_xattn_q_norm_g, m_xattn_k_norm_g, m_xattn_wo, m_ffn_norm_g, m_ffn_w_up, m_ffn_conv_w, m_ffn_conv_b, m_ffn_w_down, v_mix_norm_g, v_w_in, v_fox_fb, v_fox_q_norm_g, v_fox_k_norm_g, v_gdn_conv_w, v_gdn_a_log, v_gdn_dt_bias, v_gdn_out_norm_g, v_conf_dw_w, v_conf_dw_b, v_conf_ln_g, v_conf_ln_b, v_w_branch, v_w_out, v_mem_norm_g, v_xattn_norm_g, v_xattn_wq, v_xattn_wkv, v_xattn_q_norm_g, v_xattn_k_norm_g, v_xattn_wo, v_ffn_norm_g, v_ffn_w_up, v_ffn_conv_w, v_ffn_conv_b, v_ffn_w_down):
    args = locals()
    W = {n: args[n] for n in WEIGHTS}
    Mo = {n: args["m_" + n] for n in WEIGHTS}
    Vo = {n: args["v_" + n] for n in WEIGHTS}
    L = mix_norm_g.shape[0]
    H, hd = fox_fb.shape[1], fox_q_norm_g.shape[1]
    dm = Dims(D=x.shape[-1], H=H, hd=hd, MW=H * hd, XH=xattn_wq.shape[-1] // hd, FF=ffn_conv_b.shape[-1])

    gathers, sent, tokens = {}, {}, []

    def take_tokens():
        got = list(tokens)
        tokens.clear()
        return got

    def start_gather(l, n, after):
        blk = W[n][l].astype(BF16)
        handle, token = exchange_start(blk, f"gather{l}_{n}_start", True, after)
        gathers[(l, n)] = (handle, blk)
        tokens.append(token)

    def fetch(l, n, after):
        handle, blk = gathers.pop((l, n))
        land = exchange_wait(handle, f"gather{l}_{n}_wait", True, blk if after is None else after)
        return _assemble(n, place_own(land, blk, f"gather{l}_{n}_own", True), dm, f"gather{l}_{n}")

    def emit(l, n, g):
        send = _split(n, g, dm, f"grad{l}_{n}")
        handle, token = exchange_start(send, f"grad{l}_{n}_start", False)
        sent[(l, n)] = (handle, send)
        tokens.append(token)

    start_gather(0, "w_in", None)
    tokens.clear()
    small_full = {n: _gather_weight(n, W[n]) for n in SMALL_SHARDED}
    first_w_in = fetch(0, "w_in", None)
    layers = []
    for l in range(L):
        small = {n: (small_full[n][l] if n in SMALL_SHARDED else W[n][l]) for n in WEIGHTS
                 if n not in MATMUL_WEIGHTS and n != "mem_norm_g"}
        layers.append(LayerWeights(small, functools.partial(fetch, l), functools.partial(emit, l), take_tokens))
        for n in MATMUL_WEIGHTS:
            if (l, n) != (0, "w_in"):
                start_gather(l, n, first_w_in)
    layers[0].preset("w_in", first_w_in)

    loss, dx, grads, d_mem_g = local_step(x[0], mem[0], loss_target[0], layers, mem_norm_g, dm)
    loss = lax.psum(loss, ("x", "y", "c"))

    def whole(n):
        return d_mem_g if n == "mem_norm_g" else jnp.stack([g[n] for g in grads])

    out, after = {}, dx
    for l in reversed(range(L)):
        for n in reversed(MATMUL_WEIGHTS):
            handle, send = sent.pop((l, n))
            land = exchange_wait(handle, f"grad{l}_{n}_wait", False, after)
            parts = place_own(land, send, f"grad{l}_{n}_own", False)
            out[n] = adamw(parts, W[n], Mo[n], Vo[n], f"adamw{l}_{n}", layer=l, into=out.get(n))
            after = out[n][0]
    for n in SMALL_SHARDED:
        parts = all_to_all(_scatter_grad(n, whole(n)), "exchange_" + n)
        out[n] = adamw(parts, W[n], Mo[n], Vo[n], "adamw_" + n)
    rep = [whole(n) for n in REPLICATED]
    parts = all_gather(_pack(rep), "gather_small_grads")
    packed = adamw(parts, _pack([W[n] for n in REPLICATED]), _pack([Mo[n] for n in REPLICATED]),
                   _pack([Vo[n] for n in REPLICATED]), "adamw_small")
    unpacked = [_unpack(pk, rep) for pk in packed]
    for i, n in enumerate(REPLICATED):
        out[n] = tuple(u[i] for u in unpacked)

    return (loss, dx[None], *[out[n][0] for n in WEIGHTS], *[out[n][1] for n in WEIGHTS],
            *[out[n][2] for n in WEIGHTS], *[out[n][3] for n in WEIGHTS])
```

```python
import functools
import math
from typing import NamedTuple

import jax
import jax.numpy as jnp
from jax import lax
from jax.experimental import pallas as pl
from jax.experimental.pallas import tpu as pltpu

F32 = jnp.float32
BF16 = jnp.bfloat16
N_DEV = 8
LANES = 128
VMEM_LIMIT = 52 << 20
HI = lax.Precision.HIGHEST

ADAM_LR = 0.001
ADAM_B1 = 0.9
ADAM_B2 = 0.999
ADAM_EPS = 1e-08
ADAM_WD = 0.01
ADAM_STEP = 10

GDN_CHUNK = 64
CONF_HALO = 32
SHORT_HALO = 8
ADAMW_BLOCK_ELEMS = 128 * 1024


def _cparams(sem):
    return pltpu.CompilerParams(dimension_semantics=sem, vmem_limit_bytes=VMEM_LIMIT)


def _pick(n, cands):
    for c in cands:
        if c <= n and n % c == 0:
            return c
    return n


def _peer(k):
    x, y, c = lax.axis_index("x"), lax.axis_index("y"), lax.axis_index("c")
    return (x ^ ((k >> 2) & 1), y ^ ((k >> 1) & 1), c ^ (k & 1))


def _me():
    return 4 * lax.axis_index("x") + 2 * lax.axis_index("y") + lax.axis_index("c")


_HBM = pl.BlockSpec(memory_space=pltpu.HBM)


def _exchange(x, name, gather):
    shape = x.shape if not gather else (N_DEV,) + x.shape

    def body(x_ref, out_ref, send_sems, recv_sems, local_sem):
        me = _me()
        own = pltpu.make_async_copy(x_ref if gather else x_ref.at[me], out_ref.at[me], local_sem)
        own.start()
        sends = []
        for k in range(1, N_DEV):
            peer = me ^ k
            cp = pltpu.make_async_remote_copy(
                src_ref=x_ref if gather else x_ref.at[peer], dst_ref=out_ref.at[me],
                send_sem=send_sems.at[k - 1], recv_sem=recv_sems.at[k - 1],
                device_id=_peer(k), device_id_type=pl.DeviceIdType.MESH)
            cp.start()
            sends.append(cp)
        for cp in sends:
            cp.wait_send()
        for k in range(1, N_DEV):
            peer = me ^ k
            pltpu.make_async_remote_copy(
                src_ref=x_ref if gather else x_ref.at[peer], dst_ref=out_ref.at[peer],
                send_sem=send_sems.at[k - 1], recv_sem=recv_sems.at[k - 1],
                device_id=_peer(k), device_id_type=pl.DeviceIdType.MESH).wait_recv()
        own.wait()

    return pl.pallas_call(
        body, name=name, out_shape=jax.ShapeDtypeStruct(shape, x.dtype),
        in_specs=[_HBM], out_specs=_HBM,
        scratch_shapes=[pltpu.SemaphoreType.DMA((N_DEV - 1,)), pltpu.SemaphoreType.DMA((N_DEV - 1,)),
                        pltpu.SemaphoreType.DMA(())],
    )(x)


def all_gather(x, name):
    return _exchange(x, name, True)


def all_to_all(x, name):
    return _exchange(x, name, False)


_SEM = pl.BlockSpec(memory_space=pltpu.SEMAPHORE)
_ANY = pl.BlockSpec(memory_space=pl.ANY)
_DATAFLOW = pltpu.SideEffectType.DATAFLOW_SIDE_EFFECTING


def _split_copy(k, gather, x_ref, land_ref, send_sems, recv_sems, incoming):
    me = _me()
    peer = me ^ k
    return pltpu.make_async_remote_copy(
        src_ref=x_ref if gather else x_ref.at[peer], dst_ref=land_ref.at[peer if incoming else me],
        send_sem=send_sems.at[k - 1], recv_sem=recv_sems.at[k - 1],
        device_id=_peer(k), device_id_type=pl.DeviceIdType.MESH)


def exchange_start(x, name, gather, after=None):
    land_shape = ((N_DEV,) + x.shape) if gather else x.shape

    def body(x_ref, land_ref, *rest):
        send_sems, recv_sems, _, _, token = rest[-5:]
        for k in range(1, N_DEV):
            _split_copy(k, gather, x_ref, land_ref, send_sems, recv_sems, False).start()
        token[...] = jnp.zeros_like(token)

    sems = pltpu.SemaphoreType.DMA((N_DEV - 1,))
    operands = [pltpu.with_memory_space_constraint(x, pltpu.HBM),
                pltpu.with_memory_space_constraint(lax.empty(land_shape, x.dtype), pltpu.HBM)]
    outs = pl.pallas_call(
        body, name=name,
        out_shape=(sems, sems, pltpu.HBM(x.shape, x.dtype), pltpu.HBM(land_shape, x.dtype),
                   jax.ShapeDtypeStruct((8, LANES), F32)),
        in_specs=[_HBM, _HBM] + ([_ANY] if after is not None else []),
        out_specs=(_SEM, _SEM, _HBM, _HBM, pl.BlockSpec(memory_space=pltpu.VMEM)),
        input_output_aliases={0: 2, 1: 3},
        compiler_params=pltpu.CompilerParams(has_side_effects=_DATAFLOW),
    )(*operands, *([after] if after is not None else []))
    return tuple(outs[:4]), outs[4]


def exchange_wait(handle, name, gather, after):
    send_sems, recv_sems, x_thru, land_thru = handle

    def body(x_ref, land_ref, send_sems, recv_sems, after_ref, x_out, land_out):
        for k in range(1, N_DEV):
            _split_copy(k, gather, x_ref, land_ref, send_sems, recv_sems, False).wait_send()
        for k in range(1, N_DEV):
            _split_copy(k, gather, x_ref, land_ref, send_sems, recv_sems, True).wait_recv()

    return pl.pallas_call(
        body, name=name,
        out_shape=(pltpu.HBM(x_thru.shape, x_thru.dtype), pltpu.HBM(land_thru.shape, land_thru.dtype)),
        in_specs=[_HBM, _HBM, _SEM, _SEM, _ANY], out_specs=(_HBM, _HBM), input_output_aliases={0: 0, 1: 1},
        compiler_params=pltpu.CompilerParams(has_side_effects=_DATAFLOW),
    )(x_thru, land_thru, send_sems, recv_sems, after)[1]


_DIMS = {"nn": (((1,), (0,)), ((), ())), "nt": (((1,), (1,)), ((), ())), "tn": (((0,), (0,)), ((), ()))}


V7X_MXU_FLOPS = 9.0e14
V7X_HBM_BYTES_PER_S = 3.0e12
V7X_VMEM_STORE_BYTES_PER_S = 4.0e12
GRID_STEP_S = 0.35e-6
MATMUL_VMEM_BUDGET = 40 << 20


def _matmul_tiles(M, N, K, a_bytes, b_bytes, has_add, tm_on_lanes):
    def divisors(n, cands):
        got = [c for c in cands if c <= n and n % c == 0]
        return got or [n]

    best = None
    for tn in divisors(N, (2048, 1408, 1024, 512, 256, 128)):
        for tm in divisors(M, (2048, 1408, 1024, 512, 256, 128) + (() if tm_on_lanes else (64, 32, 16, 8))):
            for tk in divisors(K, (2816, 2048, 1408, 1024, 512, 256, 128)):
                nk = K // tk
                vmem = 2 * (tm * tk * a_bytes + tk * tn * b_bytes + tm * tn * 4 * (2 if has_add else 1))
                vmem += tm * tn * 4 * (2 if nk > 1 else 1)
                if vmem > MATMUL_VMEM_BUDGET:
                    continue
                a_reads = (N // tn) if nk > 1 else 1
                hbm = M * K * a_bytes * a_reads + K * N * b_bytes * (M // tm) + M * N * 4 * (2 if has_add else 1)
                t_mxu = 2.0 * M * N * K / V7X_MXU_FLOPS + (M * N * nk * 8 / V7X_VMEM_STORE_BYTES_PER_S if nk > 1 else 0.0)
                t = max(t_mxu, hbm / V7X_HBM_BYTES_PER_S) + GRID_STEP_S * (M // tm) * (N // tn) * nk
                if best is None or t < best[0]:
                    best = (t, tm, tn, tk)
    return best[1:]


def matmul(a, b, mode, name, add=None, out_dtype=F32, deps=()):
    if mode == "nn":
        (M, K), N = a.shape, b.shape[1]
    elif mode == "nt":
        (M, K), N = a.shape, b.shape[0]
    else:
        (K, M), N = a.shape, b.shape[1]
    tm, tn, tk = _matmul_tiles(M, N, K, a.dtype.itemsize, b.dtype.itemsize, add is not None, mode == "tn")
    nk = K // tk
    if mode == "nn":
        a_spec = pl.BlockSpec((tm, tk), lambda i, j, k: (i, k))
        b_spec = pl.BlockSpec((tk, tn), lambda i, j, k: (k, j))
    elif mode == "nt":
        a_spec = pl.BlockSpec((tm, tk), lambda i, j, k: (i, k))
        b_spec = pl.BlockSpec((tn, tk), lambda i, j, k: (j, k))
    else:
        a_spec = pl.BlockSpec((tk, tm), lambda i, j, k: (k, i))
        b_spec = pl.BlockSpec((tk, tn), lambda i, j, k: (k, j))
    o_spec = pl.BlockSpec((tm, tn), lambda i, j, k: (i, j))
    dims = _DIMS[mode]

    def body(*refs):
        a_ref, b_ref = refs[:2]
        add_ref = refs[2] if add is not None else None
        o_ref = refs[(3 if add is not None else 2) + len(deps)]
        part = lax.dot_general(a_ref[...].astype(BF16), b_ref[...].astype(BF16), dims, preferred_element_type=F32)

        def finish(r):
            if add is not None:
                r = r + add_ref[...]
            o_ref[...] = r.astype(o_ref.dtype)

        if nk == 1:
            finish(part)
            return
        acc_ref = refs[-1]
        k = pl.program_id(2)

        @pl.when(k == 0)
        def _():
            acc_ref[...] = part

        @pl.when(k > 0)
        def _():
            acc_ref[...] += part

        @pl.when(k == nk - 1)
        def _():
            finish(acc_ref[...])

    ins = ([a, b] if add is None else [a, b, add]) + list(deps)
    specs = ([a_spec, b_spec] if add is None else [a_spec, b_spec, o_spec]) + [_ANY] * len(deps)
    return pl.pallas_call(
        body, name=name, grid=(M // tm, N // tn, nk), in_specs=specs, out_specs=o_spec,
        out_shape=jax.ShapeDtypeStruct((M, N), out_dtype),
        scratch_shapes=[pltpu.VMEM((tm, tn), F32)] if nk > 1 else [],
        compiler_params=_cparams(("parallel", "parallel", "arbitrary")),
    )(*ins)


class Row(NamedTuple):
    arr: jax.Array
    cb: int
    cw: int
    halo: int = 0


def _row_vals(refs, rows, first):
    vals, it = [], iter(refs)
    for r in rows:
        cur = next(it)[...].astype(F32)
        if r.halo:
            prev = next(it)[...].astype(F32)
            prev = jnp.where(first, jnp.zeros_like(prev), prev)
            cur = jnp.concatenate([prev, cur], axis=0)
        vals.append(cur)
    return vals


def _row_specs(rows, tr, rev_nt=None):
    specs = []
    for r in rows:
        def cur_map(c, i, r=r):
            return ((rev_nt - 1 - i) if rev_nt else i, r.cb + c)
        specs.append(pl.BlockSpec((tr, r.cw), cur_map))
        if r.halo:
            q = tr // r.halo

            def prev_map(c, i, r=r, q=q):
                t = (rev_nt - 1 - i) if rev_nt else i
                return (jnp.maximum(t * q - 1, 0), r.cb + c)
            specs.append(pl.BlockSpec((r.halo, r.cw), prev_map))
    return specs


def _row_args(rows):
    args = []
    for r in rows:
        args.append(r.arr)
        if r.halo:
            args.append(r.arr)
    return args


def _param_specs(params, ncb):
    return [pl.BlockSpec((p.shape[0], p.shape[1] // ncb), lambda c, i: (0, c)) for p in params]


def tile_fwd(name, f, rows, params, outs, tr, ncb=1):
    S = rows[0].arr.shape[0]
    nt = S // tr
    n_in = sum(2 if r.halo else 1 for r in rows)

    def body(*refs):
        first = pl.program_id(1) == 0
        rv = _row_vals(refs[:n_in], rows, first)
        pv = [p[...] for p in refs[n_in:n_in + len(params)]]
        res = f(rv, pv)
        for o_ref, o in zip(refs[n_in + len(params):], res):
            o_ref[...] = o.astype(o_ref.dtype)

    return pl.pallas_call(
        body, name=name, grid=(ncb, nt),
        in_specs=_row_specs(rows, tr) + _param_specs(params, ncb),
        out_specs=[pl.BlockSpec((tr, cw), lambda c, i: (i, c)) for cw, _ in outs],
        out_shape=[jax.ShapeDtypeStruct((S, cw * ncb), dt) for cw, dt in outs],
        compiler_params=_cparams(("parallel", "parallel")),
    )(*_row_args(rows), *params)


def tile_bwd(name, f, rows, params, couts, tr, ncb=1, adds=None):
    S = rows[0].arr.shape[0]
    nt = S // tr
    n_in = sum(2 if r.halo else 1 for r in rows)
    adds = adds or [None] * len(rows)
    add_list = [a for a in adds if a is not None]
    n_p, n_c, n_a, n_r = len(params), len(couts), len(add_list), len(rows)
    halos = [r for r in rows if r.halo]

    def body(*refs):
        i = pl.program_id(1)
        first = i == nt - 1
        pos = 0
        in_refs = refs[pos:pos + n_in]; pos += n_in
        p_refs = refs[pos:pos + n_p]; pos += n_p
        c_refs = refs[pos:pos + n_c]; pos += n_c
        a_refs = list(refs[pos:pos + n_a]); pos += n_a
        dr_refs = refs[pos:pos + n_r]; pos += n_r
        dp_refs = refs[pos:pos + n_p]; pos += n_p
        carry_refs = list(refs[pos:])
        rv = _row_vals(in_refs, rows, first)
        pv = [p[...] for p in p_refs]
        _, vjp = jax.vjp(lambda rv_, pv_: f(rv_, pv_), rv, pv)
        drv, dpv = vjp([c[...].astype(F32) for c in c_refs])
        for r, d, d_ref, a in zip(rows, drv, dr_refs, adds):
            a_val = a_refs.pop(0)[...] if a is not None else None
            if r.halo:
                carry = carry_refs.pop(0)
                cur = d[r.halo:]
                if a_val is not None:
                    cur = cur + a_val
                d_ref[...] = cur

                @pl.when(i > 0)
                def _(d_ref=d_ref, carry=carry, r=r):
                    d_ref[pl.ds(tr - r.halo, r.halo), :] += carry[...]

                carry[...] = d[:r.halo]
            else:
                d_ref[...] = d if a_val is None else d + a_val

        for dp_ref, dp in zip(dp_refs, dpv):
            @pl.when(i == 0)
            def _(dp_ref=dp_ref):
                dp_ref[...] = jnp.zeros_like(dp_ref)

            dp_ref[...] += dp

    rev = lambda c, i: (nt - 1 - i, c)
    return_vals = pl.pallas_call(
        body, name=name, grid=(ncb, nt),
        in_specs=(_row_specs(rows, tr, rev_nt=nt) + _param_specs(params, ncb)
                  + [pl.BlockSpec((tr, c.shape[1] // ncb), rev) for c in couts]
                  + [pl.BlockSpec((tr, a.shape[1] // ncb), rev) for a in add_list]),
        out_specs=([pl.BlockSpec((tr, r.cw), rev) for r in rows] + _param_specs(params, ncb)),
        out_shape=([jax.ShapeDtypeStruct((S, r.cw * ncb), F32) for r in rows]
                   + [jax.ShapeDtypeStruct(p.shape, F32) for p in params]),
        scratch_shapes=[pltpu.VMEM((r.halo, r.cw), F32) for r in halos],
        compiler_params=_cparams(("parallel", "arbitrary")),
    )(*_row_args(rows), *params, *couts, *add_list)
    return list(return_vals[:n_r]), list(return_vals[n_r:])


def adamw(parts, w, m, v, name, layer=None, into=None, sent=None):
    shape = w.shape
    C = shape[-1]
    R = math.prod(shape[:-1])
    rows = R if layer is None else R // shape[0]
    parts2, w2, m2, v2 = parts.reshape(N_DEV, rows, C), w.reshape(R, C), m.reshape(R, C), v.reshape(R, C)
    lanes = -(-C // LANES) * LANES
    tr = _pick(rows, [t for t in (1024, 512, 256, 128, 64, 32, 16, 8) if t * lanes <= ADAMW_BLOCK_ELEMS])
    first = 0 if layer is None else layer * (rows // tr)

    def body(p_ref, w_ref, m_ref, v_ref, *rest):
        g_out, d_out, m_out, v_out = rest[-4:]

        def part(d):
            if sent is None:
                return p_ref[d].astype(F32)
            return jnp.where(_me() == d, rest[0][d], p_ref[d]).astype(F32)

        g = part(0)
        for d in range(1, N_DEV):
            g = g + part(d)
        mm = ADAM_B1 * m_ref[...] + (1.0 - ADAM_B1) * g
        vv = ADAM_B2 * v_ref[...] + (1.0 - ADAM_B2) * jnp.square(g)
        m_hat = mm / (1.0 - ADAM_B1 ** ADAM_STEP)
        v_hat = vv / (1.0 - ADAM_B2 ** ADAM_STEP)
        g_out[...] = g
        d_out[...] = -ADAM_LR * (m_hat / (jnp.sqrt(v_hat) + ADAM_EPS) + ADAM_WD * w_ref[...])
        m_out[...] = mm
        v_out[...] = vv

    blk = pl.BlockSpec((tr, C), lambda i: (first + i, 0))
    prev = [] if into is None else [a.reshape(R, C) for a in into]
    mine = [] if sent is None else [sent.reshape(N_DEV, rows, C)]
    eight = pl.BlockSpec((N_DEV, tr, C), lambda i: (0, i, 0))
    outs = pl.pallas_call(
        body, name=name, grid=(rows // tr,),
        in_specs=[eight, blk, blk, blk] + [eight] * len(mine) + [_ANY] * len(prev),
        out_specs=[blk] * 4, out_shape=[jax.ShapeDtypeStruct((R, C), F32)] * 4,
        input_output_aliases={4 + len(mine) + j: j for j in range(len(prev))},
        compiler_params=_cparams(("parallel",)),
    )(parts2, w2, m2, v2, *mine, *prev)
    return tuple(o.reshape(shape) for o in outs)


def _dot(a, b, mode):
    return lax.dot_general(a.astype(BF16), b.astype(BF16), _DIMS[mode], preferred_element_type=F32)


@jax.custom_vjp
def mm_nn(a, b):
    return _dot(a, b, "nn")


@jax.custom_vjp
def mm_nt(a, b):
    return _dot(a, b, "nt")


@jax.custom_vjp
def mm_tn(a, b):
    return _dot(a, b, "tn")


mm_nn.defvjp(lambda a, b: (_dot(a, b, "nn"), (a, b)), lambda r, g: (mm_nt(g, r[1]), mm_tn(r[0], g)))
mm_nt.defvjp(lambda a, b: (_dot(a, b, "nt"), (a, b)), lambda r, g: (mm_nn(g, r[1]), mm_tn(g, r[0])))
mm_tn.defvjp(lambda a, b: (_dot(a, b, "tn"), (a, b)), lambda r, g: (mm_nt(r[1], g), mm_nn(r[0], g)))


def _mmh(a, b):
    return jnp.dot(a, b, precision=HI, preferred_element_type=F32)


def _dot3(a, b, mode):
    ah, bh = a.astype(BF16), b.astype(BF16)
    al, bl = (a - ah.astype(F32)).astype(BF16), (b - bh.astype(F32)).astype(BF16)

    def d(x, y):
        return lax.dot_general(x, y, _DIMS[mode], preferred_element_type=F32)
    return d(ah, bh) + (d(ah, bl) + d(al, bh))


@jax.custom_vjp
def mm3_nn(a, b):
    return _dot3(a, b, "nn")


mm3_nn.defvjp(lambda a, b: (_dot3(a, b, "nn"), (a, b)), lambda r, g: (_dot3(g, r[1], "nt"), _dot3(r[0], g, "tn")))


def _sigmoid(x):
    return jax.nn.sigmoid(x)


def _silu(x):
    return x * jax.nn.sigmoid(x)


def _softplus(x):
    return jnp.maximum(x, 0.0) + jnp.log(1.0 + jnp.exp(-jnp.abs(x)))


def _log_sigmoid(x):
    return jnp.minimum(x, 0.0) - jnp.log(1.0 + jnp.exp(-jnp.abs(x)))


def _rms(x, g, eps=1e-6):
    return x * lax.rsqrt(jnp.mean(x * x, axis=-1, keepdims=True) + eps) * g


def _heads(fn, x, hd):
    return jnp.concatenate([fn(x[:, h * hd:(h + 1) * hd]) for h in range(x.shape[1] // hd)], axis=1)


def _causal_conv(x, w, halo, tr):
    K = w.shape[0]
    acc = jnp.zeros((tr, x.shape[1]), F32)
    for k in range(K):
        o = halo - (K - 1) + k
        acc = acc + w[k:k + 1] * x[o:o + tr]
    return acc


def _lanes(shape):
    return lax.broadcasted_iota(jnp.int32, shape, len(shape) - 1)


def f_rms(rv, pv):
    return [_rms(rv[0], pv[0])]


def make_f_fox_prep(H, hd):
    def f(rv, pv):
        fq, fk, fv, small = rv
        fb, gq, gk = pv
        qn = _heads(lambda t: _rms(t, gq), fq, hd) * (hd ** -0.5)
        kn = _heads(lambda t: _rms(t, gk), fk, hd)
        logf = jnp.where(_lanes(small.shape) < H, _log_sigmoid(small + fb), 0.0)
        return [qn, kn, fv, logf]
    return f


def make_f_short_conv(tr, act):
    def f(rv, pv):
        y = _causal_conv(rv[0], pv[0], SHORT_HALO, tr)
        return [_silu(y) if act else y]
    return f


def make_f_gdn_prep(H, hd):
    def l2(t):
        return t * lax.rsqrt(jnp.sum(t * t, axis=-1, keepdims=True) + 1e-6)

    def f(rv, pv):
        q, k, small = rv
        alog, dtb = pv
        qn = _heads(l2, q, hd) * (hd ** -0.5)
        kn = _heads(l2, k, hd)
        ln = _lanes(small.shape)
        g = -jnp.exp(alog) * _softplus(small + dtb)
        gates = jnp.where((ln >= H) & (ln < 2 * H), g, jnp.where((ln >= 2 * H) & (ln < 3 * H), _sigmoid(small), 0.0))
        return [qn, kn, gates]
    return f


def _tri(n, strict=False, upper=False):
    ii = lax.broadcasted_iota(jnp.int32, (n, n), 0)
    jj = lax.broadcasted_iota(jnp.int32, (n, n), 1)
    if upper:
        ii, jj = jj, ii
    return ii > jj if strict else ii >= jj


def _decay(gcol, mask):
    C = gcol.shape[0]
    G = jnp.broadcast_to(gcol, (C, C))
    return jnp.where(mask, jnp.exp(jnp.where(mask, G - G.T, 0.0)), 0.0)


def _nilpotent_inverses(Xs):
    C = Xs[0].shape[0]
    eye = (lax.broadcasted_iota(jnp.int32, (C, C), 0) == lax.broadcasted_iota(jnp.int32, (C, C), 1)).astype(F32)
    Ts, Ps = [eye + X for X in Xs], list(Xs)
    for _ in range(int(math.log2(C)) - 1):
        Ps = [mm3_nn(P, P) for P in Ps]
        Ts = [T + mm3_nn(T, P) for T, P in zip(Ts, Ps)]
    return Ts


@jax.custom_vjp
def _saved_inverse(X, T):
    return T


_saved_inverse.defvjp(lambda X, T: (T, T),
                      lambda T, g: (_dot3(T, _dot3(g, T, "nt"), "tn"), jnp.zeros_like(T)))


def make_f_gdn_solve(H, hd, saved):
    C = GDN_CHUNK

    def f(rv, pv):
        k, v, gates = rv[:3]
        gc = _mmh(_tri(C).astype(F32), gates)
        strict = _tri(C, strict=True)
        heads = range(H)
        ks = [k[:, h * hd:(h + 1) * hd] for h in heads]
        gcols = [gc[:, H + h:H + h + 1] for h in heads]
        betas = [gates[:, 2 * H + h:2 * H + h + 1] for h in heads]
        kbs = [kh * b for kh, b in zip(ks, betas)]
        vbs = [v[:, h * hd:(h + 1) * hd] * betas[h] for h in heads]
        Xs = [-(mm_nt(kb, kh) * _decay(g, strict)) for kb, kh, g in zip(kbs, ks, gcols)]
        if saved:
            Ts = [_saved_inverse(X, rv[3][:, h * C:(h + 1) * C]) for h, X in enumerate(Xs)]
        else:
            Ts = _nilpotent_inverses(Xs)
        us = [mm3_nn(T, vb) for T, vb in zip(Ts, vbs)]
        ws = [mm3_nn(T, kb * jnp.exp(g)) for T, kb, g in zip(Ts, kbs, gcols)]
        out = [jnp.concatenate(us, axis=1), jnp.concatenate(ws, axis=1), gc]
        return out if saved else out + [jnp.concatenate(Ts, axis=1)]
    return f


def _gdn_steps(Ss, qs, ks, us, ws, gcols):
    C = qs[0].shape[0]
    causal = _tri(C)
    rows = lax.broadcasted_iota(jnp.int32, gcols[0].shape, 0)
    attn = [mm_nt(q, k) * _decay(g, causal) for q, k, g in zip(qs, ks, gcols)]
    glast = [jnp.sum(jnp.where(rows == C - 1, g, 0.0), axis=0, keepdims=True) for g in gcols]
    v_new = [u - mm_nn(w, S) for u, w, S in zip(us, ws, Ss)]
    o_state = [mm_nn(q * jnp.exp(g), S) for q, g, S in zip(qs, gcols, Ss)]
    o_chunk = [mm_nn(a, vn) for a, vn in zip(attn, v_new)]
    update = [mm_tn(k * jnp.exp(gl - g), vn) for k, gl, g, vn in zip(ks, glast, gcols, v_new)]
    S_new = [S * jnp.exp(gl) + d for S, gl, d in zip(Ss, glast, update)]
    return [a + b for a, b in zip(o_state, o_chunk)], S_new


def make_f_gdn_out(hd):
    def f(rv, pv):
        o, gz = rv
        return [_heads(lambda t: _rms(t, pv[0]), o, hd) * _silu(gz)]
    return f


def make_f_conf(tr):
    def f(rv, pv):
        a, g = rv
        w, b, lg, lb = pv
        y = _causal_conv(a * _sigmoid(g), w, CONF_HALO, tr) + b
        xc = y - jnp.mean(y, axis=-1, keepdims=True)
        y = xc * lax.rsqrt(jnp.mean(xc * xc, axis=-1, keepdims=True) + 1e-5) * lg + lb
        return [_silu(y)]
    return f


def f_merge(rv, pv):
    n = len(rv) // 2
    y = _sigmoid(rv[0]) * rv[n]
    for j in range(1, n):
        y = y + _sigmoid(rv[j]) * rv[n + j]
    return [y]


def make_f_xattn(XH, hd):
    XW = XH * hd

    def f(rv, pv):
        q = rv[0]
        kv, gq, gk = pv
        outs = []
        for h in range(XH):
            qh = _rms(q[:, h * hd:(h + 1) * hd], gq)
            kh = _rms(kv[:, h * hd:(h + 1) * hd], gk)
            s = mm_nt(qh, kh) * (hd ** -0.5)
            e = jnp.exp(s - jnp.max(s, axis=-1, keepdims=True))
            outs.append(mm_nn(e / jnp.sum(e, axis=-1, keepdims=True), kv[:, XW + h * hd:XW + (h + 1) * hd]))
        return [jnp.concatenate(outs, axis=1)]
    return f


def make_f_ffn_act(tr):
    def f(rv, pv):
        a, v = rv
        return [_silu(_causal_conv(a, pv[0], SHORT_HALO, tr) + pv[1]) * v]
    return f


def make_f_dprep(H, hd):
    def f(rv, pv):
        do, o = rv
        ind = (lax.broadcasted_iota(jnp.int32, (H * hd, LANES), 0) // hd
               == lax.broadcasted_iota(jnp.int32, (H * hd, LANES), 1)).astype(F32)
        return [do, _mmh(do * o, ind)]
    return f


def cumsum_rows(x, name, reverse=False):
    S, C = x.shape
    tr = _pick(S, (256, 128, 64))
    nt = S // tr

    def body(x_ref, o_ref, carry):
        @pl.when(pl.program_id(0) == 0)
        def _():
            carry[...] = jnp.zeros_like(carry)

        ii = lax.broadcasted_iota(jnp.int32, (tr, tr), 0)
        jj = lax.broadcasted_iota(jnp.int32, (tr, tr), 1)
        tri = (ii <= jj) if reverse else (ii >= jj)
        y = _mmh(tri.astype(F32), x_ref[...]) + carry[...]
        o_ref[...] = y
        rows = lax.broadcasted_iota(jnp.int32, y.shape, 0)
        carry[...] = jnp.sum(jnp.where(rows == (0 if reverse else tr - 1), y, 0.0), axis=0, keepdims=True)

    spec = pl.BlockSpec((tr, C), (lambda i: (nt - 1 - i, 0)) if reverse else (lambda i: (i, 0)))
    return pl.pallas_call(
        body, name=name, grid=(nt,), in_specs=[spec], out_specs=spec,
        out_shape=jax.ShapeDtypeStruct((S, C), F32), scratch_shapes=[pltpu.VMEM((1, C), F32)],
        compiler_params=_cparams(("arbitrary",)),
    )(x)


def _fox_block(S):
    return _pick(S, (1024, 512, 256, 128))


def fox_fwd(q, k, v, frow, H, hd, name):
    S = q.shape[0]
    bq = _fox_block(S)
    nq = S // bq

    def body(q_ref, k_ref, v_ref, fr_ref, o_ref, lse_ref):
        i = pl.program_id(1)
        qv = q_ref[...]

        def step(j, carry, diag):
            m, l, acc = carry
            cols = pl.ds(pl.multiple_of(j * bq, bq), bq)
            kj, vj = k_ref[cols, :], v_ref[cols, :]
            s = lax.dot_general(qv, kj, _DIMS["nt"], preferred_element_type=F32) - fr_ref[0, :, cols]
            if diag:
                s = jnp.where(_tri(bq), s, -jnp.inf)
            m_new = jnp.maximum(m, jnp.max(s, axis=-1, keepdims=True))
            alpha = jnp.exp(m - m_new)
            p = jnp.exp(s - m_new)
            l = alpha * l + jnp.sum(p, axis=-1, keepdims=True)
            acc = alpha * acc + lax.dot_general(p.astype(BF16), vj, _DIMS["nn"], preferred_element_type=F32)
            return m_new, l, acc

        init = (jnp.full((bq, 1), -jnp.inf, F32), jnp.zeros((bq, 1), F32), jnp.zeros((bq, hd), F32))
        carry = lax.fori_loop(0, i, lambda j, c: step(j, c, False), init)
        m, l, acc = step(i, carry, True)
        o_ref[...] = acc / l
        lse_ref[0] = m + jnp.log(l)

    return pl.pallas_call(
        body, name=name, grid=(H, nq),
        in_specs=[pl.BlockSpec((bq, hd), lambda h, i: (i, h)),
                  pl.BlockSpec((S, hd), lambda h, i: (0, h)), pl.BlockSpec((S, hd), lambda h, i: (0, h)),
                  pl.BlockSpec((1, 1, S), lambda h, i: (h, 0, 0))],
        out_specs=[pl.BlockSpec((bq, hd), lambda h, i: (i, h)), pl.BlockSpec((1, bq, 1), lambda h, i: (h, i, 0))],
        out_shape=[jax.ShapeDtypeStruct((S, H * hd), F32), jax.ShapeDtypeStruct((H, S, 1), F32)],
        compiler_params=_cparams(("parallel", "parallel")),
    )(q, k, v, frow)


def fox_bwd(q, k, v, do, fcol, lse_row, delta_row, H, hd, name):
    S = q.shape[0]
    bk = _fox_block(S)
    nk = S // bk

    def body(q_ref, do_ref, k_ref, v_ref, fc_ref, lse_ref, dl_ref, dq_ref, dfq_ref, dk_ref, dv_ref, df_ref):
        j = pl.program_id(1)

        @pl.when(j == 0)
        def _():
            dq_ref[...] = jnp.zeros_like(dq_ref)
            dfq_ref[...] = jnp.zeros_like(dfq_ref)

        kj, vj = k_ref[...], v_ref[...]
        fk = fc_ref[0]

        def step(i, carry, diag):
            dk, dv, df = carry
            rows = pl.ds(pl.multiple_of(i * bk, bk), bk)
            qi, doi = q_ref[rows, :], do_ref[rows, :]
            st = lax.dot_general(kj, qi, _DIMS["nt"], preferred_element_type=F32) - fk - lse_ref[0, :, rows]
            if diag:
                st = jnp.where(_tri(bk, upper=True), st, -jnp.inf)
            pt = jnp.exp(st)
            dv = dv + lax.dot_general(pt.astype(BF16), doi, _DIMS["nn"], preferred_element_type=F32)
            dpt = lax.dot_general(vj, doi, _DIMS["nt"], preferred_element_type=F32)
            dst = pt * (dpt - dl_ref[0, :, rows])
            df = df - jnp.sum(dst, axis=-1, keepdims=True)
            dfq_ref[0, :, rows] += jnp.sum(dst, axis=0, keepdims=True)
            dsb = dst.astype(BF16)
            dk = dk + lax.dot_general(dsb, qi, _DIMS["nn"], preferred_element_type=F32)
            dq_ref[rows, :] += lax.dot_general(dsb, kj, _DIMS["tn"], preferred_element_type=F32)
            return dk, dv, df

        init = (jnp.zeros((bk, hd), F32), jnp.zeros((bk, hd), F32), jnp.zeros((bk, 1), F32))
        carry = step(j, init, True)
        dk, dv, df = lax.fori_loop(j + 1, nk, lambda i, c: step(i, c, False), carry)
        dk_ref[...] = dk
        dv_ref[...] = dv
        df_ref[0] = df

    whole = pl.BlockSpec((S, hd), lambda h, j: (0, h))
    blk = pl.BlockSpec((bk, hd), lambda h, j: (j, h))
    row = pl.BlockSpec((1, 1, S), lambda h, j: (h, 0, 0))
    col = pl.BlockSpec((1, bk, 1), lambda h, j: (h, j, 0))
    return pl.pallas_call(
        body, name=name, grid=(H, nk),
        in_specs=[whole, whole, blk, blk, col, row, row],
        out_specs=[whole, row, blk, blk, col],
        out_shape=[jax.ShapeDtypeStruct((S, H * hd), F32), jax.ShapeDtypeStruct((H, 1, S), F32)]
        + [jax.ShapeDtypeStruct((S, H * hd), F32)] * 2 + [jax.ShapeDtypeStruct((H, S, 1), F32)],
        compiler_params=_cparams(("parallel", "arbitrary")),
    )(q, do, k, v, fcol, lse_row, delta_row)


def gdn_scan_fwd(q, k, u, w, gc, H, hd, name):
    S = q.shape[0]
    C = GDN_CHUNK
    NC = S // C

    def body(q_ref, k_ref, u_ref, w_ref, gc_ref, o_ref, sin_ref, state):
        @pl.when(pl.program_id(0) == 0)
        def _():
            state[...] = jnp.zeros_like(state)

        gcv = gc_ref[...]
        sl = [slice(h * hd, (h + 1) * hd) for h in range(H)]
        Ss = [state[h] for h in range(H)]
        for h in range(H):
            sin_ref[0, h] = Ss[h]
        outs, S_new = _gdn_steps(Ss, [q_ref[:, s] for s in sl], [k_ref[:, s] for s in sl], [u_ref[:, s] for s in sl],
                                 [w_ref[:, s] for s in sl], [gcv[:, H + h:H + h + 1] for h in range(H)])
        for h in range(H):
            state[h] = S_new[h]
        o_ref[...] = jnp.concatenate(outs, axis=1)

    wide = pl.BlockSpec((C, H * hd), lambda i: (i, 0))
    return pl.pallas_call(
        body, name=name, grid=(NC,),
        in_specs=[wide] * 4 + [pl.BlockSpec((C, LANES), lambda i: (i, 0))],
        out_specs=[wide, pl.BlockSpec((1, H, hd, hd), lambda i: (i, 0, 0, 0))],
        out_shape=[jax.ShapeDtypeStruct((S, H * hd), F32), jax.ShapeDtypeStruct((NC, H, hd, hd), F32)],
        scratch_shapes=[pltpu.VMEM((H, hd, hd), F32)],
        compiler_params=_cparams(("arbitrary",)),
    )(q, k, u, w, gc)


def gdn_scan_bwd(q, k, u, w, gc, sin, do, H, hd, name):
    S = q.shape[0]
    C = GDN_CHUNK
    NC = S // C

    def body(q_ref, k_ref, u_ref, w_ref, gc_ref, sin_ref, do_ref, dq_ref, dk_ref, du_ref, dw_ref, dgc_ref, dstate):
        @pl.when(pl.program_id(0) == 0)
        def _():
            dstate[...] = jnp.zeros_like(dstate)

        gcv = gc_ref[...]
        ln = _lanes(gcv.shape)
        dgc = jnp.zeros_like(gcv)
        sl = [slice(h * hd, (h + 1) * hd) for h in range(H)]
        _, vjp = jax.vjp(_gdn_steps, [sin_ref[0, h] for h in range(H)], [q_ref[:, s] for s in sl],
                         [k_ref[:, s] for s in sl], [u_ref[:, s] for s in sl], [w_ref[:, s] for s in sl],
                         [gcv[:, H + h:H + h + 1] for h in range(H)])
        dS, dq, dk, du, dw, dg = vjp(([do_ref[:, s] for s in sl], [dstate[h] for h in range(H)]))
        for h in range(H):
            dstate[h] = dS[h]
            dgc = dgc + jnp.where(ln == H + h, dg[h], 0.0)
        for ref, lst in zip((dq_ref, dk_ref, du_ref, dw_ref), (dq, dk, du, dw)):
            ref[...] = jnp.concatenate(lst, axis=1)
        dgc_ref[...] = dgc

    wide = pl.BlockSpec((C, H * hd), lambda i: (NC - 1 - i, 0))
    narrow = pl.BlockSpec((C, LANES), lambda i: (NC - 1 - i, 0))
    return pl.pallas_call(
        body, name=name, grid=(NC,),
        in_specs=[wide] * 4 + [narrow, pl.BlockSpec((1, H, hd, hd), lambda i: (NC - 1 - i, 0, 0, 0)), wide],
        out_specs=[wide] * 4 + [narrow],
        out_shape=[jax.ShapeDtypeStruct((S, H * hd), F32)] * 4 + [jax.ShapeDtypeStruct((S, LANES), F32)],
        scratch_shapes=[pltpu.VMEM((H, hd, hd), F32)],
        compiler_params=_cparams(("arbitrary",)),
    )(q, k, u, w, gc, sin, do)


def loss_head(y, t, name):
    S, D = y.shape
    tr = _pick(S, (256, 128, 64))

    def body(y_ref, t_ref, dy_ref, acc_ref):
        @pl.when(pl.program_id(0) == 0)
        def _():
            acc_ref[...] = jnp.zeros_like(acc_ref)

        err = y_ref[...] - t_ref[...]
        dy_ref[...] = err / D
        acc_ref[...] += jnp.sum(jnp.mean(err * err, axis=-1, keepdims=True), axis=0, keepdims=True)

    blk = pl.BlockSpec((tr, D), lambda i: (i, 0))
    dy, acc = pl.pallas_call(
        body, name=name, grid=(S // tr,), in_specs=[blk, blk],
        out_specs=[blk, pl.BlockSpec((8, LANES), lambda i: (0, 0))],
        out_shape=[jax.ShapeDtypeStruct((S, D), F32), jax.ShapeDtypeStruct((8, LANES), F32)],
        compiler_params=_cparams(("arbitrary",)),
    )(y, t)
    return 0.5 * acc[0, 0], dy


class Dims(NamedTuple):
    D: int
    H: int
    hd: int
    MW: int
    XH: int
    FF: int

    @property
    def n_in(self):
        return 9 * self.MW + 3 * self.H + 3 * self.D

    @property
    def n_in_padded(self):
        return 3 * self.D + 9 * self.MW + LANES


def _in_pieces(dm):
    MW, H, D = dm.MW, dm.H, dm.D
    fq, fk, fv, ff = 0, MW, 2 * MW, 3 * MW
    gq = ff + H
    gk, gv = gq + MW, gq + 2 * MW
    ga = gv + MW
    gb, gz = ga + H, ga + 2 * H
    cu = gz + MW
    gl = cu + 2 * MW
    return [(gl, 3 * D), (fq, MW), (fk, MW), (fv, MW), (gq, MW), (gk, MW), (gv, MW), (gz, MW), (cu, 2 * MW),
            (ff, H), (ga, H), (gb, H)]


def permute_in_blocks(blocks, dm):
    n = blocks[0].shape[-1]
    parts = []
    for s, size in _in_pieces(dm):
        e = s + size
        while s < e:
            d = s // n
            hi = min(e, (d + 1) * n)
            parts.append(blocks[d][:, s - d * n:hi - d * n])
            s = hi
    parts.append(jnp.zeros((blocks[0].shape[0], LANES - 3 * dm.H), blocks[0].dtype))
    return jnp.concatenate(parts, axis=1)


def unpermute_to_blocks(wp, dm):
    segs, off = [], 0
    for s, size in _in_pieces(dm):
        segs.append((s, size, off))
        off += size
    segs.sort()
    n = dm.n_in // N_DEV
    chunks = []
    for d in range(N_DEV):
        lo, hi = d * n, (d + 1) * n
        parts = [wp[:, off + max(s, lo) - s:off + min(s + size, hi) - s] for s, size, off in segs
                 if max(s, lo) < min(s + size, hi)]
        chunks.append(jnp.concatenate(parts, axis=1))
    return jnp.stack(chunks, axis=0)


def _pad_lanes(v, at):
    return jnp.pad(v, (at, LANES - at - v.shape[0]))[None]


def _ops(x, m, P, dm, t):
    S = x.shape[0]
    tr = _pick(S, (256, 128, 64))
    return dict(tr=tr, trm=_pick(S, (128, 64)), trx=_pick(S, (512, 256, 128, 64)),
                cbq=3 * dm.D // dm.MW, cbs=(3 * dm.D + 9 * dm.MW) // LANES,
                cwf=_pick(dm.FF, (512, 256, 128)))


def layer_fwd(x, m, P, dm, t):
    D, H, hd, MW, XH, FF = dm
    XW = XH * hd
    c = _ops(x, m, P, dm, t)
    tr, cbq, cbs = c["tr"], c["cbq"], c["cbs"]
    (h,) = tile_fwd(t + "mix_norm", f_rms, [Row(x, 0, D)], [P["mix_norm_g"][None]], [(D, BF16)], tr)
    p = matmul(h, P.big("w_in", h), "nn", t + "in_proj", deps=P.deps())
    fox_rows = [Row(p, cbq, MW), Row(p, cbq + 1, MW), Row(p, cbq + 2, MW), Row(p, cbs, LANES)]
    fox_params = [_pad_lanes(P["fox_fb"], 0), P["fox_q_norm_g"][None], P["fox_k_norm_g"][None]]
    qn, kn, vb, logf = tile_fwd(t + "fox_prep", make_f_fox_prep(H, hd), fox_rows, fox_params,
                                [(MW, BF16)] * 3 + [(LANES, F32)], tr)
    Ft = cumsum_rows(logf, t + "fox_cumsum")[:, :H].T
    fcol, frow = Ft[:, :, None], Ft[:, None, :]
    out_a, lse = fox_fwd(qn, kn, vb, frow, H, hd, t + "fox_attn")
    (qkv,) = tile_fwd(t + "gdn_conv", make_f_short_conv(tr, True), [Row(p, cbq + 3, MW, SHORT_HALO)],
                      [P["gdn_conv_w"]], [(MW, F32)], tr, ncb=3)
    gp_rows = [Row(qkv, 0, MW), Row(qkv, 1, MW), Row(p, cbs, LANES)]
    gp_params = [_pad_lanes(P["gdn_a_log"], H), _pad_lanes(P["gdn_dt_bias"], H)]
    gq, gk, gates = tile_fwd(t + "gdn_prep", make_f_gdn_prep(H, hd), gp_rows, gp_params,
                             [(MW, F32), (MW, F32), (LANES, F32)], tr)
    u, w, gc, tinv = tile_fwd(t + "gdn_solve", make_f_gdn_solve(H, hd, False),
                              [Row(gk, 0, MW), Row(qkv, 2, MW), Row(gates, 0, LANES)], [],
                              [(MW, F32), (MW, F32), (LANES, F32), (H * GDN_CHUNK, F32)], GDN_CHUNK)
    o_g, sin = gdn_scan_fwd(gq, gk, u, w, gc, H, hd, t + "gdn_scan")
    (out_b,) = tile_fwd(t + "gdn_out", make_f_gdn_out(hd), [Row(o_g, 0, MW), Row(p, cbq + 6, MW)],
                        [P["gdn_out_norm_g"][None]], [(MW, F32)], tr)
    conf_params = [P["conf_dw_w"], P["conf_dw_b"][None], P["conf_ln_g"][None], P["conf_ln_b"][None]]
    (out_c,) = tile_fwd(t + "conf", make_f_conf(tr), [Row(p, cbq + 7, MW, CONF_HALO), Row(p, cbq + 8, MW, CONF_HALO)],
                        conf_params, [(MW, F32)], tr)
    branches = [out_a, out_b, out_c]
    proj = [matmul(b, P.big("w_branch", out_a)[n], "nn", t + f"branch{n}") for n, b in enumerate(branches)]
    (y,) = tile_fwd(t + "merge", f_merge, [Row(p, n, D) for n in range(3)] + [Row(pr, 0, D) for pr in proj], [],
                    [(D, BF16)], c["trm"])
    x1 = matmul(y, P.big("w_out", y), "nn", t + "out_proj", add=x)
    (h2,) = tile_fwd(t + "xa_norm", f_rms, [Row(x1, 0, D)], [P["xattn_norm_g"][None]], [(D, BF16)], tr)
    q = matmul(h2, P.big("xattn_wq", h2), "nn", t + "xa_q")
    kv = matmul(m, P.big("xattn_wkv", h2), "nn", t + "xa_kv")
    xa_params = [kv, P["xattn_q_norm_g"][None], P["xattn_k_norm_g"][None]]
    (o_x,) = tile_fwd(t + "xa_attn", make_f_xattn(XH, hd), [Row(q, 0, XW)], xa_params, [(XW, F32)], c["trx"])
    x2 = matmul(o_x, P.big("xattn_wo", o_x), "nn", t + "xa_o", add=x1)
    (h3,) = tile_fwd(t + "ffn_norm", f_rms, [Row(x2, 0, D)], [P["ffn_norm_g"][None]], [(D, BF16)], tr)
    av = matmul(h3, P.big("ffn_w_up", h3), "nn", t + "ffn_up")
    cwf = c["cwf"]
    (uf,) = tile_fwd(t + "ffn_act", make_f_ffn_act(tr), [Row(av, 0, cwf, SHORT_HALO), Row(av, FF // cwf, cwf)],
                     [P["ffn_conv_w"], P["ffn_conv_b"][None]], [(cwf, BF16)], tr, ncb=FF // cwf)
    x3 = matmul(uf, P.big("ffn_w_down", uf), "nn", t + "ffn_down", add=x2)
    res = dict(x=x, h=h, p=p, qn=qn, kn=kn, vb=vb, fcol=fcol, frow=frow, out_a=out_a, lse=lse, qkv=qkv, gq=gq, gk=gk,
               gates=gates, u=u, w=w, gc=gc, tinv=tinv, sin=sin, o_g=o_g, out_b=out_b, out_c=out_c, proj=proj, y=y, x1=x1, h2=h2,
               q=q, kv=kv, o_x=o_x, x2=x2, h3=h3, av=av, uf=uf)
    return x3, res


def layer_bwd(dx3, m, P, R, dm, t):
    D, H, hd, MW, XH, FF = dm
    XW = XH * hd
    c = _ops(R["x"], m, P, dm, t)
    tr, cbq, cbs = c["tr"], c["cbq"], c["cbs"]
    p = R["p"]
    G = {}
    P.emit("ffn_w_down", matmul(R["uf"], dx3, "tn", t + "ffn_down_dw", out_dtype=BF16))
    du = matmul(dx3, P.big("ffn_w_down", None), "nt", t + "ffn_down_dx", deps=P.deps())
    cwf = c["cwf"]
    (da, dv), (G["ffn_conv_w"], dcb) = tile_bwd(
        t + "ffn_act_b", make_f_ffn_act(tr), [Row(R["av"], 0, cwf, SHORT_HALO), Row(R["av"], FF // cwf, cwf)],
        [P["ffn_conv_w"], P["ffn_conv_b"][None]], [du], tr, ncb=FF // cwf)
    G["ffn_conv_b"] = dcb[0]
    dav = jnp.concatenate([da, dv], axis=1)
    P.emit("ffn_w_up", matmul(R["h3"], dav, "tn", t + "ffn_up_dw", out_dtype=BF16))
    dh3 = matmul(dav, P.big("ffn_w_up", None), "nt", t + "ffn_up_dx", deps=P.deps())
    (dx2,), (dg,) = tile_bwd(t + "ffn_norm_b", f_rms, [Row(R["x2"], 0, D)], [P["ffn_norm_g"][None]], [dh3], tr, adds=[dx3])
    G["ffn_norm_g"] = dg[0]
    P.emit("xattn_wo", matmul(R["o_x"], dx2, "tn", t + "xa_o_dw", out_dtype=BF16))
    do_x = matmul(dx2, P.big("xattn_wo", None), "nt", t + "xa_o_dx", deps=P.deps())
    xa_params = [R["kv"], P["xattn_q_norm_g"][None], P["xattn_k_norm_g"][None]]
    (dq,), (dkv, dgq, dgk) = tile_bwd(t + "xa_attn_b", make_f_xattn(XH, hd), [Row(R["q"], 0, XW)], xa_params, [do_x], c["trx"])
    G["xattn_q_norm_g"], G["xattn_k_norm_g"] = dgq[0], dgk[0]
    P.emit("xattn_wq", matmul(R["h2"], dq, "tn", t + "xa_q_dw", out_dtype=BF16))
    P.emit("xattn_wkv", matmul(m, dkv, "tn", t + "xa_kv_dw", out_dtype=BF16))
    dh2 = matmul(dq, P.big("xattn_wq", None), "nt", t + "xa_q_dx", deps=P.deps())
    dm_l = matmul(dkv, P.big("xattn_wkv", None), "nt", t + "xa_kv_dx")
    (dx1,), (dg,) = tile_bwd(t + "xa_norm_b", f_rms, [Row(R["x1"], 0, D)], [P["xattn_norm_g"][None]], [dh2], tr, adds=[dx2])
    G["xattn_norm_g"] = dg[0]
    P.emit("w_out", matmul(R["y"], dx1, "tn", t + "out_proj_dw", out_dtype=BF16))
    dy = matmul(dx1, P.big("w_out", None), "nt", t + "out_proj_dx", deps=P.deps())
    merge_rows = [Row(p, n, D) for n in range(3)] + [Row(pr, 0, D) for pr in R["proj"]]
    dmerge, _ = tile_bwd(t + "merge_b", f_merge, merge_rows, [], [dy], c["trm"])
    dgl, dpr = dmerge[:3], dmerge[3:]
    branches = [R["out_a"], R["out_b"], R["out_c"]]
    P.emit("w_branch", jnp.stack([matmul(branches[n], dpr[n], "tn", t + f"branch{n}_dw", out_dtype=BF16)
                                  for n in range(3)]))
    dbr = [matmul(dpr[n], P.big("w_branch", None)[n], "nt", t + f"branch{n}_dx", deps=P.deps()) for n in range(3)]
    conf_params = [P["conf_dw_w"], P["conf_dw_b"][None], P["conf_ln_g"][None], P["conf_ln_b"][None]]
    (dcu_a, dcu_g), (G["conf_dw_w"], db, dlg, dlb) = tile_bwd(
        t + "conf_b", make_f_conf(tr), [Row(p, cbq + 7, MW, CONF_HALO), Row(p, cbq + 8, MW, CONF_HALO)],
        conf_params, [dbr[2]], tr)
    G["conf_dw_b"], G["conf_ln_g"], G["conf_ln_b"] = db[0], dlg[0], dlb[0]
    (do_g, dgz), (dg,) = tile_bwd(t + "gdn_out_b", make_f_gdn_out(hd), [Row(R["o_g"], 0, MW), Row(p, cbq + 6, MW)],
                                  [P["gdn_out_norm_g"][None]], [dbr[1]], tr)
    G["gdn_out_norm_g"] = dg[0]
    dgq, dgk2, du_, dw_, dgc = gdn_scan_bwd(R["gq"], R["gk"], R["u"], R["w"], R["gc"], R["sin"], do_g, H, hd, t + "gdn_scan_b")
    (dgk1, dgv, dgates, _), _ = tile_bwd(
        t + "gdn_solve_b", make_f_gdn_solve(H, hd, True),
        [Row(R["gk"], 0, MW), Row(R["qkv"], 2, MW), Row(R["gates"], 0, LANES), Row(R["tinv"], 0, H * GDN_CHUNK)],
        [], [du_, dw_, dgc], GDN_CHUNK)
    gp_rows = [Row(R["qkv"], 0, MW), Row(R["qkv"], 1, MW), Row(p, cbs, LANES)]
    gp_params = [_pad_lanes(P["gdn_a_log"], H), _pad_lanes(P["gdn_dt_bias"], H)]
    (dqa, dka, dsmall_g), (dal, ddt) = tile_bwd(t + "gdn_prep_b", make_f_gdn_prep(H, hd), gp_rows, gp_params,
                                               [dgq, dgk1 + dgk2, dgates], tr)
    G["gdn_a_log"], G["gdn_dt_bias"] = dal[0, H:2 * H], ddt[0, H:2 * H]
    (dgqkv,), (G["gdn_conv_w"],) = tile_bwd(
        t + "gdn_conv_b", make_f_short_conv(tr, True), [Row(p, cbq + 3, MW, SHORT_HALO)], [P["gdn_conv_w"]],
        [jnp.concatenate([dqa, dka, dgv], axis=1)], tr, ncb=3)
    do_b, delta = tile_fwd(t + "fox_dprep", make_f_dprep(H, hd), [Row(dbr[0], 0, MW), Row(R["out_a"], 0, MW)], [],
                           [(MW, BF16), (LANES, F32)], tr)
    S = p.shape[0]
    delta_row = delta[:, :H].T[:, None, :]
    lse_row = R["lse"].reshape(H, 1, S)
    dqn, dfq_, dkn, dvf, dfk = fox_bwd(R["qn"], R["kn"], R["vb"], do_b, R["fcol"], lse_row, delta_row, H, hd,
                                       t + "fox_attn_b")
    dF = jnp.pad((dfk.reshape(H, S) + dfq_.reshape(H, S)).T, ((0, 0), (0, LANES - H)))
    dlogf = cumsum_rows(dF, t + "fox_cumsum_b", reverse=True)
    fox_rows = [Row(p, cbq, MW), Row(p, cbq + 1, MW), Row(p, cbq + 2, MW), Row(p, cbs, LANES)]
    fox_params = [_pad_lanes(P["fox_fb"], 0), P["fox_q_norm_g"][None], P["fox_k_norm_g"][None]]
    (dfq, dfk_, dfv, dsmall_f), (dfb, dgq_, dgk_) = tile_bwd(t + "fox_prep_b", make_f_fox_prep(H, hd), fox_rows, fox_params,
                                                          [dqn, dkn, dvf, dlogf], tr)
    G["fox_fb"], G["fox_q_norm_g"], G["fox_k_norm_g"] = dfb[0, :H], dgq_[0], dgk_[0]
    dp = jnp.concatenate(dgl + [dfq, dfk_, dfv, dgqkv, dgz, dcu_a, dcu_g, dsmall_f + dsmall_g], axis=1)
    P.emit("w_in", matmul(R["h"], dp, "tn", t + "in_proj_dw", out_dtype=BF16))
    dh = matmul(dp, P.big("w_in", None), "nt", t + "in_proj_dx", deps=P.deps())
    (dx,), (dg,) = tile_bwd(t + "mix_norm_b", f_rms, [Row(R["x"], 0, D)], [P["mix_norm_g"][None]], [dh], tr, adds=[dx1])
    G["mix_norm_g"] = dg[0]
    return dx, dm_l, G


def local_step(x, mem, target, layers, mem_norm_g, dm):
    trm = _pick(mem.shape[0], (256, 128, 64, 32, 16, 8))
    (m,) = tile_fwd("mem_norm", f_rms, [Row(mem, 0, dm.D)], [mem_norm_g[None]], [(dm.D, F32)], trm)
    res = []
    for l, P in enumerate(layers):
        x, R = layer_fwd(x, m, P, dm, f"l{l}_")
        res.append(R)
    loss, dx = loss_head(x, target, "loss_head")
    grads, dm_sum = [None] * len(layers), None
    for l in reversed(range(len(layers))):
        dx, dm_l, grads[l] = layer_bwd(dx, m, layers[l], res[l], dm, f"l{l}_")
        dm_sum = dm_l if dm_sum is None else dm_sum + dm_l
    _, (dg,) = tile_bwd("mem_norm_b", f_rms, [Row(mem, 0, dm.D)], [mem_norm_g[None]], [dm_sum], trm)
    return loss, dx, grads, dg[0]


ARG_NAMES = ["x", "mem", "mix_norm_g", "w_in", "fox_fb", "fox_q_norm_g", "fox_k_norm_g", "gdn_conv_w", "gdn_a_log",
             "gdn_dt_bias", "gdn_out_norm_g", "conf_dw_w", "conf_dw_b", "conf_ln_g", "conf_ln_b", "w_branch", "w_out",
             "mem_norm_g", "xattn_norm_g", "xattn_wq", "xattn_wkv", "xattn_q_norm_g", "xattn_k_norm_g", "xattn_wo",
             "ffn_norm_g", "ffn_w_up", "ffn_conv_w", "ffn_conv_b", "ffn_w_down"]
WEIGHTS = ARG_NAMES[2:]
COL_SHARDED = ["w_in", "gdn_conv_w", "conf_dw_w", "w_branch", "xattn_wo", "ffn_w_up", "ffn_conv_w"]
ROW_SHARDED = ["w_out", "xattn_wq", "xattn_wkv", "ffn_w_down"]
MATMUL_WEIGHTS = ["w_in", "w_branch", "w_out", "xattn_wq", "xattn_wkv", "xattn_wo", "ffn_w_up", "ffn_w_down"]
REPLICATED = [n for n in WEIGHTS if n not in COL_SHARDED + ROW_SHARDED]
SMALL_SHARDED = [n for n in COL_SHARDED + ROW_SHARDED if n not in MATMUL_WEIGHTS]


class LayerWeights:
    def __init__(self, small, fetch, emit, deps):
        self.small, self._fetch, self.emit, self.deps, self._cache = small, fetch, emit, deps, {}

    def __getitem__(self, name):
        return self.small[name]

    def preset(self, name, value):
        self._cache[name] = value

    def big(self, name, after):
        if name not in self._cache:
            self._cache[name] = self._fetch(name, after)
        return self._cache[name]


def regroup(x, blocked_in, name, own=None):
    if blocked_in:
        _, R, n = x.shape
    else:
        R, n = x.shape[0], x.shape[1] // N_DEV
    tr = _pick(R, (512, 256, 128, 64, 32, 16))
    blocked = pl.BlockSpec((None, tr, n), lambda d, i: (d, i, 0))
    flat = pl.BlockSpec((tr, n), lambda d, i: (i, d))

    def body(x_ref, *rest):
        o_ref = rest[-1]
        if own is None:
            o_ref[...] = x_ref[...]
        else:
            o_ref[...] = jnp.where(pl.program_id(0) == _me(), rest[0][...], x_ref[...])

    return pl.pallas_call(
        body, name=name, grid=(N_DEV, R // tr),
        in_specs=[blocked if blocked_in else flat] + ([pl.BlockSpec((tr, n), lambda d, i: (i, 0))] if own is not None else []),
        out_specs=flat if blocked_in else blocked,
        out_shape=jax.ShapeDtypeStruct((R, N_DEV * n) if blocked_in else (N_DEV, R, n), x.dtype),
        compiler_params=_cparams(("parallel", "parallel")),
    )(x, *([own] if own is not None else []))


def _assemble(name, land, own, dm, tag):
    me = _me()
    if name == "w_in":
        return permute_in_blocks([jnp.where(me == d, own, land[d]) for d in range(N_DEV)], dm)
    if name in COL_SHARDED:
        n = land.shape[-1]
        whole = regroup(land.reshape(N_DEV, -1, n), True, tag + "_regroup", own=own.reshape(-1, n))
        return whole.reshape(land.shape[1:-1] + (N_DEV * n,))
    slot = lax.broadcasted_iota(jnp.int32, (N_DEV,) + (1,) * own.ndim, 0)
    return jnp.where(slot == me, own[None], land).reshape((land.shape[0] * land.shape[1],) + land.shape[2:])


def _split(name, g, dm, tag):
    if name == "w_in":
        return unpermute_to_blocks(g, dm)
    if name in COL_SHARDED:
        n = g.shape[-1] // N_DEV
        return regroup(g.reshape(-1, g.shape[-1]), False, tag + "_regroup").reshape((N_DEV,) + g.shape[:-1] + (n,))
    return g.reshape((N_DEV, g.shape[0] // N_DEV) + g.shape[1:])


def _gather_weight(name, w):
    g = all_gather(w, "gather_" + name)
    if name in COL_SHARDED:
        g = jnp.moveaxis(g, 0, -2)
        return g.reshape(g.shape[:-2] + (g.shape[-2] * g.shape[-1],))
    g = jnp.moveaxis(g, 0, 1)
    return g.reshape(g.shape[:1] + (g.shape[1] * g.shape[2],) + g.shape[3:])


def _scatter_grad(name, g):
    if name in COL_SHARDED:
        g = g.reshape(g.shape[:-1] + (N_DEV, g.shape[-1] // N_DEV))
        return jnp.moveaxis(g, -2, 0)
    g = g.reshape(g.shape[:1] + (N_DEV, g.shape[1] // N_DEV) + g.shape[2:])
    return jnp.moveaxis(g, 1, 0)


def _pack(arrs):
    flat = jnp.concatenate([a.reshape(-1) for a in arrs])
    rows = -(-flat.shape[0] // (8 * LANES)) * 8
    return jnp.pad(flat, (0, rows * LANES - flat.shape[0])).reshape(rows, LANES)


def _unpack(packed, like):
    flat, out, o = packed.reshape(-1), [], 0
    for a in like:
        out.append(flat[o:o + a.size].reshape(a.shape))
        o += a.size
    return out


def kernel(x, mem, mix_norm_g, w_in, fox_fb, fox_q_norm_g, fox_k_norm_g, gdn_conv_w, gdn_a_log, gdn_dt_bias, gdn_out_norm_g, conf_dw_w, conf_dw_b, conf_ln_g, conf_ln_b, w_branch, w_out, mem_norm_g, xattn_norm_g, xattn_wq, xattn_wkv, xattn_q_norm_g, xattn_k_norm_g, xattn_wo, ffn_norm_g, ffn_w_up, ffn_conv_w, ffn_conv_b, ffn_w_down, loss_target, m_mix_norm_g, m_w_in, m_fox_fb, m_fox_q_norm_g, m_fox_k_norm_g, m_gdn_conv_w, m_gdn_a_log, m_gdn_dt_bias, m_gdn_out_norm_g, m_conf_dw_w, m_conf_dw_b, m_conf_ln_g, m_conf_ln_b, m_w_branch, m_w_out, m_mem_norm_g, m_xattn_norm_g, m_xattn_wq, m_xattn_wkv, m_xattn_q_norm_g, m_xattn_k_norm_g, m_xattn_wo, m_ffn_norm_g, m_ffn_w_up, m_ffn_conv_w, m_ffn_conv_b, m_ffn_w_down, v_mix_norm_g, v_w_in, v_fox_fb, v_fox_q_norm_g, v_fox_k_norm_g, v_gdn_conv_w, v_gdn_a_log, v_gdn_dt_bias, v_gdn_out_norm_g, v_conf_dw_w, v_conf_dw_b, v_conf_ln_g, v_conf_ln_b, v_w_branch, v_w_out, v_mem_norm_g, v_xattn_norm_g, v_xattn_wq, v_xattn_wkv, v_xattn_q_norm_g, v_xattn_k_norm_g, v_xattn_wo, v_ffn_norm_g, v_ffn_w_up, v_ffn_conv_w, v_ffn_conv_b, v_ffn_w_down):
    args = locals()
    W = {n: args[n] for n in WEIGHTS}
    Mo = {n: args["m_" + n] for n in WEIGHTS}
    Vo = {n: args["v_" + n] for n in WEIGHTS}
    L = mix_norm_g.shape[0]
    H, hd = fox_fb.shape[1], fox_q_norm_g.shape[1]
    dm = Dims(D=x.shape[-1], H=H, hd=hd, MW=H * hd, XH=xattn_wq.shape[-1] // hd, FF=ffn_conv_b.shape[-1])

    gathers, sent, tokens = {}, {}, []

    def take_tokens():
        got = list(tokens)
        tokens.clear()
        return got

    def start_gather(l, n, after):
        blk = W[n][l].astype(BF16)
        handle, token = exchange_start(blk, f"gather{l}_{n}_start", True, after)
        gathers[(l, n)] = (handle, blk)
        tokens.append(token)
        return token

    def fetch(l, n, after):
        handle, blk = gathers.pop((l, n))
        land = exchange_wait(handle, f"gather{l}_{n}_wait", True, blk if after is None else after)
        return _assemble(n, land, blk, dm, f"gather{l}_{n}")

    def emit(l, n, g):
        send = _split(n, g, dm, f"grad{l}_{n}")
        handle, token = exchange_start(send, f"grad{l}_{n}_start", False)
        sent[(l, n)] = (handle, send)
        tokens.append(token)

    start_gather(0, "w_in", None)
    tokens.clear()
    small_full = {n: _gather_weight(n, W[n]) for n in SMALL_SHARDED}
    first_w_in = fetch(0, "w_in", None)
    layers, follow = [], first_w_in
    for l in range(L):
        small = {n: (small_full[n][l] if n in SMALL_SHARDED else W[n][l]) for n in WEIGHTS
                 if n not in MATMUL_WEIGHTS and n != "mem_norm_g"}
        layers.append(LayerWeights(small, functools.partial(fetch, l), functools.partial(emit, l), take_tokens))
        for n in MATMUL_WEIGHTS:
            if (l, n) != (0, "w_in"):
                follow = start_gather(l, n, follow)
    layers[0].preset("w_in", first_w_in)

    loss, dx, grads, d_mem_g = local_step(x[0], mem[0], loss_target[0], layers, mem_norm_g, dm)
    loss = lax.psum(loss, ("x", "y", "c"))

    def whole(n):
        return d_mem_g if n == "mem_norm_g" else jnp.stack([g[n] for g in grads])

    out, after = {}, dx
    for l in reversed(range(L)):
        for n in reversed(MATMUL_WEIGHTS):
            handle, send = sent.pop((l, n))
            land = exchange_wait(handle, f"grad{l}_{n}_wait", False, after)
            out[n] = adamw(land, W[n], Mo[n], Vo[n], f"adamw{l}_{n}", layer=l, into=out.get(n), sent=send)
            after = out[n][0]
    for n in SMALL_SHARDED:
        parts = all_to_all(_scatter_grad(n, whole(n)), "exchange_" + n)
        out[n] = adamw(parts, W[n], Mo[n], Vo[n], "adamw_" + n)
    rep = [whole(n) for n in REPLICATED]
    parts = all_gather(_pack(rep), "gather_small_grads")
    packed = adamw(parts, _pack([W[n] for n in REPLICATED]), _pack([Mo[n] for n in REPLICATED]),
                   _pack([Vo[n] for n in REPLICATED]), "adamw_small")
    unpacked = [_unpack(pk, rep) for pk in packed]
    for i, n in enumerate(REPLICATED):
        out[n] = tuple(u[i] for u in unpacked)

    return (loss, dx[None], *[out[n][0] for n in WEIGHTS], *[out[n][1] for n in WEIGHTS],
            *[out[n][2] for n in WEIGHTS], *[out[n][3] for n in WEIGHTS])
```

```python
import functools
import math
from typing import NamedTuple

import jax
import jax.numpy as jnp
from jax import lax
from jax.experimental import pallas as pl
from jax.experimental.pallas import tpu as pltpu

F32 = jnp.float32
BF16 = jnp.bfloat16
N_DEV = 8
LANES = 128
VMEM_LIMIT = 52 << 20
HI = lax.Precision.HIGHEST

ADAM_LR = 0.001
ADAM_B1 = 0.9
ADAM_B2 = 0.999
ADAM_EPS = 1e-08
ADAM_WD = 0.01
ADAM_STEP = 10

GDN_CHUNK = 64
CONF_HALO = 32
SHORT_HALO = 8
ADAMW_BLOCK_ELEMS = 128 * 1024


def _cparams(sem):
    return pltpu.CompilerParams(dimension_semantics=sem, vmem_limit_bytes=VMEM_LIMIT)


def _pick(n, cands):
    for c in cands:
        if c <= n and n % c == 0:
            return c
    return n


def _peer(k):
    x, y, c = lax.axis_index("x"), lax.axis_index("y"), lax.axis_index("c")
    return (x ^ ((k >> 2) & 1), y ^ ((k >> 1) & 1), c ^ (k & 1))


def _me():
    return 4 * lax.axis_index("x") + 2 * lax.axis_index("y") + lax.axis_index("c")


_HBM = pl.BlockSpec(memory_space=pltpu.HBM)


def _exchange(x, name, gather):
    shape = x.shape if not gather else (N_DEV,) + x.shape

    def body(x_ref, out_ref, send_sems, recv_sems, local_sem):
        me = _me()
        own = pltpu.make_async_copy(x_ref if gather else x_ref.at[me], out_ref.at[me], local_sem)
        own.start()
        sends = []
        for k in range(1, N_DEV):
            peer = me ^ k
            cp = pltpu.make_async_remote_copy(
                src_ref=x_ref if gather else x_ref.at[peer], dst_ref=out_ref.at[me],
                send_sem=send_sems.at[k - 1], recv_sem=recv_sems.at[k - 1],
                device_id=_peer(k), device_id_type=pl.DeviceIdType.MESH)
            cp.start()
            sends.append(cp)
        for cp in sends:
            cp.wait_send()
        for k in range(1, N_DEV):
            peer = me ^ k
            pltpu.make_async_remote_copy(
                src_ref=x_ref if gather else x_ref.at[peer], dst_ref=out_ref.at[peer],
                send_sem=send_sems.at[k - 1], recv_sem=recv_sems.at[k - 1],
                device_id=_peer(k), device_id_type=pl.DeviceIdType.MESH).wait_recv()
        own.wait()

    return pl.pallas_call(
        body, name=name, out_shape=jax.ShapeDtypeStruct(shape, x.dtype),
        in_specs=[_HBM], out_specs=_HBM,
        scratch_shapes=[pltpu.SemaphoreType.DMA((N_DEV - 1,)), pltpu.SemaphoreType.DMA((N_DEV - 1,)),
                        pltpu.SemaphoreType.DMA(())],
    )(x)


def all_gather(x, name):
    return _exchange(x, name, True)


def all_to_all(x, name):
    return _exchange(x, name, False)


_SEM = pl.BlockSpec(memory_space=pltpu.SEMAPHORE)
_ANY = pl.BlockSpec(memory_space=pl.ANY)
_DATAFLOW = pltpu.SideEffectType.DATAFLOW_SIDE_EFFECTING


def _split_copy(k, gather, x_ref, land_ref, send_sems, recv_sems, incoming):
    me = _me()
    peer = me ^ k
    return pltpu.make_async_remote_copy(
        src_ref=x_ref if gather else x_ref.at[peer], dst_ref=land_ref.at[peer if incoming else me],
        send_sem=send_sems.at[k - 1], recv_sem=recv_sems.at[k - 1],
        device_id=_peer(k), device_id_type=pl.DeviceIdType.MESH)


def exchange_start(x, name, gather, after=None):
    land_shape = ((N_DEV,) + x.shape) if gather else x.shape

    def body(x_ref, land_ref, *rest):
        send_sems, recv_sems, _, _, token = rest[-5:]
        for k in range(1, N_DEV):
            _split_copy(k, gather, x_ref, land_ref, send_sems, recv_sems, False).start()
        token[...] = jnp.zeros_like(token)

    sems = pltpu.SemaphoreType.DMA((N_DEV - 1,))
    operands = [pltpu.with_memory_space_constraint(x, pltpu.HBM),
                pltpu.with_memory_space_constraint(lax.empty(land_shape, x.dtype), pltpu.HBM)]
    outs = pl.pallas_call(
        body, name=name,
        out_shape=(sems, sems, pltpu.HBM(x.shape, x.dtype), pltpu.HBM(land_shape, x.dtype),
                   jax.ShapeDtypeStruct((8, LANES), F32)),
        in_specs=[_HBM, _HBM] + ([_ANY] if after is not None else []),
        out_specs=(_SEM, _SEM, _HBM, _HBM, pl.BlockSpec(memory_space=pltpu.VMEM)),
        input_output_aliases={0: 2, 1: 3},
        compiler_params=pltpu.CompilerParams(has_side_effects=_DATAFLOW),
    )(*operands, *([after] if after is not None else []))
    return tuple(outs[:4]), outs[4]


def exchange_wait(handle, name, gather, after):
    send_sems, recv_sems, x_thru, land_thru = handle

    def body(x_ref, land_ref, send_sems, recv_sems, after_ref, x_out, land_out):
        for k in range(1, N_DEV):
            _split_copy(k, gather, x_ref, land_ref, send_sems, recv_sems, False).wait_send()
        for k in range(1, N_DEV):
            _split_copy(k, gather, x_ref, land_ref, send_sems, recv_sems, True).wait_recv()

    return pl.pallas_call(
        body, name=name,
        out_shape=(pltpu.HBM(x_thru.shape, x_thru.dtype), pltpu.HBM(land_thru.shape, land_thru.dtype)),
        in_specs=[_HBM, _HBM, _SEM, _SEM, _ANY], out_specs=(_HBM, _HBM), input_output_aliases={0: 0, 1: 1},
        compiler_params=pltpu.CompilerParams(has_side_effects=_DATAFLOW),
    )(x_thru, land_thru, send_sems, recv_sems, after)[1]


_DIMS = {"nn": (((1,), (0,)), ((), ())), "nt": (((1,), (1,)), ((), ())), "tn": (((0,), (0,)), ((), ()))}


V7X_MXU_FLOPS = 9.0e14
V7X_HBM_BYTES_PER_S = 3.0e12
V7X_VMEM_STORE_BYTES_PER_S = 4.0e12
GRID_STEP_S = 0.35e-6
MATMUL_VMEM_BUDGET = 40 << 20


def _matmul_tiles(M, N, K, a_bytes, b_bytes, has_add, tm_on_lanes):
    def divisors(n, cands):
        got = [c for c in cands if c <= n and n % c == 0]
        return got or [n]

    best = None
    for tn in divisors(N, (2048, 1408, 1024, 512, 256, 128)):
        for tm in divisors(M, (2048, 1408, 1024, 512, 256, 128) + (() if tm_on_lanes else (64, 32, 16, 8))):
            for tk in divisors(K, (2816, 2048, 1408, 1024, 512, 256, 128)):
                nk = K // tk
                vmem = 2 * (tm * tk * a_bytes + tk * tn * b_bytes + tm * tn * 4 * (2 if has_add else 1))
                vmem += tm * tn * 4 * (2 if nk > 1 else 1)
                if vmem > MATMUL_VMEM_BUDGET:
                    continue
                a_reads = (N // tn) if nk > 1 else 1
                hbm = M * K * a_bytes * a_reads + K * N * b_bytes * (M // tm) + M * N * 4 * (2 if has_add else 1)
                t_mxu = 2.0 * M * N * K / V7X_MXU_FLOPS + (M * N * nk * 8 / V7X_VMEM_STORE_BYTES_PER_S if nk > 1 else 0.0)
                t = max(t_mxu, hbm / V7X_HBM_BYTES_PER_S) + GRID_STEP_S * (M // tm) * (N // tn) * nk
                if best is None or t < best[0]:
                    best = (t, tm, tn, tk)
    return best[1:]


def matmul(a, b, mode, name, add=None, out_dtype=F32, deps=()):
    if mode == "nn":
        (M, K), N = a.shape, b.shape[1]
    elif mode == "nt":
        (M, K), N = a.shape, b.shape[0]
    else:
        (K, M), N = a.shape, b.shape[1]
    tm, tn, tk = _matmul_tiles(M, N, K, a.dtype.itemsize, b.dtype.itemsize, add is not None, mode == "tn")
    nk = K // tk
    if mode == "nn":
        a_spec = pl.BlockSpec((tm, tk), lambda i, j, k: (i, k))
        b_spec = pl.BlockSpec((tk, tn), lambda i, j, k: (k, j))
    elif mode == "nt":
        a_spec = pl.BlockSpec((tm, tk), lambda i, j, k: (i, k))
        b_spec = pl.BlockSpec((tn, tk), lambda i, j, k: (j, k))
    else:
        a_spec = pl.BlockSpec((tk, tm), lambda i, j, k: (k, i))
        b_spec = pl.BlockSpec((tk, tn), lambda i, j, k: (k, j))
    o_spec = pl.BlockSpec((tm, tn), lambda i, j, k: (i, j))
    dims = _DIMS[mode]

    def body(*refs):
        a_ref, b_ref = refs[:2]
        add_ref = refs[2] if add is not None else None
        o_ref = refs[(3 if add is not None else 2) + len(deps)]
        part = lax.dot_general(a_ref[...].astype(BF16), b_ref[...].astype(BF16), dims, preferred_element_type=F32)

        def finish(r):
            if add is not None:
                r = r + add_ref[...]
            o_ref[...] = r.astype(o_ref.dtype)

        if nk == 1:
            finish(part)
            return
        acc_ref = refs[-1]
        k = pl.program_id(2)

        @pl.when(k == 0)
        def _():
            acc_ref[...] = part

        @pl.when(k > 0)
        def _():
            acc_ref[...] += part

        @pl.when(k == nk - 1)
        def _():
            finish(acc_ref[...])

    ins = ([a, b] if add is None else [a, b, add]) + list(deps)
    specs = ([a_spec, b_spec] if add is None else [a_spec, b_spec, o_spec]) + [_ANY] * len(deps)
    return pl.pallas_call(
        body, name=name, grid=(M // tm, N // tn, nk), in_specs=specs, out_specs=o_spec,
        out_shape=jax.ShapeDtypeStruct((M, N), out_dtype),
        scratch_shapes=[pltpu.VMEM((tm, tn), F32)] if nk > 1 else [],
        compiler_params=_cparams(("parallel", "parallel", "arbitrary")),
    )(*ins)


class Row(NamedTuple):
    arr: jax.Array
    cb: int
    cw: int
    halo: int = 0


def _row_vals(refs, rows, first):
    vals, it = [], iter(refs)
    for r in rows:
        cur = next(it)[...].astype(F32)
        if r.halo:
            prev = next(it)[...].astype(F32)
            prev = jnp.where(first, jnp.zeros_like(prev), prev)
            cur = jnp.concatenate([prev, cur], axis=0)
        vals.append(cur)
    return vals


def _row_specs(rows, tr, rev_nt=None):
    specs = []
    for r in rows:
        def cur_map(c, i, r=r):
            return ((rev_nt - 1 - i) if rev_nt else i, r.cb + c)
        specs.append(pl.BlockSpec((tr, r.cw), cur_map))
        if r.halo:
            q = tr // r.halo

            def prev_map(c, i, r=r, q=q):
                t = (rev_nt - 1 - i) if rev_nt else i
                return (jnp.maximum(t * q - 1, 0), r.cb + c)
            specs.append(pl.BlockSpec((r.halo, r.cw), prev_map))
    return specs


def _row_args(rows):
    args = []
    for r in rows:
        args.append(r.arr)
        if r.halo:
            args.append(r.arr)
    return args


def _param_specs(params, ncb):
    return [pl.BlockSpec((p.shape[0], p.shape[1] // ncb), lambda c, i: (0, c)) for p in params]


def tile_fwd(name, f, rows, params, outs, tr, ncb=1):
    S = rows[0].arr.shape[0]
    nt = S // tr
    n_in = sum(2 if r.halo else 1 for r in rows)

    def body(*refs):
        first = pl.program_id(1) == 0
        rv = _row_vals(refs[:n_in], rows, first)
        pv = [p[...] for p in refs[n_in:n_in + len(params)]]
        res = f(rv, pv)
        for o_ref, o in zip(refs[n_in + len(params):], res):
            o_ref[...] = o.astype(o_ref.dtype)

    return pl.pallas_call(
        body, name=name, grid=(ncb, nt),
        in_specs=_row_specs(rows, tr) + _param_specs(params, ncb),
        out_specs=[pl.BlockSpec((tr, cw), lambda c, i: (i, c)) for cw, _ in outs],
        out_shape=[jax.ShapeDtypeStruct((S, cw * ncb), dt) for cw, dt in outs],
        compiler_params=_cparams(("parallel", "parallel")),
    )(*_row_args(rows), *params)


def tile_bwd(name, f, rows, params, couts, tr, ncb=1, adds=None):
    S = rows[0].arr.shape[0]
    nt = S // tr
    n_in = sum(2 if r.halo else 1 for r in rows)
    adds = adds or [None] * len(rows)
    add_list = [a for a in adds if a is not None]
    n_p, n_c, n_a, n_r = len(params), len(couts), len(add_list), len(rows)
    halos = [r for r in rows if r.halo]

    def body(*refs):
        i = pl.program_id(1)
        first = i == nt - 1
        pos = 0
        in_refs = refs[pos:pos + n_in]; pos += n_in
        p_refs = refs[pos:pos + n_p]; pos += n_p
        c_refs = refs[pos:pos + n_c]; pos += n_c
        a_refs = list(refs[pos:pos + n_a]); pos += n_a
        dr_refs = refs[pos:pos + n_r]; pos += n_r
        dp_refs = refs[pos:pos + n_p]; pos += n_p
        carry_refs = list(refs[pos:])
        rv = _row_vals(in_refs, rows, first)
        pv = [p[...] for p in p_refs]
        _, vjp = jax.vjp(lambda rv_, pv_: f(rv_, pv_), rv, pv)
        drv, dpv = vjp([c[...].astype(F32) for c in c_refs])
        for r, d, d_ref, a in zip(rows, drv, dr_refs, adds):
            a_val = a_refs.pop(0)[...] if a is not None else None
            if r.halo:
                carry = carry_refs.pop(0)
                cur = d[r.halo:]
                if a_val is not None:
                    cur = cur + a_val
                d_ref[...] = cur

                @pl.when(i > 0)
                def _(d_ref=d_ref, carry=carry, r=r):
                    d_ref[pl.ds(tr - r.halo, r.halo), :] += carry[...]

                carry[...] = d[:r.halo]
            else:
                d_ref[...] = d if a_val is None else d + a_val

        for dp_ref, dp in zip(dp_refs, dpv):
            @pl.when(i == 0)
            def _(dp_ref=dp_ref):
                dp_ref[...] = jnp.zeros_like(dp_ref)

            dp_ref[...] += dp

    rev = lambda c, i: (nt - 1 - i, c)
    return_vals = pl.pallas_call(
        body, name=name, grid=(ncb, nt),
        in_specs=(_row_specs(rows, tr, rev_nt=nt) + _param_specs(params, ncb)
                  + [pl.BlockSpec((tr, c.shape[1] // ncb), rev) for c in couts]
                  + [pl.BlockSpec((tr, a.shape[1] // ncb), rev) for a in add_list]),
        out_specs=([pl.BlockSpec((tr, r.cw), rev) for r in rows] + _param_specs(params, ncb)),
        out_shape=([jax.ShapeDtypeStruct((S, r.cw * ncb), F32) for r in rows]
                   + [jax.ShapeDtypeStruct(p.shape, F32) for p in params]),
        scratch_shapes=[pltpu.VMEM((r.halo, r.cw), F32) for r in halos],
        compiler_params=_cparams(("parallel", "arbitrary")),
    )(*_row_args(rows), *params, *couts, *add_list)
    return list(return_vals[:n_r]), list(return_vals[n_r:])


def adamw(parts, w, m, v, name, layer=None, into=None, sent=None):
    shape = w.shape
    C = shape[-1]
    R = math.prod(shape[:-1])
    rows = R if layer is None else R // shape[0]
    parts2, w2, m2, v2 = parts.reshape(N_DEV, rows, C), w.reshape(R, C), m.reshape(R, C), v.reshape(R, C)
    lanes = -(-C // LANES) * LANES
    tr = _pick(rows, [t for t in (1024, 512, 256, 128, 64, 32, 16, 8) if t * lanes <= ADAMW_BLOCK_ELEMS])
    first = 0 if layer is None else layer * (rows // tr)

    def body(p_ref, w_ref, m_ref, v_ref, *rest):
        g_out, d_out, m_out, v_out = rest[-4:]

        def part(d):
            if sent is None:
                return p_ref[d].astype(F32)
            return jnp.where(_me() == d, rest[0][d], p_ref[d]).astype(F32)

        g = part(0)
        for d in range(1, N_DEV):
            g = g + part(d)
        mm = ADAM_B1 * m_ref[...] + (1.0 - ADAM_B1) * g
        vv = ADAM_B2 * v_ref[...] + (1.0 - ADAM_B2) * jnp.square(g)
        m_hat = mm / (1.0 - ADAM_B1 ** ADAM_STEP)
        v_hat = vv / (1.0 - ADAM_B2 ** ADAM_STEP)
        g_out[...] = g
        d_out[...] = -ADAM_LR * (m_hat / (jnp.sqrt(v_hat) + ADAM_EPS) + ADAM_WD * w_ref[...])
        m_out[...] = mm
        v_out[...] = vv

    blk = pl.BlockSpec((tr, C), lambda i: (first + i, 0))
    prev = [] if into is None else [a.reshape(R, C) for a in into]
    mine = [] if sent is None else [sent.reshape(N_DEV, rows, C)]
    eight = pl.BlockSpec((N_DEV, tr, C), lambda i: (0, i, 0))
    outs = pl.pallas_call(
        body, name=name, grid=(rows // tr,),
        in_specs=[eight, blk, blk, blk] + [eight] * len(mine) + [_ANY] * len(prev),
        out_specs=[blk] * 4, out_shape=[jax.ShapeDtypeStruct((R, C), F32)] * 4,
        input_output_aliases={4 + len(mine) + j: j for j in range(len(prev))},
        compiler_params=_cparams(("parallel",)),
    )(parts2, w2, m2, v2, *mine, *prev)
    return tuple(o.reshape(shape) for o in outs)


def _dot(a, b, mode):
    return lax.dot_general(a.astype(BF16), b.astype(BF16), _DIMS[mode], preferred_element_type=F32)


@jax.custom_vjp
def mm_nn(a, b):
    return _dot(a, b, "nn")


@jax.custom_vjp
def mm_nt(a, b):
    return _dot(a, b, "nt")


@jax.custom_vjp
def mm_tn(a, b):
    return _dot(a, b, "tn")


mm_nn.defvjp(lambda a, b: (_dot(a, b, "nn"), (a, b)), lambda r, g: (mm_nt(g, r[1]), mm_tn(r[0], g)))
mm_nt.defvjp(lambda a, b: (_dot(a, b, "nt"), (a, b)), lambda r, g: (mm_nn(g, r[1]), mm_tn(g, r[0])))
mm_tn.defvjp(lambda a, b: (_dot(a, b, "tn"), (a, b)), lambda r, g: (mm_nt(r[1], g), mm_nn(r[0], g)))


def _mmh(a, b):
    return jnp.dot(a, b, precision=HI, preferred_element_type=F32)


def _dot3(a, b, mode):
    ah, bh = a.astype(BF16), b.astype(BF16)
    al, bl = (a - ah.astype(F32)).astype(BF16), (b - bh.astype(F32)).astype(BF16)

    def d(x, y):
        return lax.dot_general(x, y, _DIMS[mode], preferred_element_type=F32)
    return d(ah, bh) + (d(ah, bl) + d(al, bh))


@jax.custom_vjp
def mm3_nn(a, b):
    return _dot3(a, b, "nn")


mm3_nn.defvjp(lambda a, b: (_dot3(a, b, "nn"), (a, b)), lambda r, g: (_dot3(g, r[1], "nt"), _dot3(r[0], g, "tn")))


def _sigmoid(x):
    return jax.nn.sigmoid(x)


def _silu(x):
    return x * jax.nn.sigmoid(x)


def _softplus(x):
    return jnp.maximum(x, 0.0) + jnp.log(1.0 + jnp.exp(-jnp.abs(x)))


def _log_sigmoid(x):
    return jnp.minimum(x, 0.0) - jnp.log(1.0 + jnp.exp(-jnp.abs(x)))


def _rms(x, g, eps=1e-6):
    return x * lax.rsqrt(jnp.mean(x * x, axis=-1, keepdims=True) + eps) * g


def _heads(fn, x, hd):
    return jnp.concatenate([fn(x[:, h * hd:(h + 1) * hd]) for h in range(x.shape[1] // hd)], axis=1)


def _causal_conv(x, w, halo, tr):
    K = w.shape[0]
    acc = jnp.zeros((tr, x.shape[1]), F32)
    for k in range(K):
        o = halo - (K - 1) + k
        acc = acc + w[k:k + 1] * x[o:o + tr]
    return acc


def _lanes(shape):
    return lax.broadcasted_iota(jnp.int32, shape, len(shape) - 1)


def f_rms(rv, pv):
    return [_rms(rv[0], pv[0])]


def make_f_fox_prep(H, hd):
    def f(rv, pv):
        fq, fk, fv, small = rv
        fb, gq, gk = pv
        qn = _heads(lambda t: _rms(t, gq), fq, hd) * (hd ** -0.5)
        kn = _heads(lambda t: _rms(t, gk), fk, hd)
        logf = jnp.where(_lanes(small.shape) < H, _log_sigmoid(small + fb), 0.0)
        return [qn, kn, fv, logf]
    return f


def make_f_short_conv(tr, act):
    def f(rv, pv):
        y = _causal_conv(rv[0], pv[0], SHORT_HALO, tr)
        return [_silu(y) if act else y]
    return f


def make_f_gdn_prep(H, hd):
    def l2(t):
        return t * lax.rsqrt(jnp.sum(t * t, axis=-1, keepdims=True) + 1e-6)

    def f(rv, pv):
        q, k, small = rv
        alog, dtb = pv
        qn = _heads(l2, q, hd) * (hd ** -0.5)
        kn = _heads(l2, k, hd)
        ln = _lanes(small.shape)
        g = -jnp.exp(alog) * _softplus(small + dtb)
        gates = jnp.where((ln >= H) & (ln < 2 * H), g, jnp.where((ln >= 2 * H) & (ln < 3 * H), _sigmoid(small), 0.0))
        return [qn, kn, gates]
    return f


def _tri(n, strict=False, upper=False):
    ii = lax.broadcasted_iota(jnp.int32, (n, n), 0)
    jj = lax.broadcasted_iota(jnp.int32, (n, n), 1)
    if upper:
        ii, jj = jj, ii
    return ii > jj if strict else ii >= jj


def _decay(gcol, mask):
    C = gcol.shape[0]
    G = jnp.broadcast_to(gcol, (C, C))
    return jnp.where(mask, jnp.exp(jnp.where(mask, G - G.T, 0.0)), 0.0)


def _nilpotent_inverses(Xs):
    C = Xs[0].shape[0]
    eye = (lax.broadcasted_iota(jnp.int32, (C, C), 0) == lax.broadcasted_iota(jnp.int32, (C, C), 1)).astype(F32)
    Ts, Ps = [eye + X for X in Xs], list(Xs)
    for _ in range(int(math.log2(C)) - 1):
        Ps = [mm3_nn(P, P) for P in Ps]
        Ts = [T + mm3_nn(T, P) for T, P in zip(Ts, Ps)]
    return Ts


@jax.custom_vjp
def _saved_inverse(X, T):
    return T


_saved_inverse.defvjp(lambda X, T: (T, T),
                      lambda T, g: (_dot3(T, _dot3(g, T, "nt"), "tn"), jnp.zeros_like(T)))


def make_f_gdn_solve(H, hd, saved):
    C = GDN_CHUNK

    def f(rv, pv):
        k, v, gates = rv[:3]
        gc = _mmh(_tri(C).astype(F32), gates)
        strict = _tri(C, strict=True)
        heads = range(H)
        ks = [k[:, h * hd:(h + 1) * hd] for h in heads]
        gcols = [gc[:, H + h:H + h + 1] for h in heads]
        betas = [gates[:, 2 * H + h:2 * H + h + 1] for h in heads]
        kbs = [kh * b for kh, b in zip(ks, betas)]
        vbs = [v[:, h * hd:(h + 1) * hd] * betas[h] for h in heads]
        Xs = [-(mm_nt(kb, kh) * _decay(g, strict)) for kb, kh, g in zip(kbs, ks, gcols)]
        if saved:
            Ts = [_saved_inverse(X, rv[3][:, h * C:(h + 1) * C]) for h, X in enumerate(Xs)]
        else:
            Ts = _nilpotent_inverses(Xs)
        us = [mm3_nn(T, vb) for T, vb in zip(Ts, vbs)]
        ws = [mm3_nn(T, kb * jnp.exp(g)) for T, kb, g in zip(Ts, kbs, gcols)]
        out = [jnp.concatenate(us, axis=1), jnp.concatenate(ws, axis=1), gc]
        return out if saved else out + [jnp.concatenate(Ts, axis=1)]
    return f


def _gdn_steps(Ss, qs, ks, us, ws, gcols):
    C = qs[0].shape[0]
    causal = _tri(C)
    rows = lax.broadcasted_iota(jnp.int32, gcols[0].shape, 0)
    attn = [mm_nt(q, k) * _decay(g, causal) for q, k, g in zip(qs, ks, gcols)]
    glast = [jnp.sum(jnp.where(rows == C - 1, g, 0.0), axis=0, keepdims=True) for g in gcols]
    v_new = [u - mm_nn(w, S) for u, w, S in zip(us, ws, Ss)]
    o_state = [mm_nn(q * jnp.exp(g), S) for q, g, S in zip(qs, gcols, Ss)]
    o_chunk = [mm_nn(a, vn) for a, vn in zip(attn, v_new)]
    update = [mm_tn(k * jnp.exp(gl - g), vn) for k, gl, g, vn in zip(ks, glast, gcols, v_new)]
    S_new = [S * jnp.exp(gl) + d for S, gl, d in zip(Ss, glast, update)]
    return [a + b for a, b in zip(o_state, o_chunk)], S_new


def make_f_gdn_out(hd):
    def f(rv, pv):
        o, gz = rv
        return [_heads(lambda t: _rms(t, pv[0]), o, hd) * _silu(gz)]
    return f


def make_f_conf(tr):
    def f(rv, pv):
        a, g = rv
        w, b, lg, lb = pv
        y = _causal_conv(a * _sigmoid(g), w, CONF_HALO, tr) + b
        xc = y - jnp.mean(y, axis=-1, keepdims=True)
        y = xc * lax.rsqrt(jnp.mean(xc * xc, axis=-1, keepdims=True) + 1e-5) * lg + lb
        return [_silu(y)]
    return f


def f_merge(rv, pv):
    n = len(rv) // 2
    y = _sigmoid(rv[0]) * rv[n]
    for j in range(1, n):
        y = y + _sigmoid(rv[j]) * rv[n + j]
    return [y]


def make_f_xattn(XH, hd):
    XW = XH * hd

    def f(rv, pv):
        q = rv[0]
        kv, gq, gk = pv
        outs = []
        for h in range(XH):
            qh = _rms(q[:, h * hd:(h + 1) * hd], gq)
            kh = _rms(kv[:, h * hd:(h + 1) * hd], gk)
            s = mm_nt(qh, kh) * (hd ** -0.5)
            e = jnp.exp(s - jnp.max(s, axis=-1, keepdims=True))
            outs.append(mm_nn(e / jnp.sum(e, axis=-1, keepdims=True), kv[:, XW + h * hd:XW + (h + 1) * hd]))
        return [jnp.concatenate(outs, axis=1)]
    return f


def make_f_ffn_act(tr):
    def f(rv, pv):
        a, v = rv
        return [_silu(_causal_conv(a, pv[0], SHORT_HALO, tr) + pv[1]) * v]
    return f


def make_f_dprep(H, hd):
    def f(rv, pv):
        do, o = rv
        ind = (lax.broadcasted_iota(jnp.int32, (H * hd, LANES), 0) // hd
               == lax.broadcasted_iota(jnp.int32, (H * hd, LANES), 1)).astype(F32)
        return [do, _mmh(do * o, ind)]
    return f


def cumsum_rows(x, name, reverse=False):
    S, C = x.shape
    tr = _pick(S, (256, 128, 64))
    nt = S // tr

    def body(x_ref, o_ref, carry):
        @pl.when(pl.program_id(0) == 0)
        def _():
            carry[...] = jnp.zeros_like(carry)

        ii = lax.broadcasted_iota(jnp.int32, (tr, tr), 0)
        jj = lax.broadcasted_iota(jnp.int32, (tr, tr), 1)
        tri = (ii <= jj) if reverse else (ii >= jj)
        y = _mmh(tri.astype(F32), x_ref[...]) + carry[...]
        o_ref[...] = y
        rows = lax.broadcasted_iota(jnp.int32, y.shape, 0)
        carry[...] = jnp.sum(jnp.where(rows == (0 if reverse else tr - 1), y, 0.0), axis=0, keepdims=True)

    spec = pl.BlockSpec((tr, C), (lambda i: (nt - 1 - i, 0)) if reverse else (lambda i: (i, 0)))
    return pl.pallas_call(
        body, name=name, grid=(nt,), in_specs=[spec], out_specs=spec,
        out_shape=jax.ShapeDtypeStruct((S, C), F32), scratch_shapes=[pltpu.VMEM((1, C), F32)],
        compiler_params=_cparams(("arbitrary",)),
    )(x)


def _fox_block(S):
    return _pick(S, (1024, 512, 256, 128))


def fox_fwd(q, k, v, frow, H, hd, name):
    S = q.shape[0]
    bq = _fox_block(S)
    nq = S // bq

    def body(q_ref, k_ref, v_ref, fr_ref, o_ref, lse_ref):
        i = pl.program_id(1)
        qv = q_ref[...]

        def step(j, carry, diag):
            m, l, acc = carry
            cols = pl.ds(pl.multiple_of(j * bq, bq), bq)
            kj, vj = k_ref[cols, :], v_ref[cols, :]
            s = lax.dot_general(qv, kj, _DIMS["nt"], preferred_element_type=F32) - fr_ref[0, :, cols]
            if diag:
                s = jnp.where(_tri(bq), s, -jnp.inf)
            m_new = jnp.maximum(m, jnp.max(s, axis=-1, keepdims=True))
            alpha = jnp.exp(m - m_new)
            p = jnp.exp(s - m_new)
            l = alpha * l + jnp.sum(p, axis=-1, keepdims=True)
            acc = alpha * acc + lax.dot_general(p.astype(BF16), vj, _DIMS["nn"], preferred_element_type=F32)
            return m_new, l, acc

        init = (jnp.full((bq, 1), -jnp.inf, F32), jnp.zeros((bq, 1), F32), jnp.zeros((bq, hd), F32))
        carry = lax.fori_loop(0, i, lambda j, c: step(j, c, False), init)
        m, l, acc = step(i, carry, True)
        o_ref[...] = acc / l
        lse_ref[0] = m + jnp.log(l)

    return pl.pallas_call(
        body, name=name, grid=(H, nq),
        in_specs=[pl.BlockSpec((bq, hd), lambda h, i: (i, h)),
                  pl.BlockSpec((S, hd), lambda h, i: (0, h)), pl.BlockSpec((S, hd), lambda h, i: (0, h)),
                  pl.BlockSpec((1, 1, S), lambda h, i: (h, 0, 0))],
        out_specs=[pl.BlockSpec((bq, hd), lambda h, i: (i, h)), pl.BlockSpec((1, bq, 1), lambda h, i: (h, i, 0))],
        out_shape=[jax.ShapeDtypeStruct((S, H * hd), F32), jax.ShapeDtypeStruct((H, S, 1), F32)],
        compiler_params=_cparams(("parallel", "parallel")),
    )(q, k, v, frow)


def fox_bwd(q, k, v, do, fcol, lse_row, delta_row, H, hd, name):
    S = q.shape[0]
    bk = _fox_block(S)
    nk = S // bk

    def body(q_ref, do_ref, k_ref, v_ref, fc_ref, lse_ref, dl_ref, dq_ref, dfq_ref, dk_ref, dv_ref, df_ref):
        j = pl.program_id(1)

        @pl.when(j == 0)
        def _():
            dq_ref[...] = jnp.zeros_like(dq_ref)
            dfq_ref[...] = jnp.zeros_like(dfq_ref)

        kj, vj = k_ref[...], v_ref[...]
        fk = fc_ref[0]

        def step(i, carry, diag):
            dk, dv, df = carry
            rows = pl.ds(pl.multiple_of(i * bk, bk), bk)
            qi, doi = q_ref[rows, :], do_ref[rows, :]
            st = lax.dot_general(kj, qi, _DIMS["nt"], preferred_element_type=F32) - fk - lse_ref[0, :, rows]
            if diag:
                st = jnp.where(_tri(bk, upper=True), st, -jnp.inf)
            pt = jnp.exp(st)
            dv = dv + lax.dot_general(pt.astype(BF16), doi, _DIMS["nn"], preferred_element_type=F32)
            dpt = lax.dot_general(vj, doi, _DIMS["nt"], preferred_element_type=F32)
            dst = pt * (dpt - dl_ref[0, :, rows])
            df = df - jnp.sum(dst, axis=-1, keepdims=True)
            dfq_ref[0, :, rows] += jnp.sum(dst, axis=0, keepdims=True)
            dsb = dst.astype(BF16)
            dk = dk + lax.dot_general(dsb, qi, _DIMS["nn"], preferred_element_type=F32)
            dq_ref[rows, :] += lax.dot_general(dsb, kj, _DIMS["tn"], preferred_element_type=F32)
            return dk, dv, df

        init = (jnp.zeros((bk, hd), F32), jnp.zeros((bk, hd), F32), jnp.zeros((bk, 1), F32))
        carry = step(j, init, True)
        dk, dv, df = lax.fori_loop(j + 1, nk, lambda i, c: step(i, c, False), carry)
        dk_ref[...] = dk
        dv_ref[...] = dv
        df_ref[0] = df

    whole = pl.BlockSpec((S, hd), lambda h, j: (0, h))
    blk = pl.BlockSpec((bk, hd), lambda h, j: (j, h))
    row = pl.BlockSpec((1, 1, S), lambda h, j: (h, 0, 0))
    col = pl.BlockSpec((1, bk, 1), lambda h, j: (h, j, 0))
    return pl.pallas_call(
        body, name=name, grid=(H, nk),
        in_specs=[whole, whole, blk, blk, col, row, row],
        out_specs=[whole, row, blk, blk, col],
        out_shape=[jax.ShapeDtypeStruct((S, H * hd), F32), jax.ShapeDtypeStruct((H, 1, S), F32)]
        + [jax.ShapeDtypeStruct((S, H * hd), F32)] * 2 + [jax.ShapeDtypeStruct((H, S, 1), F32)],
        compiler_params=_cparams(("parallel", "arbitrary")),
    )(q, do, k, v, fcol, lse_row, delta_row)


def gdn_scan_fwd(q, k, u, w, gc, H, hd, name):
    S = q.shape[0]
    C = GDN_CHUNK
    NC = S // C

    def body(q_ref, k_ref, u_ref, w_ref, gc_ref, o_ref, sin_ref, state):
        @pl.when(pl.program_id(0) == 0)
        def _():
            state[...] = jnp.zeros_like(state)

        gcv = gc_ref[...]
        sl = [slice(h * hd, (h + 1) * hd) for h in range(H)]
        Ss = [state[h] for h in range(H)]
        for h in range(H):
            sin_ref[0, h] = Ss[h]
        outs, S_new = _gdn_steps(Ss, [q_ref[:, s] for s in sl], [k_ref[:, s] for s in sl], [u_ref[:, s] for s in sl],
                                 [w_ref[:, s] for s in sl], [gcv[:, H + h:H + h + 1] for h in range(H)])
        for h in range(H):
            state[h] = S_new[h]
        o_ref[...] = jnp.concatenate(outs, axis=1)

    wide = pl.BlockSpec((C, H * hd), lambda i: (i, 0))
    return pl.pallas_call(
        body, name=name, grid=(NC,),
        in_specs=[wide] * 4 + [pl.BlockSpec((C, LANES), lambda i: (i, 0))],
        out_specs=[wide, pl.BlockSpec((1, H, hd, hd), lambda i: (i, 0, 0, 0))],
        out_shape=[jax.ShapeDtypeStruct((S, H * hd), F32), jax.ShapeDtypeStruct((NC, H, hd, hd), F32)],
        scratch_shapes=[pltpu.VMEM((H, hd, hd), F32)],
        compiler_params=_cparams(("arbitrary",)),
    )(q, k, u, w, gc)


def gdn_scan_bwd(q, k, u, w, gc, sin, do, H, hd, name):
    S = q.shape[0]
    C = GDN_CHUNK
    NC = S // C

    def body(q_ref, k_ref, u_ref, w_ref, gc_ref, sin_ref, do_ref, dq_ref, dk_ref, du_ref, dw_ref, dgc_ref, dstate):
        @pl.when(pl.program_id(0) == 0)
        def _():
            dstate[...] = jnp.zeros_like(dstate)

        gcv = gc_ref[...]
        ln = _lanes(gcv.shape)
        dgc = jnp.zeros_like(gcv)
        sl = [slice(h * hd, (h + 1) * hd) for h in range(H)]
        _, vjp = jax.vjp(_gdn_steps, [sin_ref[0, h] for h in range(H)], [q_ref[:, s] for s in sl],
                         [k_ref[:, s] for s in sl], [u_ref[:, s] for s in sl], [w_ref[:, s] for s in sl],
                         [gcv[:, H + h:H + h + 1] for h in range(H)])
        dS, dq, dk, du, dw, dg = vjp(([do_ref[:, s] for s in sl], [dstate[h] for h in range(H)]))
        for h in range(H):
            dstate[h] = dS[h]
            dgc = dgc + jnp.where(ln == H + h, dg[h], 0.0)
        for ref, lst in zip((dq_ref, dk_ref, du_ref, dw_ref), (dq, dk, du, dw)):
            ref[...] = jnp.concatenate(lst, axis=1)
        dgc_ref[...] = dgc

    wide = pl.BlockSpec((C, H * hd), lambda i: (NC - 1 - i, 0))
    narrow = pl.BlockSpec((C, LANES), lambda i: (NC - 1 - i, 0))
    return pl.pallas_call(
        body, name=name, grid=(NC,),
        in_specs=[wide] * 4 + [narrow, pl.BlockSpec((1, H, hd, hd), lambda i: (NC - 1 - i, 0, 0, 0)), wide],
        out_specs=[wide] * 4 + [narrow],
        out_shape=[jax.ShapeDtypeStruct((S, H * hd), F32)] * 4 + [jax.ShapeDtypeStruct((S, LANES), F32)],
        scratch_shapes=[pltpu.VMEM((H, hd, hd), F32)],
        compiler_params=_cparams(("arbitrary",)),
    )(q, k, u, w, gc, sin, do)


def loss_head(y, t, name):
    S, D = y.shape
    tr = _pick(S, (256, 128, 64))

    def body(y_ref, t_ref, dy_ref, acc_ref):
        @pl.when(pl.program_id(0) == 0)
        def _():
            acc_ref[...] = jnp.zeros_like(acc_ref)

        err = y_ref[...] - t_ref[...]
        dy_ref[...] = err / D
        acc_ref[...] += jnp.sum(jnp.mean(err * err, axis=-1, keepdims=True), axis=0, keepdims=True)

    blk = pl.BlockSpec((tr, D), lambda i: (i, 0))
    dy, acc = pl.pallas_call(
        body, name=name, grid=(S // tr,), in_specs=[blk, blk],
        out_specs=[blk, pl.BlockSpec((8, LANES), lambda i: (0, 0))],
        out_shape=[jax.ShapeDtypeStruct((S, D), F32), jax.ShapeDtypeStruct((8, LANES), F32)],
        compiler_params=_cparams(("arbitrary",)),
    )(y, t)
    return 0.5 * acc[0, 0], dy


class Dims(NamedTuple):
    D: int
    H: int
    hd: int
    MW: int
    XH: int
    FF: int

    @property
    def n_in(self):
        return 9 * self.MW + 3 * self.H + 3 * self.D

    @property
    def n_in_padded(self):
        return 3 * self.D + 9 * self.MW + LANES


def _in_pieces(dm):
    MW, H, D = dm.MW, dm.H, dm.D
    fq, fk, fv, ff = 0, MW, 2 * MW, 3 * MW
    gq = ff + H
    gk, gv = gq + MW, gq + 2 * MW
    ga = gv + MW
    gb, gz = ga + H, ga + 2 * H
    cu = gz + MW
    gl = cu + 2 * MW
    return [(gl, 3 * D), (fq, MW), (fk, MW), (fv, MW), (gq, MW), (gk, MW), (gv, MW), (gz, MW), (cu, 2 * MW),
            (ff, H), (ga, H), (gb, H)]


def permute_in_blocks(blocks, dm):
    n = blocks[0].shape[-1]
    parts = []
    for s, size in _in_pieces(dm):
        e = s + size
        while s < e:
            d = s // n
            hi = min(e, (d + 1) * n)
            parts.append(blocks[d][:, s - d * n:hi - d * n])
            s = hi
    parts.append(jnp.zeros((blocks[0].shape[0], LANES - 3 * dm.H), blocks[0].dtype))
    return jnp.concatenate(parts, axis=1)


def unpermute_to_blocks(wp, dm):
    segs, off = [], 0
    for s, size in _in_pieces(dm):
        segs.append((s, size, off))
        off += size
    segs.sort()
    n = dm.n_in // N_DEV
    chunks = []
    for d in range(N_DEV):
        lo, hi = d * n, (d + 1) * n
        parts = [wp[:, off + max(s, lo) - s:off + min(s + size, hi) - s] for s, size, off in segs
                 if max(s, lo) < min(s + size, hi)]
        chunks.append(jnp.concatenate(parts, axis=1))
    return jnp.stack(chunks, axis=0)


def _pad_lanes(v, at):
    return jnp.pad(v, (at, LANES - at - v.shape[0]))[None]


def _ops(x, m, P, dm, t):
    S = x.shape[0]
    tr = _pick(S, (256, 128, 64))
    return dict(tr=tr, trm=_pick(S, (128, 64)), trx=_pick(S, (512, 256, 128, 64)),
                cbq=3 * dm.D // dm.MW, cbs=(3 * dm.D + 9 * dm.MW) // LANES,
                cwf=_pick(dm.FF, (512, 256, 128)))


def layer_fwd(x, m, P, dm, t):
    D, H, hd, MW, XH, FF = dm
    XW = XH * hd
    c = _ops(x, m, P, dm, t)
    tr, cbq, cbs = c["tr"], c["cbq"], c["cbs"]
    (h,) = tile_fwd(t + "mix_norm", f_rms, [Row(x, 0, D)], [P["mix_norm_g"][None]], [(D, BF16)], tr)
    p = matmul(h, P.big("w_in", h), "nn", t + "in_proj", deps=P.deps())
    fox_rows = [Row(p, cbq, MW), Row(p, cbq + 1, MW), Row(p, cbq + 2, MW), Row(p, cbs, LANES)]
    fox_params = [_pad_lanes(P["fox_fb"], 0), P["fox_q_norm_g"][None], P["fox_k_norm_g"][None]]
    qn, kn, vb, logf = tile_fwd(t + "fox_prep", make_f_fox_prep(H, hd), fox_rows, fox_params,
                                [(MW, BF16)] * 3 + [(LANES, F32)], tr)
    Ft = cumsum_rows(logf, t + "fox_cumsum")[:, :H].T
    fcol, frow = Ft[:, :, None], Ft[:, None, :]
    out_a, lse = fox_fwd(qn, kn, vb, frow, H, hd, t + "fox_attn")
    (qkv,) = tile_fwd(t + "gdn_conv", make_f_short_conv(tr, True), [Row(p, cbq + 3, MW, SHORT_HALO)],
                      [P["gdn_conv_w"]], [(MW, F32)], tr, ncb=3)
    gp_rows = [Row(qkv, 0, MW), Row(qkv, 1, MW), Row(p, cbs, LANES)]
    gp_params = [_pad_lanes(P["gdn_a_log"], H), _pad_lanes(P["gdn_dt_bias"], H)]
    gq, gk, gates = tile_fwd(t + "gdn_prep", make_f_gdn_prep(H, hd), gp_rows, gp_params,
                             [(MW, F32), (MW, F32), (LANES, F32)], tr)
    u, w, gc, tinv = tile_fwd(t + "gdn_solve", make_f_gdn_solve(H, hd, False),
                              [Row(gk, 0, MW), Row(qkv, 2, MW), Row(gates, 0, LANES)], [],
                              [(MW, F32), (MW, F32), (LANES, F32), (H * GDN_CHUNK, F32)], GDN_CHUNK)
    o_g, sin = gdn_scan_fwd(gq, gk, u, w, gc, H, hd, t + "gdn_scan")
    (out_b,) = tile_fwd(t + "gdn_out", make_f_gdn_out(hd), [Row(o_g, 0, MW), Row(p, cbq + 6, MW)],
                        [P["gdn_out_norm_g"][None]], [(MW, F32)], tr)
    conf_params = [P["conf_dw_w"], P["conf_dw_b"][None], P["conf_ln_g"][None], P["conf_ln_b"][None]]
    (out_c,) = tile_fwd(t + "conf", make_f_conf(tr), [Row(p, cbq + 7, MW, CONF_HALO), Row(p, cbq + 8, MW, CONF_HALO)],
                        conf_params, [(MW, F32)], tr)
    branches = [out_a, out_b, out_c]
    proj = [matmul(b, P.big("w_branch", out_a)[n], "nn", t + f"branch{n}") for n, b in enumerate(branches)]
    (y,) = tile_fwd(t + "merge", f_merge, [Row(p, n, D) for n in range(3)] + [Row(pr, 0, D) for pr in proj], [],
                    [(D, BF16)], c["trm"])
    x1 = matmul(y, P.big("w_out", y), "nn", t + "out_proj", add=x)
    (h2,) = tile_fwd(t + "xa_norm", f_rms, [Row(x1, 0, D)], [P["xattn_norm_g"][None]], [(D, BF16)], tr)
    q = matmul(h2, P.big("xattn_wq", h2), "nn", t + "xa_q")
    kv = matmul(m, P.big("xattn_wkv", h2), "nn", t + "xa_kv")
    xa_params = [kv, P["xattn_q_norm_g"][None], P["xattn_k_norm_g"][None]]
    (o_x,) = tile_fwd(t + "xa_attn", make_f_xattn(XH, hd), [Row(q, 0, XW)], xa_params, [(XW, F32)], c["trx"])
    x2 = matmul(o_x, P.big("xattn_wo", o_x), "nn", t + "xa_o", add=x1)
    (h3,) = tile_fwd(t + "ffn_norm", f_rms, [Row(x2, 0, D)], [P["ffn_norm_g"][None]], [(D, BF16)], tr)
    av = matmul(h3, P.big("ffn_w_up", h3), "nn", t + "ffn_up")
    cwf = c["cwf"]
    (uf,) = tile_fwd(t + "ffn_act", make_f_ffn_act(tr), [Row(av, 0, cwf, SHORT_HALO), Row(av, FF // cwf, cwf)],
                     [P["ffn_conv_w"], P["ffn_conv_b"][None]], [(cwf, BF16)], tr, ncb=FF // cwf)
    x3 = matmul(uf, P.big("ffn_w_down", uf), "nn", t + "ffn_down", add=x2)
    res = dict(x=x, h=h, p=p, qn=qn, kn=kn, vb=vb, fcol=fcol, frow=frow, out_a=out_a, lse=lse, qkv=qkv, gq=gq, gk=gk,
               gates=gates, u=u, w=w, gc=gc, tinv=tinv, sin=sin, o_g=o_g, out_b=out_b, out_c=out_c, proj=proj, y=y, x1=x1, h2=h2,
               q=q, kv=kv, o_x=o_x, x2=x2, h3=h3, av=av, uf=uf)
    return x3, res


def layer_bwd(dx3, m, P, R, dm, t):
    D, H, hd, MW, XH, FF = dm
    XW = XH * hd
    c = _ops(R["x"], m, P, dm, t)
    tr, cbq, cbs = c["tr"], c["cbq"], c["cbs"]
    p = R["p"]
    G = {}
    P.emit("ffn_w_down", matmul(R["uf"], dx3, "tn", t + "ffn_down_dw", out_dtype=BF16))
    du = matmul(dx3, P.big("ffn_w_down", None), "nt", t + "ffn_down_dx", deps=P.deps())
    cwf = c["cwf"]
    (da, dv), (G["ffn_conv_w"], dcb) = tile_bwd(
        t + "ffn_act_b", make_f_ffn_act(tr), [Row(R["av"], 0, cwf, SHORT_HALO), Row(R["av"], FF // cwf, cwf)],
        [P["ffn_conv_w"], P["ffn_conv_b"][None]], [du], tr, ncb=FF // cwf)
    G["ffn_conv_b"] = dcb[0]
    dav = jnp.concatenate([da, dv], axis=1)
    P.emit("ffn_w_up", matmul(R["h3"], dav, "tn", t + "ffn_up_dw", out_dtype=BF16))
    dh3 = matmul(dav, P.big("ffn_w_up", None), "nt", t + "ffn_up_dx", deps=P.deps())
    (dx2,), (dg,) = tile_bwd(t + "ffn_norm_b", f_rms, [Row(R["x2"], 0, D)], [P["ffn_norm_g"][None]], [dh3], tr, adds=[dx3])
    G["ffn_norm_g"] = dg[0]
    P.emit("xattn_wo", matmul(R["o_x"], dx2, "tn", t + "xa_o_dw", out_dtype=BF16))
    do_x = matmul(dx2, P.big("xattn_wo", None), "nt", t + "xa_o_dx", deps=P.deps())
    xa_params = [R["kv"], P["xattn_q_norm_g"][None], P["xattn_k_norm_g"][None]]
    (dq,), (dkv, dgq, dgk) = tile_bwd(t + "xa_attn_b", make_f_xattn(XH, hd), [Row(R["q"], 0, XW)], xa_params, [do_x], c["trx"])
    G["xattn_q_norm_g"], G["xattn_k_norm_g"] = dgq[0], dgk[0]
    P.emit("xattn_wq", matmul(R["h2"], dq, "tn", t + "xa_q_dw", out_dtype=BF16))
    P.emit("xattn_wkv", matmul(m, dkv, "tn", t + "xa_kv_dw", out_dtype=BF16))
    dh2 = matmul(dq, P.big("xattn_wq", None), "nt", t + "xa_q_dx", deps=P.deps())
    dm_l = matmul(dkv, P.big("xattn_wkv", None), "nt", t + "xa_kv_dx")
    (dx1,), (dg,) = tile_bwd(t + "xa_norm_b", f_rms, [Row(R["x1"], 0, D)], [P["xattn_norm_g"][None]], [dh2], tr, adds=[dx2])
    G["xattn_norm_g"] = dg[0]
    P.emit("w_out", matmul(R["y"], dx1, "tn", t + "out_proj_dw", out_dtype=BF16))
    dy = matmul(dx1, P.big("w_out", None), "nt", t + "out_proj_dx", deps=P.deps())
    merge_rows = [Row(p, n, D) for n in range(3)] + [Row(pr, 0, D) for pr in R["proj"]]
    dmerge, _ = tile_bwd(t + "merge_b", f_merge, merge_rows, [], [dy], c["trm"])
    dgl, dpr = dmerge[:3], dmerge[3:]
    branches = [R["out_a"], R["out_b"], R["out_c"]]
    P.emit("w_branch", jnp.stack([matmul(branches[n], dpr[n], "tn", t + f"branch{n}_dw", out_dtype=BF16)
                                  for n in range(3)]))
    dbr = [matmul(dpr[n], P.big("w_branch", None)[n], "nt", t + f"branch{n}_dx", deps=P.deps()) for n in range(3)]
    conf_params = [P["conf_dw_w"], P["conf_dw_b"][None], P["conf_ln_g"][None], P["conf_ln_b"][None]]
    (dcu_a, dcu_g), (G["conf_dw_w"], db, dlg, dlb) = tile_bwd(
        t + "conf_b", make_f_conf(tr), [Row(p, cbq + 7, MW, CONF_HALO), Row(p, cbq + 8, MW, CONF_HALO)],
        conf_params, [dbr[2]], tr)
    G["conf_dw_b"], G["conf_ln_g"], G["conf_ln_b"] = db[0], dlg[0], dlb[0]
    (do_g, dgz), (dg,) = tile_bwd(t + "gdn_out_b", make_f_gdn_out(hd), [Row(R["o_g"], 0, MW), Row(p, cbq + 6, MW)],
                                  [P["gdn_out_norm_g"][None]], [dbr[1]], tr)
    G["gdn_out_norm_g"] = dg[0]
    dgq, dgk2, du_, dw_, dgc = gdn_scan_bwd(R["gq"], R["gk"], R["u"], R["w"], R["gc"], R["sin"], do_g, H, hd, t + "gdn_scan_b")
    (dgk1, dgv, dgates, _), _ = tile_bwd(
        t + "gdn_solve_b", make_f_gdn_solve(H, hd, True),
        [Row(R["gk"], 0, MW), Row(R["qkv"], 2, MW), Row(R["gates"], 0, LANES), Row(R["tinv"], 0, H * GDN_CHUNK)],
        [], [du_, dw_, dgc], GDN_CHUNK)
    gp_rows = [Row(R["qkv"], 0, MW), Row(R["qkv"], 1, MW), Row(p, cbs, LANES)]
    gp_params = [_pad_lanes(P["gdn_a_log"], H), _pad_lanes(P["gdn_dt_bias"], H)]
    (dqa, dka, dsmall_g), (dal, ddt) = tile_bwd(t + "gdn_prep_b", make_f_gdn_prep(H, hd), gp_rows, gp_params,
                                               [dgq, dgk1 + dgk2, dgates], tr)
    G["gdn_a_log"], G["gdn_dt_bias"] = dal[0, H:2 * H], ddt[0, H:2 * H]
    (dgqkv,), (G["gdn_conv_w"],) = tile_bwd(
        t + "gdn_conv_b", make_f_short_conv(tr, True), [Row(p, cbq + 3, MW, SHORT_HALO)], [P["gdn_conv_w"]],
        [jnp.concatenate([dqa, dka, dgv], axis=1)], tr, ncb=3)
    do_b, delta = tile_fwd(t + "fox_dprep", make_f_dprep(H, hd), [Row(dbr[0], 0, MW), Row(R["out_a"], 0, MW)], [],
                           [(MW, BF16), (LANES, F32)], tr)
    S = p.shape[0]
    delta_row = delta[:, :H].T[:, None, :]
    lse_row = R["lse"].reshape(H, 1, S)
    dqn, dfq_, dkn, dvf, dfk = fox_bwd(R["qn"], R["kn"], R["vb"], do_b, R["fcol"], lse_row, delta_row, H, hd,
                                       t + "fox_attn_b")
    dF = jnp.pad((dfk.reshape(H, S) + dfq_.reshape(H, S)).T, ((0, 0), (0, LANES - H)))
    dlogf = cumsum_rows(dF, t + "fox_cumsum_b", reverse=True)
    fox_rows = [Row(p, cbq, MW), Row(p, cbq + 1, MW), Row(p, cbq + 2, MW), Row(p, cbs, LANES)]
    fox_params = [_pad_lanes(P["fox_fb"], 0), P["fox_q_norm_g"][None], P["fox_k_norm_g"][None]]
    (dfq, dfk_, dfv, dsmall_f), (dfb, dgq_, dgk_) = tile_bwd(t + "fox_prep_b", make_f_fox_prep(H, hd), fox_rows, fox_params,
                                                          [dqn, dkn, dvf, dlogf], tr)
    G["fox_fb"], G["fox_q_norm_g"], G["fox_k_norm_g"] = dfb[0, :H], dgq_[0], dgk_[0]
    dp = jnp.concatenate(dgl + [dfq, dfk_, dfv, dgqkv, dgz, dcu_a, dcu_g, dsmall_f + dsmall_g], axis=1)
    P.emit("w_in", matmul(R["h"], dp, "tn", t + "in_proj_dw", out_dtype=BF16))
    dh = matmul(dp, P.big("w_in", None), "nt", t + "in_proj_dx", deps=P.deps())
    (dx,), (dg,) = tile_bwd(t + "mix_norm_b", f_rms, [Row(R["x"], 0, D)], [P["mix_norm_g"][None]], [dh], tr, adds=[dx1])
    G["mix_norm_g"] = dg[0]
    return dx, dm_l, G


def local_step(x, mem, target, layers, mem_norm_g, dm):
    trm = _pick(mem.shape[0], (256, 128, 64, 32, 16, 8))
    (m,) = tile_fwd("mem_norm", f_rms, [Row(mem, 0, dm.D)], [mem_norm_g[None]], [(dm.D, F32)], trm)
    res = []
    for l, P in enumerate(layers):
        x, R = layer_fwd(x, m, P, dm, f"l{l}_")
        res.append(R)
    loss, dx = loss_head(x, target, "loss_head")
    grads, dm_sum = [None] * len(layers), None
    for l in reversed(range(len(layers))):
        dx, dm_l, grads[l] = layer_bwd(dx, m, layers[l], res[l], dm, f"l{l}_")
        dm_sum = dm_l if dm_sum is None else dm_sum + dm_l
    _, (dg,) = tile_bwd("mem_norm_b", f_rms, [Row(mem, 0, dm.D)], [mem_norm_g[None]], [dm_sum], trm)
    return loss, dx, grads, dg[0]


ARG_NAMES = ["x", "mem", "mix_norm_g", "w_in", "fox_fb", "fox_q_norm_g", "fox_k_norm_g", "gdn_conv_w", "gdn_a_log",
             "gdn_dt_bias", "gdn_out_norm_g", "conf_dw_w", "conf_dw_b", "conf_ln_g", "conf_ln_b", "w_branch", "w_out",
             "mem_norm_g", "xattn_norm_g", "xattn_wq", "xattn_wkv", "xattn_q_norm_g", "xattn_k_norm_g", "xattn_wo",
             "ffn_norm_g", "ffn_w_up", "ffn_conv_w", "ffn_conv_b", "ffn_w_down"]
WEIGHTS = ARG_NAMES[2:]
COL_SHARDED = ["w_in", "gdn_conv_w", "conf_dw_w", "w_branch", "xattn_wo", "ffn_w_up", "ffn_conv_w"]
ROW_SHARDED = ["w_out", "xattn_wq", "xattn_wkv", "ffn_w_down"]
MATMUL_WEIGHTS = ["w_in", "w_branch", "w_out", "xattn_wq", "xattn_wkv", "xattn_wo", "ffn_w_up", "ffn_w_down"]
REPLICATED = [n for n in WEIGHTS if n not in COL_SHARDED + ROW_SHARDED]
SMALL_SHARDED = [n for n in COL_SHARDED + ROW_SHARDED if n not in MATMUL_WEIGHTS]


class LayerWeights:
    def __init__(self, small, fetch, emit, deps):
        self.small, self._fetch, self.emit, self.deps, self._cache = small, fetch, emit, deps, {}

    def __getitem__(self, name):
        return self.small[name]

    def preset(self, name, value):
        self._cache[name] = value

    def big(self, name, after):
        if name not in self._cache:
            self._cache[name] = self._fetch(name, after)
        return self._cache[name]


def regroup(x, blocked_in, name, own=None):
    if blocked_in:
        _, R, n = x.shape
    else:
        R, n = x.shape[0], x.shape[1] // N_DEV
    tr = _pick(R, (512, 256, 128, 64, 32, 16))
    blocked = pl.BlockSpec((None, tr, n), lambda d, i: (d, i, 0))
    flat = pl.BlockSpec((tr, n), lambda d, i: (i, d))

    def body(x_ref, *rest):
        o_ref = rest[-1]
        if own is None:
            o_ref[...] = x_ref[...]
        else:
            o_ref[...] = jnp.where(pl.program_id(0) == _me(), rest[0][...], x_ref[...])

    return pl.pallas_call(
        body, name=name, grid=(N_DEV, R // tr),
        in_specs=[blocked if blocked_in else flat] + ([pl.BlockSpec((tr, n), lambda d, i: (i, 0))] if own is not None else []),
        out_specs=flat if blocked_in else blocked,
        out_shape=jax.ShapeDtypeStruct((R, N_DEV * n) if blocked_in else (N_DEV, R, n), x.dtype),
        compiler_params=_cparams(("parallel", "parallel")),
    )(x, *([own] if own is not None else []))


def _assemble(name, land, own, dm, tag):
    me = _me()
    if name == "w_in":
        return permute_in_blocks([jnp.where(me == d, own, land[d]) for d in range(N_DEV)], dm)
    if name in COL_SHARDED:
        n = land.shape[-1]
        whole = regroup(land.reshape(N_DEV, -1, n), True, tag + "_regroup", own=own.reshape(-1, n))
        return whole.reshape(land.shape[1:-1] + (N_DEV * n,))
    slot = lax.broadcasted_iota(jnp.int32, (N_DEV,) + (1,) * own.ndim, 0)
    return jnp.where(slot == me, own[None], land).reshape((land.shape[0] * land.shape[1],) + land.shape[2:])


def _split(name, g, dm, tag):
    if name == "w_in":
        return unpermute_to_blocks(g, dm)
    if name in COL_SHARDED:
        n = g.shape[-1] // N_DEV
        return regroup(g.reshape(-1, g.shape[-1]), False, tag + "_regroup").reshape((N_DEV,) + g.shape[:-1] + (n,))
    return g.reshape((N_DEV, g.shape[0] // N_DEV) + g.shape[1:])


def _gather_small_weights(shards):
    names = list(shards)
    like = [shards[n] for n in names]
    g = all_gather(_pack(like), "gather_small_weights")
    per_dev = [_unpack(g[d], like) for d in range(N_DEV)]
    return {n: jnp.concatenate([per_dev[d][i] for d in range(N_DEV)], axis=-1) for i, n in enumerate(names)}


def _scatter_grad(name, g):
    if name in COL_SHARDED:
        g = g.reshape(g.shape[:-1] + (N_DEV, g.shape[-1] // N_DEV))
        return jnp.moveaxis(g, -2, 0)
    g = g.reshape(g.shape[:1] + (N_DEV, g.shape[1] // N_DEV) + g.shape[2:])
    return jnp.moveaxis(g, 1, 0)


def _pack(arrs):
    flat = jnp.concatenate([a.reshape(-1) for a in arrs])
    rows = -(-flat.shape[0] // (8 * LANES)) * 8
    return jnp.pad(flat, (0, rows * LANES - flat.shape[0])).reshape(rows, LANES)


def _unpack(packed, like):
    flat, out, o = packed.reshape(-1), [], 0
    for a in like:
        out.append(flat[o:o + a.size].reshape(a.shape))
        o += a.size
    return out


def kernel(x, mem, mix_norm_g, w_in, fox_fb, fox_q_norm_g, fox_k_norm_g, gdn_conv_w, gdn_a_log, gdn_dt_bias, gdn_out_norm_g, conf_dw_w, conf_dw_b, conf_ln_g, conf_ln_b, w_branch, w_out, mem_norm_g, xattn_norm_g, xattn_wq, xattn_wkv, xattn_q_norm_g, xattn_k_norm_g, xattn_wo, ffn_norm_g, ffn_w_up, ffn_conv_w, ffn_conv_b, ffn_w_down, loss_target, m_mix_norm_g, m_w_in, m_fox_fb, m_fox_q_norm_g, m_fox_k_norm_g, m_gdn_conv_w, m_gdn_a_log, m_gdn_dt_bias, m_gdn_out_norm_g, m_conf_dw_w, m_conf_dw_b, m_conf_ln_g, m_conf_ln_b, m_w_branch, m_w_out, m_mem_norm_g, m_xattn_norm_g, m_xattn_wq, m_xattn_wkv, m_xattn_q_norm_g, m_xattn_k_norm_g, m_xattn_wo, m_ffn_norm_g, m_ffn_w_up, m_ffn_conv_w, m_ffn_conv_b, m_ffn_w_down, v_mix_norm_g, v_w_in, v_fox_fb, v_fox_q_norm_g, v_fox_k_norm_g, v_gdn_conv_w, v_gdn_a_log, v_gdn_dt_bias, v_gdn_out_norm_g, v_conf_dw_w, v_conf_dw_b, v_conf_ln_g, v_conf_ln_b, v_w_branch, v_w_out, v_mem_norm_g, v_xattn_norm_g, v_xattn_wq, v_xattn_wkv, v_xattn_q_norm_g, v_xattn_k_norm_g, v_xattn_wo, v_ffn_norm_g, v_ffn_w_up, v_ffn_conv_w, v_ffn_conv_b, v_ffn_w_down):
    args = locals()
    W = {n: args[n] for n in WEIGHTS}
    Mo = {n: args["m_" + n] for n in WEIGHTS}
    Vo = {n: args["v_" + n] for n in WEIGHTS}
    L = mix_norm_g.shape[0]
    H, hd = fox_fb.shape[1], fox_q_norm_g.shape[1]
    dm = Dims(D=x.shape[-1], H=H, hd=hd, MW=H * hd, XH=xattn_wq.shape[-1] // hd, FF=ffn_conv_b.shape[-1])

    gathers, sent, tokens = {}, {}, []

    def take_tokens():
        got = list(tokens)
        tokens.clear()
        return got

    def start_gather(l, n, after):
        blk = W[n][l].astype(BF16)
        handle, token = exchange_start(blk, f"gather{l}_{n}_start", True, after)
        gathers[(l, n)] = (handle, blk)
        tokens.append(token)
        return token

    def fetch(l, n, after):
        handle, blk = gathers.pop((l, n))
        land = exchange_wait(handle, f"gather{l}_{n}_wait", True, blk if after is None else after)
        return _assemble(n, land, blk, dm, f"gather{l}_{n}")

    def emit(l, n, g):
        send = _split(n, g, dm, f"grad{l}_{n}")
        handle, token = exchange_start(send, f"grad{l}_{n}_start", False)
        sent[(l, n)] = (handle, send)
        tokens.append(token)

    small_full = _gather_small_weights({n: W[n] for n in SMALL_SHARDED})
    layers, follow = [], small_full[SMALL_SHARDED[0]]
    for l in range(L):
        small = {n: (small_full[n][l] if n in SMALL_SHARDED else W[n][l]) for n in WEIGHTS
                 if n not in MATMUL_WEIGHTS and n != "mem_norm_g"}
        layers.append(LayerWeights(small, functools.partial(fetch, l), functools.partial(emit, l), take_tokens))
        for n in MATMUL_WEIGHTS:
            follow = start_gather(l, n, follow)

    loss, dx, grads, d_mem_g = local_step(x[0], mem[0], loss_target[0], layers, mem_norm_g, dm)
    loss = lax.psum(loss, ("x", "y", "c"))

    def whole(n):
        return d_mem_g if n == "mem_norm_g" else jnp.stack([g[n] for g in grads])

    out, after = {}, dx
    for l in reversed(range(L)):
        for n in reversed(MATMUL_WEIGHTS):
            handle, send = sent.pop((l, n))
            land = exchange_wait(handle, f"grad{l}_{n}_wait", False, after)
            out[n] = adamw(land, W[n], Mo[n], Vo[n], f"adamw{l}_{n}", layer=l, into=out.get(n), sent=send)
            after = out[n][0]
    for n in SMALL_SHARDED:
        parts = all_to_all(_scatter_grad(n, whole(n)), "exchange_" + n)
        out[n] = adamw(parts, W[n], Mo[n], Vo[n], "adamw_" + n)
    rep = [whole(n) for n in REPLICATED]
    parts = all_gather(_pack(rep), "gather_small_grads")
    packed = adamw(parts, _pack([W[n] for n in REPLICATED]), _pack([Mo[n] for n in REPLICATED]),
                   _pack([Vo[n] for n in REPLICATED]), "adamw_small")
    unpacked = [_unpack(pk, rep) for pk in packed]
    for i, n in enumerate(REPLICATED):
        out[n] = tuple(u[i] for u in unpacked)

    return (loss, dx[None], *[out[n][0] for n in WEIGHTS], *[out[n][1] for n in WEIGHTS],
            *[out[n][2] for n in WEIGHTS], *[out[n][3] for n in WEIGHTS])
```

```python
import functools
import math
from typing import NamedTuple

import jax
import jax.numpy as jnp
from jax import lax
from jax.experimental import pallas as pl
from jax.experimental.pallas import tpu as pltpu

F32 = jnp.float32
BF16 = jnp.bfloat16
N_DEV = 8
LANES = 128
VMEM_LIMIT = 52 << 20
HI = lax.Precision.HIGHEST

ADAM_LR = 0.001
ADAM_B1 = 0.9
ADAM_B2 = 0.999
ADAM_EPS = 1e-08
ADAM_WD = 0.01
ADAM_STEP = 10

GDN_CHUNK = 64
CONF_HALO = 32
SHORT_HALO = 8
ADAMW_BLOCK_ELEMS = 128 * 1024


def _cparams(sem):
    return pltpu.CompilerParams(dimension_semantics=sem, vmem_limit_bytes=VMEM_LIMIT)


def _pick(n, cands):
    for c in cands:
        if c <= n and n % c == 0:
            return c
    return n


def _peer(k):
    x, y, c = lax.axis_index("x"), lax.axis_index("y"), lax.axis_index("c")
    return (x ^ ((k >> 2) & 1), y ^ ((k >> 1) & 1), c ^ (k & 1))


def _me():
    return 4 * lax.axis_index("x") + 2 * lax.axis_index("y") + lax.axis_index("c")


_HBM = pl.BlockSpec(memory_space=pltpu.HBM)


def _exchange(x, name, gather):
    shape = x.shape if not gather else (N_DEV,) + x.shape

    def body(x_ref, out_ref, send_sems, recv_sems, local_sem):
        me = _me()
        own = pltpu.make_async_copy(x_ref if gather else x_ref.at[me], out_ref.at[me], local_sem)
        own.start()
        sends = []
        for k in range(1, N_DEV):
            peer = me ^ k
            cp = pltpu.make_async_remote_copy(
                src_ref=x_ref if gather else x_ref.at[peer], dst_ref=out_ref.at[me],
                send_sem=send_sems.at[k - 1], recv_sem=recv_sems.at[k - 1],
                device_id=_peer(k), device_id_type=pl.DeviceIdType.MESH)
            cp.start()
            sends.append(cp)
        for cp in sends:
            cp.wait_send()
        for k in range(1, N_DEV):
            peer = me ^ k
            pltpu.make_async_remote_copy(
                src_ref=x_ref if gather else x_ref.at[peer], dst_ref=out_ref.at[peer],
                send_sem=send_sems.at[k - 1], recv_sem=recv_sems.at[k - 1],
                device_id=_peer(k), device_id_type=pl.DeviceIdType.MESH).wait_recv()
        own.wait()

    return pl.pallas_call(
        body, name=name, out_shape=jax.ShapeDtypeStruct(shape, x.dtype),
        in_specs=[_HBM], out_specs=_HBM,
        scratch_shapes=[pltpu.SemaphoreType.DMA((N_DEV - 1,)), pltpu.SemaphoreType.DMA((N_DEV - 1,)),
                        pltpu.SemaphoreType.DMA(())],
    )(x)


def all_gather(x, name):
    return _exchange(x, name, True)


def all_to_all(x, name):
    return _exchange(x, name, False)


_SEM = pl.BlockSpec(memory_space=pltpu.SEMAPHORE)
_ANY = pl.BlockSpec(memory_space=pl.ANY)
_DATAFLOW = pltpu.SideEffectType.DATAFLOW_SIDE_EFFECTING


def _split_copy(k, gather, x_ref, land_ref, send_sems, recv_sems, incoming):
    me = _me()
    peer = me ^ k
    return pltpu.make_async_remote_copy(
        src_ref=x_ref if gather else x_ref.at[peer], dst_ref=land_ref.at[peer if incoming else me],
        send_sem=send_sems.at[k - 1], recv_sem=recv_sems.at[k - 1],
        device_id=_peer(k), device_id_type=pl.DeviceIdType.MESH)


def exchange_start(x, name, gather, after=None):
    land_shape = ((N_DEV,) + x.shape) if gather else x.shape

    def body(x_ref, land_ref, *rest):
        send_sems, recv_sems, _, _, token = rest[-5:]
        for k in range(1, N_DEV):
            _split_copy(k, gather, x_ref, land_ref, send_sems, recv_sems, False).start()
        token[...] = jnp.zeros_like(token)

    sems = pltpu.SemaphoreType.DMA((N_DEV - 1,))
    operands = [pltpu.with_memory_space_constraint(x, pltpu.HBM),
                pltpu.with_memory_space_constraint(lax.empty(land_shape, x.dtype), pltpu.HBM)]
    outs = pl.pallas_call(
        body, name=name,
        out_shape=(sems, sems, pltpu.HBM(x.shape, x.dtype), pltpu.HBM(land_shape, x.dtype),
                   jax.ShapeDtypeStruct((8, LANES), F32)),
        in_specs=[_HBM, _HBM] + ([_ANY] if after is not None else []),
        out_specs=(_SEM, _SEM, _HBM, _HBM, pl.BlockSpec(memory_space=pltpu.VMEM)),
        input_output_aliases={0: 2, 1: 3},
        compiler_params=pltpu.CompilerParams(has_side_effects=_DATAFLOW),
    )(*operands, *([after] if after is not None else []))
    return tuple(outs[:4]), outs[4]


def exchange_wait(handle, name, gather, after):
    send_sems, recv_sems, x_thru, land_thru = handle

    def body(x_ref, land_ref, send_sems, recv_sems, after_ref, x_out, land_out):
        for k in range(1, N_DEV):
            _split_copy(k, gather, x_ref, land_ref, send_sems, recv_sems, False).wait_send()
        for k in range(1, N_DEV):
            _split_copy(k, gather, x_ref, land_ref, send_sems, recv_sems, True).wait_recv()

    return pl.pallas_call(
        body, name=name,
        out_shape=(pltpu.HBM(x_thru.shape, x_thru.dtype), pltpu.HBM(land_thru.shape, land_thru.dtype)),
        in_specs=[_HBM, _HBM, _SEM, _SEM, _ANY], out_specs=(_HBM, _HBM), input_output_aliases={0: 0, 1: 1},
        compiler_params=pltpu.CompilerParams(has_side_effects=_DATAFLOW),
    )(x_thru, land_thru, send_sems, recv_sems, after)[1]


_DIMS = {"nn": (((1,), (0,)), ((), ())), "nt": (((1,), (1,)), ((), ())), "tn": (((0,), (0,)), ((), ()))}


V7X_MXU_FLOPS = 9.0e14
V7X_HBM_BYTES_PER_S = 3.0e12
V7X_VMEM_STORE_BYTES_PER_S = 4.0e12
GRID_STEP_S = 0.35e-6
MATMUL_VMEM_BUDGET = 40 << 20


def _matmul_tiles(M, N, K, a_bytes, b_bytes, has_add, tm_on_lanes):
    def divisors(n, cands):
        got = [c for c in cands if c <= n and n % c == 0]
        return got or [n]

    best = None
    for tn in divisors(N, (2048, 1408, 1024, 512, 256, 128)):
        for tm in divisors(M, (2048, 1408, 1024, 512, 256, 128) + (() if tm_on_lanes else (64, 32, 16, 8))):
            for tk in divisors(K, (2816, 2048, 1408, 1024, 512, 256, 128)):
                nk = K // tk
                vmem = 2 * (tm * tk * a_bytes + tk * tn * b_bytes + tm * tn * 4 * (2 if has_add else 1))
                vmem += tm * tn * 4 * (2 if nk > 1 else 1)
                if vmem > MATMUL_VMEM_BUDGET:
                    continue
                a_reads = (N // tn) if nk > 1 else 1
                hbm = M * K * a_bytes * a_reads + K * N * b_bytes * (M // tm) + M * N * 4 * (2 if has_add else 1)
                t_mxu = 2.0 * M * N * K / V7X_MXU_FLOPS + (M * N * nk * 8 / V7X_VMEM_STORE_BYTES_PER_S if nk > 1 else 0.0)
                t = max(t_mxu, hbm / V7X_HBM_BYTES_PER_S) + GRID_STEP_S * (M // tm) * (N // tn) * nk
                if best is None or t < best[0]:
                    best = (t, tm, tn, tk)
    return best[1:]


def matmul(a, b, mode, name, add=None, out_dtype=F32, deps=()):
    if mode == "nn":
        (M, K), N = a.shape, b.shape[1]
    elif mode == "nt":
        (M, K), N = a.shape, b.shape[0]
    else:
        (K, M), N = a.shape, b.shape[1]
    tm, tn, tk = _matmul_tiles(M, N, K, a.dtype.itemsize, b.dtype.itemsize, add is not None, mode == "tn")
    nk = K // tk
    if mode == "nn":
        a_spec = pl.BlockSpec((tm, tk), lambda i, j, k: (i, k))
        b_spec = pl.BlockSpec((tk, tn), lambda i, j, k: (k, j))
    elif mode == "nt":
        a_spec = pl.BlockSpec((tm, tk), lambda i, j, k: (i, k))
        b_spec = pl.BlockSpec((tn, tk), lambda i, j, k: (j, k))
    else:
        a_spec = pl.BlockSpec((tk, tm), lambda i, j, k: (k, i))
        b_spec = pl.BlockSpec((tk, tn), lambda i, j, k: (k, j))
    o_spec = pl.BlockSpec((tm, tn), lambda i, j, k: (i, j))
    dims = _DIMS[mode]

    def body(*refs):
        a_ref, b_ref = refs[:2]
        add_ref = refs[2] if add is not None else None
        o_ref = refs[(3 if add is not None else 2) + len(deps)]
        part = lax.dot_general(a_ref[...].astype(BF16), b_ref[...].astype(BF16), dims, preferred_element_type=F32)

        def finish(r):
            if add is not None:
                r = r + add_ref[...]
            o_ref[...] = r.astype(o_ref.dtype)

        if nk == 1:
            finish(part)
            return
        acc_ref = refs[-1]
        k = pl.program_id(2)

        @pl.when(k == 0)
        def _():
            acc_ref[...] = part

        @pl.when(k > 0)
        def _():
            acc_ref[...] += part

        @pl.when(k == nk - 1)
        def _():
            finish(acc_ref[...])

    ins = ([a, b] if add is None else [a, b, add]) + list(deps)
    specs = ([a_spec, b_spec] if add is None else [a_spec, b_spec, o_spec]) + [_ANY] * len(deps)
    return pl.pallas_call(
        body, name=name, grid=(M // tm, N // tn, nk), in_specs=specs, out_specs=o_spec,
        out_shape=jax.ShapeDtypeStruct((M, N), out_dtype),
        scratch_shapes=[pltpu.VMEM((tm, tn), F32)] if nk > 1 else [],
        compiler_params=_cparams(("parallel", "parallel", "arbitrary")),
    )(*ins)


class Row(NamedTuple):
    arr: jax.Array
    cb: int
    cw: int
    halo: int = 0


def _row_vals(refs, rows, first):
    vals, it = [], iter(refs)
    for r in rows:
        cur = next(it)[...].astype(F32)
        if r.halo:
            prev = next(it)[...].astype(F32)
            prev = jnp.where(first, jnp.zeros_like(prev), prev)
            cur = jnp.concatenate([prev, cur], axis=0)
        vals.append(cur)
    return vals


def _row_specs(rows, tr, rev_nt=None):
    specs = []
    for r in rows:
        def cur_map(c, i, r=r):
            return ((rev_nt - 1 - i) if rev_nt else i, r.cb + c)
        specs.append(pl.BlockSpec((tr, r.cw), cur_map))
        if r.halo:
            q = tr // r.halo

            def prev_map(c, i, r=r, q=q):
                t = (rev_nt - 1 - i) if rev_nt else i
                return (jnp.maximum(t * q - 1, 0), r.cb + c)
            specs.append(pl.BlockSpec((r.halo, r.cw), prev_map))
    return specs


def _row_args(rows):
    args = []
    for r in rows:
        args.append(r.arr)
        if r.halo:
            args.append(r.arr)
    return args


def _param_specs(params, ncb):
    return [pl.BlockSpec((p.shape[0], p.shape[1] // ncb), lambda c, i: (0, c)) for p in params]


def tile_fwd(name, f, rows, params, outs, tr, ncb=1):
    S = rows[0].arr.shape[0]
    nt = S // tr
    n_in = sum(2 if r.halo else 1 for r in rows)

    def body(*refs):
        first = pl.program_id(1) == 0
        rv = _row_vals(refs[:n_in], rows, first)
        pv = [p[...] for p in refs[n_in:n_in + len(params)]]
        res = f(rv, pv)
        for o_ref, o in zip(refs[n_in + len(params):], res):
            o_ref[...] = o.astype(o_ref.dtype)

    return pl.pallas_call(
        body, name=name, grid=(ncb, nt),
        in_specs=_row_specs(rows, tr) + _param_specs(params, ncb),
        out_specs=[pl.BlockSpec((tr, cw), lambda c, i: (i, c)) for cw, _ in outs],
        out_shape=[jax.ShapeDtypeStruct((S, cw * ncb), dt) for cw, dt in outs],
        compiler_params=_cparams(("parallel", "parallel")),
    )(*_row_args(rows), *params)


def tile_bwd(name, f, rows, params, couts, tr, ncb=1, adds=None):
    S = rows[0].arr.shape[0]
    nt = S // tr
    n_in = sum(2 if r.halo else 1 for r in rows)
    adds = adds or [None] * len(rows)
    add_list = [a for a in adds if a is not None]
    n_p, n_c, n_a, n_r = len(params), len(couts), len(add_list), len(rows)
    halos = [r for r in rows if r.halo]

    def body(*refs):
        i = pl.program_id(1)
        first = i == nt - 1
        pos = 0
        in_refs = refs[pos:pos + n_in]; pos += n_in
        p_refs = refs[pos:pos + n_p]; pos += n_p
        c_refs = refs[pos:pos + n_c]; pos += n_c
        a_refs = list(refs[pos:pos + n_a]); pos += n_a
        dr_refs = refs[pos:pos + n_r]; pos += n_r
        dp_refs = refs[pos:pos + n_p]; pos += n_p
        carry_refs = list(refs[pos:])
        rv = _row_vals(in_refs, rows, first)
        pv = [p[...] for p in p_refs]
        _, vjp = jax.vjp(lambda rv_, pv_: f(rv_, pv_), rv, pv)
        drv, dpv = vjp([c[...].astype(F32) for c in c_refs])
        for r, d, d_ref, a in zip(rows, drv, dr_refs, adds):
            a_val = a_refs.pop(0)[...] if a is not None else None
            if r.halo:
                carry = carry_refs.pop(0)
                cur = d[r.halo:]
                if a_val is not None:
                    cur = cur + a_val
                d_ref[...] = cur

                @pl.when(i > 0)
                def _(d_ref=d_ref, carry=carry, r=r):
                    d_ref[pl.ds(tr - r.halo, r.halo), :] += carry[...]

                carry[...] = d[:r.halo]
            else:
                d_ref[...] = d if a_val is None else d + a_val

        for dp_ref, dp in zip(dp_refs, dpv):
            @pl.when(i == 0)
            def _(dp_ref=dp_ref):
                dp_ref[...] = jnp.zeros_like(dp_ref)

            dp_ref[...] += dp

    rev = lambda c, i: (nt - 1 - i, c)
    return_vals = pl.pallas_call(
        body, name=name, grid=(ncb, nt),
        in_specs=(_row_specs(rows, tr, rev_nt=nt) + _param_specs(params, ncb)
                  + [pl.BlockSpec((tr, c.shape[1] // ncb), rev) for c in couts]
                  + [pl.BlockSpec((tr, a.shape[1] // ncb), rev) for a in add_list]),
        out_specs=([pl.BlockSpec((tr, r.cw), rev) for r in rows] + _param_specs(params, ncb)),
        out_shape=([jax.ShapeDtypeStruct((S, r.cw * ncb), F32) for r in rows]
                   + [jax.ShapeDtypeStruct(p.shape, F32) for p in params]),
        scratch_shapes=[pltpu.VMEM((r.halo, r.cw), F32) for r in halos],
        compiler_params=_cparams(("parallel", "arbitrary")),
    )(*_row_args(rows), *params, *couts, *add_list)
    return list(return_vals[:n_r]), list(return_vals[n_r:])


def adamw(parts, w, m, v, name, layer=None, into=None, sent=None):
    shape = w.shape
    C = shape[-1]
    R = math.prod(shape[:-1])
    rows = R if layer is None else R // shape[0]
    parts2, w2, m2, v2 = parts.reshape(N_DEV, rows, C), w.reshape(R, C), m.reshape(R, C), v.reshape(R, C)
    lanes = -(-C // LANES) * LANES
    tr = _pick(rows, [t for t in (1024, 512, 256, 128, 64, 32, 16, 8) if t * lanes <= ADAMW_BLOCK_ELEMS])
    first = 0 if layer is None else layer * (rows // tr)

    def body(p_ref, w_ref, m_ref, v_ref, *rest):
        g_out, d_out, m_out, v_out = rest[-4:]

        def part(d):
            if sent is None:
                return p_ref[d].astype(F32)
            return jnp.where(_me() == d, rest[0][d], p_ref[d]).astype(F32)

        g = part(0)
        for d in range(1, N_DEV):
            g = g + part(d)
        mm = ADAM_B1 * m_ref[...] + (1.0 - ADAM_B1) * g
        vv = ADAM_B2 * v_ref[...] + (1.0 - ADAM_B2) * jnp.square(g)
        m_hat = mm / (1.0 - ADAM_B1 ** ADAM_STEP)
        v_hat = vv / (1.0 - ADAM_B2 ** ADAM_STEP)
        g_out[...] = g
        d_out[...] = -ADAM_LR * (m_hat / (jnp.sqrt(v_hat) + ADAM_EPS) + ADAM_WD * w_ref[...])
        m_out[...] = mm
        v_out[...] = vv

    blk = pl.BlockSpec((tr, C), lambda i: (first + i, 0))
    prev = [] if into is None else [a.reshape(R, C) for a in into]
    mine = [] if sent is None else [sent.reshape(N_DEV, rows, C)]
    eight = pl.BlockSpec((N_DEV, tr, C), lambda i: (0, i, 0))
    outs = pl.pallas_call(
        body, name=name, grid=(rows // tr,),
        in_specs=[eight, blk, blk, blk] + [eight] * len(mine) + [_ANY] * len(prev),
        out_specs=[blk] * 4, out_shape=[jax.ShapeDtypeStruct((R, C), F32)] * 4,
        input_output_aliases={4 + len(mine) + j: j for j in range(len(prev))},
        compiler_params=_cparams(("parallel",)),
    )(parts2, w2, m2, v2, *mine, *prev)
    return tuple(o.reshape(shape) for o in outs)


def _dot(a, b, mode):
    return lax.dot_general(a.astype(BF16), b.astype(BF16), _DIMS[mode], preferred_element_type=F32)


@jax.custom_vjp
def mm_nn(a, b):
    return _dot(a, b, "nn")


@jax.custom_vjp
def mm_nt(a, b):
    return _dot(a, b, "nt")


@jax.custom_vjp
def mm_tn(a, b):
    return _dot(a, b, "tn")


mm_nn.defvjp(lambda a, b: (_dot(a, b, "nn"), (a, b)), lambda r, g: (mm_nt(g, r[1]), mm_tn(r[0], g)))
mm_nt.defvjp(lambda a, b: (_dot(a, b, "nt"), (a, b)), lambda r, g: (mm_nn(g, r[1]), mm_tn(g, r[0])))
mm_tn.defvjp(lambda a, b: (_dot(a, b, "tn"), (a, b)), lambda r, g: (mm_nt(r[1], g), mm_nn(r[0], g)))


def _mmh(a, b):
    return jnp.dot(a, b, precision=HI, preferred_element_type=F32)


def _dot3(a, b, mode):
    ah, bh = a.astype(BF16), b.astype(BF16)
    al, bl = (a - ah.astype(F32)).astype(BF16), (b - bh.astype(F32)).astype(BF16)

    def d(x, y):
        return lax.dot_general(x, y, _DIMS[mode], preferred_element_type=F32)
    return d(ah, bh) + (d(ah, bl) + d(al, bh))


@jax.custom_vjp
def mm3_nn(a, b):
    return _dot3(a, b, "nn")


mm3_nn.defvjp(lambda a, b: (_dot3(a, b, "nn"), (a, b)), lambda r, g: (_dot3(g, r[1], "nt"), _dot3(r[0], g, "tn")))


def _sigmoid(x):
    return jax.nn.sigmoid(x)


def _silu(x):
    return x * jax.nn.sigmoid(x)


def _softplus(x):
    return jnp.maximum(x, 0.0) + jnp.log(1.0 + jnp.exp(-jnp.abs(x)))


def _log_sigmoid(x):
    return jnp.minimum(x, 0.0) - jnp.log(1.0 + jnp.exp(-jnp.abs(x)))


def _rms(x, g, eps=1e-6):
    return x * lax.rsqrt(jnp.mean(x * x, axis=-1, keepdims=True) + eps) * g


def _heads(fn, x, hd):
    return jnp.concatenate([fn(x[:, h * hd:(h + 1) * hd]) for h in range(x.shape[1] // hd)], axis=1)


def _causal_conv(x, w, halo, tr):
    K = w.shape[0]
    acc = jnp.zeros((tr, x.shape[1]), F32)
    for k in range(K):
        o = halo - (K - 1) + k
        acc = acc + w[k:k + 1] * x[o:o + tr]
    return acc


def _lanes(shape):
    return lax.broadcasted_iota(jnp.int32, shape, len(shape) - 1)


def f_rms(rv, pv):
    return [_rms(rv[0], pv[0])]


def make_f_fox_prep(H, hd):
    def f(rv, pv):
        fq, fk, fv, small = rv
        fb, gq, gk = pv
        qn = _heads(lambda t: _rms(t, gq), fq, hd) * (hd ** -0.5)
        kn = _heads(lambda t: _rms(t, gk), fk, hd)
        logf = jnp.where(_lanes(small.shape) < H, _log_sigmoid(small + fb), 0.0)
        return [qn, kn, fv, logf]
    return f


def make_f_short_conv(tr, act):
    def f(rv, pv):
        y = _causal_conv(rv[0], pv[0], SHORT_HALO, tr)
        return [_silu(y) if act else y]
    return f


def make_f_gdn_prep(H, hd):
    def l2(t):
        return t * lax.rsqrt(jnp.sum(t * t, axis=-1, keepdims=True) + 1e-6)

    def f(rv, pv):
        q, k, small = rv
        alog, dtb = pv
        qn = _heads(l2, q, hd) * (hd ** -0.5)
        kn = _heads(l2, k, hd)
        ln = _lanes(small.shape)
        g = -jnp.exp(alog) * _softplus(small + dtb)
        gates = jnp.where((ln >= H) & (ln < 2 * H), g, jnp.where((ln >= 2 * H) & (ln < 3 * H), _sigmoid(small), 0.0))
        return [qn, kn, gates]
    return f


def _tri(n, strict=False, upper=False):
    ii = lax.broadcasted_iota(jnp.int32, (n, n), 0)
    jj = lax.broadcasted_iota(jnp.int32, (n, n), 1)
    if upper:
        ii, jj = jj, ii
    return ii > jj if strict else ii >= jj


def _decay(gcol, mask):
    C = gcol.shape[0]
    G = jnp.broadcast_to(gcol, (C, C))
    return jnp.where(mask, jnp.exp(jnp.where(mask, G - G.T, 0.0)), 0.0)


def _nilpotent_inverses(Xs):
    C = Xs[0].shape[0]
    eye = (lax.broadcasted_iota(jnp.int32, (C, C), 0) == lax.broadcasted_iota(jnp.int32, (C, C), 1)).astype(F32)
    Ts, Ps = [eye + X for X in Xs], list(Xs)
    for _ in range(int(math.log2(C)) - 1):
        Ps = [mm3_nn(P, P) for P in Ps]
        Ts = [T + mm3_nn(T, P) for T, P in zip(Ts, Ps)]
    return Ts


@jax.custom_vjp
def _saved_inverse(X, T):
    return T


_saved_inverse.defvjp(lambda X, T: (T, T),
                      lambda T, g: (_dot3(T, _dot3(g, T, "nt"), "tn"), jnp.zeros_like(T)))


def make_f_gdn_solve(H, hd, saved):
    C = GDN_CHUNK

    def f(rv, pv):
        k, v, gates = rv[:3]
        gc = _mmh(_tri(C).astype(F32), gates)
        strict = _tri(C, strict=True)
        heads = range(H)
        ks = [k[:, h * hd:(h + 1) * hd] for h in heads]
        gcols = [gc[:, H + h:H + h + 1] for h in heads]
        betas = [gates[:, 2 * H + h:2 * H + h + 1] for h in heads]
        kbs = [kh * b for kh, b in zip(ks, betas)]
        vbs = [v[:, h * hd:(h + 1) * hd] * betas[h] for h in heads]
        Xs = [-(mm_nt(kb, kh) * _decay(g, strict)) for kb, kh, g in zip(kbs, ks, gcols)]
        if saved:
            Ts = [_saved_inverse(X, rv[3][:, h * C:(h + 1) * C]) for h, X in enumerate(Xs)]
        else:
            Ts = _nilpotent_inverses(Xs)
        us = [mm3_nn(T, vb) for T, vb in zip(Ts, vbs)]
        ws = [mm3_nn(T, kb * jnp.exp(g)) for T, kb, g in zip(Ts, kbs, gcols)]
        out = [jnp.concatenate(us, axis=1), jnp.concatenate(ws, axis=1), gc]
        return out if saved else out + [jnp.concatenate(Ts, axis=1)]
    return f


def _gdn_steps(Ss, qs, ks, us, ws, gcols):
    C = qs[0].shape[0]
    causal = _tri(C)
    rows = lax.broadcasted_iota(jnp.int32, gcols[0].shape, 0)
    attn = [mm_nt(q, k) * _decay(g, causal) for q, k, g in zip(qs, ks, gcols)]
    glast = [jnp.sum(jnp.where(rows == C - 1, g, 0.0), axis=0, keepdims=True) for g in gcols]
    v_new = [u - mm_nn(w, S) for u, w, S in zip(us, ws, Ss)]
    o_state = [mm_nn(q * jnp.exp(g), S) for q, g, S in zip(qs, gcols, Ss)]
    o_chunk = [mm_nn(a, vn) for a, vn in zip(attn, v_new)]
    update = [mm_tn(k * jnp.exp(gl - g), vn) for k, gl, g, vn in zip(ks, glast, gcols, v_new)]
    S_new = [S * jnp.exp(gl) + d for S, gl, d in zip(Ss, glast, update)]
    return [a + b for a, b in zip(o_state, o_chunk)], S_new


def make_f_gdn_out(hd):
    def f(rv, pv):
        o, gz = rv
        return [_heads(lambda t: _rms(t, pv[0]), o, hd) * _silu(gz)]
    return f


def make_f_conf(tr):
    def f(rv, pv):
        a, g = rv
        w, b, lg, lb = pv
        y = _causal_conv(a * _sigmoid(g), w, CONF_HALO, tr) + b
        xc = y - jnp.mean(y, axis=-1, keepdims=True)
        y = xc * lax.rsqrt(jnp.mean(xc * xc, axis=-1, keepdims=True) + 1e-5) * lg + lb
        return [_silu(y)]
    return f


def f_merge(rv, pv):
    n = len(rv) // 2
    y = _sigmoid(rv[0]) * rv[n]
    for j in range(1, n):
        y = y + _sigmoid(rv[j]) * rv[n + j]
    return [y]


def make_f_xattn(XH, hd):
    XW = XH * hd

    def f(rv, pv):
        q = rv[0]
        kv, gq, gk = pv
        outs = []
        for h in range(XH):
            qh = _rms(q[:, h * hd:(h + 1) * hd], gq)
            kh = _rms(kv[:, h * hd:(h + 1) * hd], gk)
            s = mm_nt(qh, kh) * (hd ** -0.5)
            e = jnp.exp(s - jnp.max(s, axis=-1, keepdims=True))
            outs.append(mm_nn(e / jnp.sum(e, axis=-1, keepdims=True), kv[:, XW + h * hd:XW + (h + 1) * hd]))
        return [jnp.concatenate(outs, axis=1)]
    return f


def make_f_ffn_act(tr):
    def f(rv, pv):
        a, v = rv
        return [_silu(_causal_conv(a, pv[0], SHORT_HALO, tr) + pv[1]) * v]
    return f


def make_f_dprep(H, hd):
    def f(rv, pv):
        do, o = rv
        ind = (lax.broadcasted_iota(jnp.int32, (H * hd, LANES), 0) // hd
               == lax.broadcasted_iota(jnp.int32, (H * hd, LANES), 1)).astype(F32)
        return [do, _mmh(do * o, ind)]
    return f


def cumsum_rows(x, name, reverse=False):
    S, C = x.shape
    tr = _pick(S, (256, 128, 64))
    nt = S // tr

    def body(x_ref, o_ref, carry):
        @pl.when(pl.program_id(0) == 0)
        def _():
            carry[...] = jnp.zeros_like(carry)

        ii = lax.broadcasted_iota(jnp.int32, (tr, tr), 0)
        jj = lax.broadcasted_iota(jnp.int32, (tr, tr), 1)
        tri = (ii <= jj) if reverse else (ii >= jj)
        y = _mmh(tri.astype(F32), x_ref[...]) + carry[...]
        o_ref[...] = y
        rows = lax.broadcasted_iota(jnp.int32, y.shape, 0)
        carry[...] = jnp.sum(jnp.where(rows == (0 if reverse else tr - 1), y, 0.0), axis=0, keepdims=True)

    spec = pl.BlockSpec((tr, C), (lambda i: (nt - 1 - i, 0)) if reverse else (lambda i: (i, 0)))
    return pl.pallas_call(
        body, name=name, grid=(nt,), in_specs=[spec], out_specs=spec,
        out_shape=jax.ShapeDtypeStruct((S, C), F32), scratch_shapes=[pltpu.VMEM((1, C), F32)],
        compiler_params=_cparams(("arbitrary",)),
    )(x)


def _fox_block(S):
    return _pick(S, (1024, 512, 256, 128))


def fox_fwd(q, k, v, frow, H, hd, name):
    S = q.shape[0]
    bq = _fox_block(S)
    nq = S // bq

    def body(q_ref, k_ref, v_ref, fr_ref, o_ref, lse_ref):
        i = pl.program_id(1)
        qv = q_ref[...]

        def step(j, carry, diag):
            m, l, acc = carry
            cols = pl.ds(pl.multiple_of(j * bq, bq), bq)
            kj, vj = k_ref[cols, :], v_ref[cols, :]
            s = lax.dot_general(qv, kj, _DIMS["nt"], preferred_element_type=F32) - fr_ref[0, :, cols]
            if diag:
                s = jnp.where(_tri(bq), s, -jnp.inf)
            m_new = jnp.maximum(m, jnp.max(s, axis=-1, keepdims=True))
            alpha = jnp.exp(m - m_new)
            p = jnp.exp(s - m_new)
            l = alpha * l + jnp.sum(p, axis=-1, keepdims=True)
            acc = alpha * acc + lax.dot_general(p.astype(BF16), vj, _DIMS["nn"], preferred_element_type=F32)
            return m_new, l, acc

        init = (jnp.full((bq, 1), -jnp.inf, F32), jnp.zeros((bq, 1), F32), jnp.zeros((bq, hd), F32))
        carry = lax.fori_loop(0, i, lambda j, c: step(j, c, False), init)
        m, l, acc = step(i, carry, True)
        o_ref[...] = acc / l
        lse_ref[0] = m + jnp.log(l)

    return pl.pallas_call(
        body, name=name, grid=(H, nq),
        in_specs=[pl.BlockSpec((bq, hd), lambda h, i: (i, h)),
                  pl.BlockSpec((S, hd), lambda h, i: (0, h)), pl.BlockSpec((S, hd), lambda h, i: (0, h)),
                  pl.BlockSpec((1, 1, S), lambda h, i: (h, 0, 0))],
        out_specs=[pl.BlockSpec((bq, hd), lambda h, i: (i, h)), pl.BlockSpec((1, bq, 1), lambda h, i: (h, i, 0))],
        out_shape=[jax.ShapeDtypeStruct((S, H * hd), F32), jax.ShapeDtypeStruct((H, S, 1), F32)],
        compiler_params=_cparams(("parallel", "parallel")),
    )(q, k, v, frow)


def fox_bwd(q, k, v, do, fcol, lse_row, delta_row, H, hd, name):
    S = q.shape[0]
    bk = _fox_block(S)
    nk = S // bk

    def body(q_ref, do_ref, k_ref, v_ref, fc_ref, lse_ref, dl_ref, dq_ref, dfq_ref, dk_ref, dv_ref, df_ref):
        j = pl.program_id(1)

        @pl.when(j == 0)
        def _():
            dq_ref[...] = jnp.zeros_like(dq_ref)
            dfq_ref[...] = jnp.zeros_like(dfq_ref)

        kj, vj = k_ref[...], v_ref[...]
        fk = fc_ref[0]

        def step(i, carry, diag):
            dk, dv, df = carry
            rows = pl.ds(pl.multiple_of(i * bk, bk), bk)
            qi, doi = q_ref[rows, :], do_ref[rows, :]
            st = lax.dot_general(kj, qi, _DIMS["nt"], preferred_element_type=F32) - fk - lse_ref[0, :, rows]
            if diag:
                st = jnp.where(_tri(bk, upper=True), st, -jnp.inf)
            pt = jnp.exp(st)
            dv = dv + lax.dot_general(pt.astype(BF16), doi, _DIMS["nn"], preferred_element_type=F32)
            dpt = lax.dot_general(vj, doi, _DIMS["nt"], preferred_element_type=F32)
            dst = pt * (dpt - dl_ref[0, :, rows])
            df = df - jnp.sum(dst, axis=-1, keepdims=True)
            dfq_ref[0, :, rows] += jnp.sum(dst, axis=0, keepdims=True)
            dsb = dst.astype(BF16)
            dk = dk + lax.dot_general(dsb, qi, _DIMS["nn"], preferred_element_type=F32)
            dq_ref[rows, :] += lax.dot_general(dsb, kj, _DIMS["tn"], preferred_element_type=F32)
            return dk, dv, df

        init = (jnp.zeros((bk, hd), F32), jnp.zeros((bk, hd), F32), jnp.zeros((bk, 1), F32))
        carry = step(j, init, True)
        dk, dv, df = lax.fori_loop(j + 1, nk, lambda i, c: step(i, c, False), carry)
        dk_ref[...] = dk
        dv_ref[...] = dv
        df_ref[0] = df

    whole = pl.BlockSpec((S, hd), lambda h, j: (0, h))
    blk = pl.BlockSpec((bk, hd), lambda h, j: (j, h))
    row = pl.BlockSpec((1, 1, S), lambda h, j: (h, 0, 0))
    col = pl.BlockSpec((1, bk, 1), lambda h, j: (h, j, 0))
    return pl.pallas_call(
        body, name=name, grid=(H, nk),
        in_specs=[whole, whole, blk, blk, col, row, row],
        out_specs=[whole, row, blk, blk, col],
        out_shape=[jax.ShapeDtypeStruct((S, H * hd), F32), jax.ShapeDtypeStruct((H, 1, S), F32)]
        + [jax.ShapeDtypeStruct((S, H * hd), F32)] * 2 + [jax.ShapeDtypeStruct((H, S, 1), F32)],
        compiler_params=_cparams(("parallel", "arbitrary")),
    )(q, do, k, v, fcol, lse_row, delta_row)


def gdn_scan_fwd(q, k, u, w, gc, H, hd, name):
    S = q.shape[0]
    C = GDN_CHUNK
    NC = S // C

    def body(q_ref, k_ref, u_ref, w_ref, gc_ref, o_ref, sin_ref, state):
        @pl.when(pl.program_id(0) == 0)
        def _():
            state[...] = jnp.zeros_like(state)

        gcv = gc_ref[...]
        sl = [slice(h * hd, (h + 1) * hd) for h in range(H)]
        Ss = [state[h] for h in range(H)]
        for h in range(H):
            sin_ref[0, h] = Ss[h]
        outs, S_new = _gdn_steps(Ss, [q_ref[:, s] for s in sl], [k_ref[:, s] for s in sl], [u_ref[:, s] for s in sl],
                                 [w_ref[:, s] for s in sl], [gcv[:, H + h:H + h + 1] for h in range(H)])
        for h in range(H):
            state[h] = S_new[h]
        o_ref[...] = jnp.concatenate(outs, axis=1)

    wide = pl.BlockSpec((C, H * hd), lambda i: (i, 0))
    return pl.pallas_call(
        body, name=name, grid=(NC,),
        in_specs=[wide] * 4 + [pl.BlockSpec((C, LANES), lambda i: (i, 0))],
        out_specs=[wide, pl.BlockSpec((1, H, hd, hd), lambda i: (i, 0, 0, 0))],
        out_shape=[jax.ShapeDtypeStruct((S, H * hd), F32), jax.ShapeDtypeStruct((NC, H, hd, hd), F32)],
        scratch_shapes=[pltpu.VMEM((H, hd, hd), F32)],
        compiler_params=_cparams(("arbitrary",)),
    )(q, k, u, w, gc)


def gdn_scan_bwd(q, k, u, w, gc, sin, do, H, hd, name):
    S = q.shape[0]
    C = GDN_CHUNK
    NC = S // C

    def body(q_ref, k_ref, u_ref, w_ref, gc_ref, sin_ref, do_ref, dq_ref, dk_ref, du_ref, dw_ref, dgc_ref, dstate):
        @pl.when(pl.program_id(0) == 0)
        def _():
            dstate[...] = jnp.zeros_like(dstate)

        gcv = gc_ref[...]
        ln = _lanes(gcv.shape)
        dgc = jnp.zeros_like(gcv)
        sl = [slice(h * hd, (h + 1) * hd) for h in range(H)]
        _, vjp = jax.vjp(_gdn_steps, [sin_ref[0, h] for h in range(H)], [q_ref[:, s] for s in sl],
                         [k_ref[:, s] for s in sl], [u_ref[:, s] for s in sl], [w_ref[:, s] for s in sl],
                         [gcv[:, H + h:H + h + 1] for h in range(H)])
        dS, dq, dk, du, dw, dg = vjp(([do_ref[:, s] for s in sl], [dstate[h] for h in range(H)]))
        for h in range(H):
            dstate[h] = dS[h]
            dgc = dgc + jnp.where(ln == H + h, dg[h], 0.0)
        for ref, lst in zip((dq_ref, dk_ref, du_ref, dw_ref), (dq, dk, du, dw)):
            ref[...] = jnp.concatenate(lst, axis=1)
        dgc_ref[...] = dgc

    wide = pl.BlockSpec((C, H * hd), lambda i: (NC - 1 - i, 0))
    narrow = pl.BlockSpec((C, LANES), lambda i: (NC - 1 - i, 0))
    return pl.pallas_call(
        body, name=name, grid=(NC,),
        in_specs=[wide] * 4 + [narrow, pl.BlockSpec((1, H, hd, hd), lambda i: (NC - 1 - i, 0, 0, 0)), wide],
        out_specs=[wide] * 4 + [narrow],
        out_shape=[jax.ShapeDtypeStruct((S, H * hd), F32)] * 4 + [jax.ShapeDtypeStruct((S, LANES), F32)],
        scratch_shapes=[pltpu.VMEM((H, hd, hd), F32)],
        compiler_params=_cparams(("arbitrary",)),
    )(q, k, u, w, gc, sin, do)


def loss_head(y, t, name):
    S, D = y.shape
    tr = _pick(S, (256, 128, 64))

    def body(y_ref, t_ref, dy_ref, acc_ref):
        @pl.when(pl.program_id(0) == 0)
        def _():
            acc_ref[...] = jnp.zeros_like(acc_ref)

        err = y_ref[...] - t_ref[...]
        dy_ref[...] = err / D
        acc_ref[...] += jnp.sum(jnp.mean(err * err, axis=-1, keepdims=True), axis=0, keepdims=True)

    blk = pl.BlockSpec((tr, D), lambda i: (i, 0))
    dy, acc = pl.pallas_call(
        body, name=name, grid=(S // tr,), in_specs=[blk, blk],
        out_specs=[blk, pl.BlockSpec((8, LANES), lambda i: (0, 0))],
        out_shape=[jax.ShapeDtypeStruct((S, D), F32), jax.ShapeDtypeStruct((8, LANES), F32)],
        compiler_params=_cparams(("arbitrary",)),
    )(y, t)
    return 0.5 * acc[0, 0], dy


class Dims(NamedTuple):
    D: int
    H: int
    hd: int
    MW: int
    XH: int
    FF: int

    @property
    def n_in(self):
        return 9 * self.MW + 3 * self.H + 3 * self.D

    @property
    def n_in_padded(self):
        return 3 * self.D + 9 * self.MW + LANES


def _in_pieces(dm):
    MW, H, D = dm.MW, dm.H, dm.D
    fq, fk, fv, ff = 0, MW, 2 * MW, 3 * MW
    gq = ff + H
    gk, gv = gq + MW, gq + 2 * MW
    ga = gv + MW
    gb, gz = ga + H, ga + 2 * H
    cu = gz + MW
    gl = cu + 2 * MW
    return [(gl, 3 * D), (fq, MW), (fk, MW), (fv, MW), (gq, MW), (gk, MW), (gv, MW), (gz, MW), (cu, 2 * MW),
            (ff, H), (ga, H), (gb, H)]


def permute_in_blocks(blocks, dm):
    n = blocks[0].shape[-1]
    parts = []
    for s, size in _in_pieces(dm):
        e = s + size
        while s < e:
            d = s // n
            hi = min(e, (d + 1) * n)
            parts.append(blocks[d][:, s - d * n:hi - d * n])
            s = hi
    parts.append(jnp.zeros((blocks[0].shape[0], LANES - 3 * dm.H), blocks[0].dtype))
    return jnp.concatenate(parts, axis=1)


def unpermute_to_blocks(wp, dm):
    segs, off = [], 0
    for s, size in _in_pieces(dm):
        segs.append((s, size, off))
        off += size
    segs.sort()
    n = dm.n_in // N_DEV
    chunks = []
    for d in range(N_DEV):
        lo, hi = d * n, (d + 1) * n
        parts = [wp[:, off + max(s, lo) - s:off + min(s + size, hi) - s] for s, size, off in segs
                 if max(s, lo) < min(s + size, hi)]
        chunks.append(jnp.concatenate(parts, axis=1))
    return jnp.stack(chunks, axis=0)


def _pad_lanes(v, at):
    return jnp.pad(v, (at, LANES - at - v.shape[0]))[None]


def _ops(x, m, P, dm, t):
    S = x.shape[0]
    tr = _pick(S, (256, 128, 64))
    return dict(tr=tr, trm=_pick(S, (128, 64)), trx=_pick(S, (512, 256, 128, 64)),
                cbq=3 * dm.D // dm.MW, cbs=(3 * dm.D + 9 * dm.MW) // LANES,
                cwf=_pick(dm.FF, (512, 256, 128)))


def layer_fwd(x, m, P, dm, t):
    D, H, hd, MW, XH, FF = dm
    XW = XH * hd
    c = _ops(x, m, P, dm, t)
    tr, cbq, cbs = c["tr"], c["cbq"], c["cbs"]
    (h,) = tile_fwd(t + "mix_norm", f_rms, [Row(x, 0, D)], [P["mix_norm_g"][None]], [(D, BF16)], tr)
    p = matmul(h, P.big("w_in", h), "nn", t + "in_proj", deps=P.deps())
    fox_rows = [Row(p, cbq, MW), Row(p, cbq + 1, MW), Row(p, cbq + 2, MW), Row(p, cbs, LANES)]
    fox_params = [_pad_lanes(P["fox_fb"], 0), P["fox_q_norm_g"][None], P["fox_k_norm_g"][None]]
    qn, kn, vb, logf = tile_fwd(t + "fox_prep", make_f_fox_prep(H, hd), fox_rows, fox_params,
                                [(MW, BF16)] * 3 + [(LANES, F32)], tr)
    Ft = cumsum_rows(logf, t + "fox_cumsum")[:, :H].T
    fcol, frow = Ft[:, :, None], Ft[:, None, :]
    out_a, lse = fox_fwd(qn, kn, vb, frow, H, hd, t + "fox_attn")
    (qkv,) = tile_fwd(t + "gdn_conv", make_f_short_conv(tr, True), [Row(p, cbq + 3, MW, SHORT_HALO)],
                      [P["gdn_conv_w"]], [(MW, F32)], tr, ncb=3)
    gp_rows = [Row(qkv, 0, MW), Row(qkv, 1, MW), Row(p, cbs, LANES)]
    gp_params = [_pad_lanes(P["gdn_a_log"], H), _pad_lanes(P["gdn_dt_bias"], H)]
    gq, gk, gates = tile_fwd(t + "gdn_prep", make_f_gdn_prep(H, hd), gp_rows, gp_params,
                             [(MW, F32), (MW, F32), (LANES, F32)], tr)
    u, w, gc, tinv = tile_fwd(t + "gdn_solve", make_f_gdn_solve(H, hd, False),
                              [Row(gk, 0, MW), Row(qkv, 2, MW), Row(gates, 0, LANES)], [],
                              [(MW, F32), (MW, F32), (LANES, F32), (H * GDN_CHUNK, F32)], GDN_CHUNK)
    o_g, sin = gdn_scan_fwd(gq, gk, u, w, gc, H, hd, t + "gdn_scan")
    (out_b,) = tile_fwd(t + "gdn_out", make_f_gdn_out(hd), [Row(o_g, 0, MW), Row(p, cbq + 6, MW)],
                        [P["gdn_out_norm_g"][None]], [(MW, F32)], tr)
    conf_params = [P["conf_dw_w"], P["conf_dw_b"][None], P["conf_ln_g"][None], P["conf_ln_b"][None]]
    (out_c,) = tile_fwd(t + "conf", make_f_conf(tr), [Row(p, cbq + 7, MW, CONF_HALO), Row(p, cbq + 8, MW, CONF_HALO)],
                        conf_params, [(MW, F32)], tr)
    branches = [out_a, out_b, out_c]
    proj = [matmul(b, P.big("w_branch", out_a)[n], "nn", t + f"branch{n}") for n, b in enumerate(branches)]
    (y,) = tile_fwd(t + "merge", f_merge, [Row(p, n, D) for n in range(3)] + [Row(pr, 0, D) for pr in proj], [],
                    [(D, BF16)], c["trm"])
    x1 = matmul(y, P.big("w_out", y), "nn", t + "out_proj", add=x)
    (h2,) = tile_fwd(t + "xa_norm", f_rms, [Row(x1, 0, D)], [P["xattn_norm_g"][None]], [(D, BF16)], tr)
    q = matmul(h2, P.big("xattn_wq", h2), "nn", t + "xa_q")
    kv = matmul(m, P.big("xattn_wkv", h2), "nn", t + "xa_kv")
    xa_params = [kv, P["xattn_q_norm_g"][None], P["xattn_k_norm_g"][None]]
    (o_x,) = tile_fwd(t + "xa_attn", make_f_xattn(XH, hd), [Row(q, 0, XW)], xa_params, [(XW, F32)], c["trx"])
    x2 = matmul(o_x, P.big("xattn_wo", o_x), "nn", t + "xa_o", add=x1)
    (h3,) = tile_fwd(t + "ffn_norm", f_rms, [Row(x2, 0, D)], [P["ffn_norm_g"][None]], [(D, BF16)], tr)
    av = matmul(h3, P.big("ffn_w_up", h3), "nn", t + "ffn_up")
    cwf = c["cwf"]
    (uf,) = tile_fwd(t + "ffn_act", make_f_ffn_act(tr), [Row(av, 0, cwf, SHORT_HALO), Row(av, FF // cwf, cwf)],
                     [P["ffn_conv_w"], P["ffn_conv_b"][None]], [(cwf, BF16)], tr, ncb=FF // cwf)
    x3 = matmul(uf, P.big("ffn_w_down", uf), "nn", t + "ffn_down", add=x2)
    res = dict(x=x, h=h, p=p, qn=qn, kn=kn, vb=vb, fcol=fcol, frow=frow, out_a=out_a, lse=lse, qkv=qkv, gq=gq, gk=gk,
               gates=gates, u=u, w=w, gc=gc, tinv=tinv, sin=sin, o_g=o_g, out_b=out_b, out_c=out_c, proj=proj, y=y, x1=x1, h2=h2,
               q=q, kv=kv, o_x=o_x, x2=x2, h3=h3, av=av, uf=uf)
    return x3, res


def layer_bwd(dx3, m, P, R, dm, t):
    D, H, hd, MW, XH, FF = dm
    XW = XH * hd
    c = _ops(R["x"], m, P, dm, t)
    tr, cbq, cbs = c["tr"], c["cbq"], c["cbs"]
    p = R["p"]
    G = {}
    P.emit("ffn_w_down", matmul(R["uf"], dx3, "tn", t + "ffn_down_dw", out_dtype=BF16))
    du = matmul(dx3, P.big("ffn_w_down", None), "nt", t + "ffn_down_dx", deps=P.deps())
    cwf = c["cwf"]
    (da, dv), (G["ffn_conv_w"], dcb) = tile_bwd(
        t + "ffn_act_b", make_f_ffn_act(tr), [Row(R["av"], 0, cwf, SHORT_HALO), Row(R["av"], FF // cwf, cwf)],
        [P["ffn_conv_w"], P["ffn_conv_b"][None]], [du], tr, ncb=FF // cwf)
    G["ffn_conv_b"] = dcb[0]
    dav = jnp.concatenate([da, dv], axis=1).astype(BF16)
    P.emit("ffn_w_up", matmul(R["h3"], dav, "tn", t + "ffn_up_dw", out_dtype=BF16))
    dh3 = matmul(dav, P.big("ffn_w_up", None), "nt", t + "ffn_up_dx", deps=P.deps())
    (dx2,), (dg,) = tile_bwd(t + "ffn_norm_b", f_rms, [Row(R["x2"], 0, D)], [P["ffn_norm_g"][None]], [dh3], tr, adds=[dx3])
    G["ffn_norm_g"] = dg[0]
    P.emit("xattn_wo", matmul(R["o_x"], dx2, "tn", t + "xa_o_dw", out_dtype=BF16))
    do_x = matmul(dx2, P.big("xattn_wo", None), "nt", t + "xa_o_dx", deps=P.deps())
    xa_params = [R["kv"], P["xattn_q_norm_g"][None], P["xattn_k_norm_g"][None]]
    (dq,), (dkv, dgq, dgk) = tile_bwd(t + "xa_attn_b", make_f_xattn(XH, hd), [Row(R["q"], 0, XW)], xa_params, [do_x], c["trx"])
    G["xattn_q_norm_g"], G["xattn_k_norm_g"] = dgq[0], dgk[0]
    P.emit("xattn_wq", matmul(R["h2"], dq, "tn", t + "xa_q_dw", out_dtype=BF16))
    P.emit("xattn_wkv", matmul(m, dkv, "tn", t + "xa_kv_dw", out_dtype=BF16))
    dh2 = matmul(dq, P.big("xattn_wq", None), "nt", t + "xa_q_dx", deps=P.deps())
    dm_l = matmul(dkv, P.big("xattn_wkv", None), "nt", t + "xa_kv_dx")
    (dx1,), (dg,) = tile_bwd(t + "xa_norm_b", f_rms, [Row(R["x1"], 0, D)], [P["xattn_norm_g"][None]], [dh2], tr, adds=[dx2])
    G["xattn_norm_g"] = dg[0]
    P.emit("w_out", matmul(R["y"], dx1, "tn", t + "out_proj_dw", out_dtype=BF16))
    dy = matmul(dx1, P.big("w_out", None), "nt", t + "out_proj_dx", deps=P.deps())
    merge_rows = [Row(p, n, D) for n in range(3)] + [Row(pr, 0, D) for pr in R["proj"]]
    dmerge, _ = tile_bwd(t + "merge_b", f_merge, merge_rows, [], [dy], c["trm"])
    dgl, dpr = dmerge[:3], dmerge[3:]
    branches = [R["out_a"], R["out_b"], R["out_c"]]
    P.emit("w_branch", jnp.stack([matmul(branches[n], dpr[n], "tn", t + f"branch{n}_dw", out_dtype=BF16)
                                  for n in range(3)]))
    dbr = [matmul(dpr[n], P.big("w_branch", None)[n], "nt", t + f"branch{n}_dx", deps=P.deps()) for n in range(3)]
    conf_params = [P["conf_dw_w"], P["conf_dw_b"][None], P["conf_ln_g"][None], P["conf_ln_b"][None]]
    (dcu_a, dcu_g), (G["conf_dw_w"], db, dlg, dlb) = tile_bwd(
        t + "conf_b", make_f_conf(tr), [Row(p, cbq + 7, MW, CONF_HALO), Row(p, cbq + 8, MW, CONF_HALO)],
        conf_params, [dbr[2]], tr)
    G["conf_dw_b"], G["conf_ln_g"], G["conf_ln_b"] = db[0], dlg[0], dlb[0]
    (do_g, dgz), (dg,) = tile_bwd(t + "gdn_out_b", make_f_gdn_out(hd), [Row(R["o_g"], 0, MW), Row(p, cbq + 6, MW)],
                                  [P["gdn_out_norm_g"][None]], [dbr[1]], tr)
    G["gdn_out_norm_g"] = dg[0]
    dgq, dgk2, du_, dw_, dgc = gdn_scan_bwd(R["gq"], R["gk"], R["u"], R["w"], R["gc"], R["sin"], do_g, H, hd, t + "gdn_scan_b")
    (dgk1, dgv, dgates, _), _ = tile_bwd(
        t + "gdn_solve_b", make_f_gdn_solve(H, hd, True),
        [Row(R["gk"], 0, MW), Row(R["qkv"], 2, MW), Row(R["gates"], 0, LANES), Row(R["tinv"], 0, H * GDN_CHUNK)],
        [], [du_, dw_, dgc], GDN_CHUNK)
    gp_rows = [Row(R["qkv"], 0, MW), Row(R["qkv"], 1, MW), Row(p, cbs, LANES)]
    gp_params = [_pad_lanes(P["gdn_a_log"], H), _pad_lanes(P["gdn_dt_bias"], H)]
    (dqa, dka, dsmall_g), (dal, ddt) = tile_bwd(t + "gdn_prep_b", make_f_gdn_prep(H, hd), gp_rows, gp_params,
                                               [dgq, dgk1 + dgk2, dgates], tr)
    G["gdn_a_log"], G["gdn_dt_bias"] = dal[0, H:2 * H], ddt[0, H:2 * H]
    (dgqkv,), (G["gdn_conv_w"],) = tile_bwd(
        t + "gdn_conv_b", make_f_short_conv(tr, True), [Row(p, cbq + 3, MW, SHORT_HALO)], [P["gdn_conv_w"]],
        [jnp.concatenate([dqa, dka, dgv], axis=1)], tr, ncb=3)
    do_b, delta = tile_fwd(t + "fox_dprep", make_f_dprep(H, hd), [Row(dbr[0], 0, MW), Row(R["out_a"], 0, MW)], [],
                           [(MW, BF16), (LANES, F32)], tr)
    S = p.shape[0]
    delta_row = delta[:, :H].T[:, None, :]
    lse_row = R["lse"].reshape(H, 1, S)
    dqn, dfq_, dkn, dvf, dfk = fox_bwd(R["qn"], R["kn"], R["vb"], do_b, R["fcol"], lse_row, delta_row, H, hd,
                                       t + "fox_attn_b")
    dF = jnp.pad((dfk.reshape(H, S) + dfq_.reshape(H, S)).T, ((0, 0), (0, LANES - H)))
    dlogf = cumsum_rows(dF, t + "fox_cumsum_b", reverse=True)
    fox_rows = [Row(p, cbq, MW), Row(p, cbq + 1, MW), Row(p, cbq + 2, MW), Row(p, cbs, LANES)]
    fox_params = [_pad_lanes(P["fox_fb"], 0), P["fox_q_norm_g"][None], P["fox_k_norm_g"][None]]
    (dfq, dfk_, dfv, dsmall_f), (dfb, dgq_, dgk_) = tile_bwd(t + "fox_prep_b", make_f_fox_prep(H, hd), fox_rows, fox_params,
                                                          [dqn, dkn, dvf, dlogf], tr)
    G["fox_fb"], G["fox_q_norm_g"], G["fox_k_norm_g"] = dfb[0, :H], dgq_[0], dgk_[0]
    dp = jnp.concatenate(dgl + [dfq, dfk_, dfv, dgqkv, dgz, dcu_a, dcu_g, dsmall_f + dsmall_g], axis=1).astype(BF16)
    P.emit("w_in", matmul(R["h"], dp, "tn", t + "in_proj_dw", out_dtype=BF16))
    dh = matmul(dp, P.big("w_in", None), "nt", t + "in_proj_dx", deps=P.deps())
    (dx,), (dg,) = tile_bwd(t + "mix_norm_b", f_rms, [Row(R["x"], 0, D)], [P["mix_norm_g"][None]], [dh], tr, adds=[dx1])
    G["mix_norm_g"] = dg[0]
    return dx, dm_l, G


def local_step(x, mem, target, layers, mem_norm_g, dm):
    trm = _pick(mem.shape[0], (256, 128, 64, 32, 16, 8))
    (m,) = tile_fwd("mem_norm", f_rms, [Row(mem, 0, dm.D)], [mem_norm_g[None]], [(dm.D, F32)], trm)
    res = []
    for l, P in enumerate(layers):
        x, R = layer_fwd(x, m, P, dm, f"l{l}_")
        res.append(R)
    loss, dx = loss_head(x, target, "loss_head")
    grads, dm_sum = [None] * len(layers), None
    for l in reversed(range(len(layers))):
        dx, dm_l, grads[l] = layer_bwd(dx, m, layers[l], res[l], dm, f"l{l}_")
        dm_sum = dm_l if dm_sum is None else dm_sum + dm_l
    _, (dg,) = tile_bwd("mem_norm_b", f_rms, [Row(mem, 0, dm.D)], [mem_norm_g[None]], [dm_sum], trm)
    return loss, dx, grads, dg[0]


ARG_NAMES = ["x", "mem", "mix_norm_g", "w_in", "fox_fb", "fox_q_norm_g", "fox_k_norm_g", "gdn_conv_w", "gdn_a_log",
             "gdn_dt_bias", "gdn_out_norm_g", "conf_dw_w", "conf_dw_b", "conf_ln_g", "conf_ln_b", "w_branch", "w_out",
             "mem_norm_g", "xattn_norm_g", "xattn_wq", "xattn_wkv", "xattn_q_norm_g", "xattn_k_norm_g", "xattn_wo",
             "ffn_norm_g", "ffn_w_up", "ffn_conv_w", "ffn_conv_b", "ffn_w_down"]
WEIGHTS = ARG_NAMES[2:]
COL_SHARDED = ["w_in", "gdn_conv_w", "conf_dw_w", "w_branch", "xattn_wo", "ffn_w_up", "ffn_conv_w"]
ROW_SHARDED = ["w_out", "xattn_wq", "xattn_wkv", "ffn_w_down"]
MATMUL_WEIGHTS = ["w_in", "w_branch", "w_out", "xattn_wq", "xattn_wkv", "xattn_wo", "ffn_w_up", "ffn_w_down"]
REPLICATED = [n for n in WEIGHTS if n not in COL_SHARDED + ROW_SHARDED]
SMALL_SHARDED = [n for n in COL_SHARDED + ROW_SHARDED if n not in MATMUL_WEIGHTS]


class LayerWeights:
    def __init__(self, small, fetch, emit, deps):
        self.small, self._fetch, self.emit, self.deps, self._cache = small, fetch, emit, deps, {}

    def __getitem__(self, name):
        return self.small[name]

    def preset(self, name, value):
        self._cache[name] = value

    def big(self, name, after):
        if name not in self._cache:
            self._cache[name] = self._fetch(name, after)
        return self._cache[name]


def regroup(x, blocked_in, name, own=None):
    if blocked_in:
        _, R, n = x.shape
    else:
        R, n = x.shape[0], x.shape[1] // N_DEV
    tr = _pick(R, (512, 256, 128, 64, 32, 16))
    blocked = pl.BlockSpec((None, tr, n), lambda d, i: (d, i, 0))
    flat = pl.BlockSpec((tr, n), lambda d, i: (i, d))

    def body(x_ref, *rest):
        o_ref = rest[-1]
        if own is None:
            o_ref[...] = x_ref[...]
        else:
            o_ref[...] = jnp.where(pl.program_id(0) == _me(), rest[0][...], x_ref[...])

    return pl.pallas_call(
        body, name=name, grid=(N_DEV, R // tr),
        in_specs=[blocked if blocked_in else flat] + ([pl.BlockSpec((tr, n), lambda d, i: (i, 0))] if own is not None else []),
        out_specs=flat if blocked_in else blocked,
        out_shape=jax.ShapeDtypeStruct((R, N_DEV * n) if blocked_in else (N_DEV, R, n), x.dtype),
        compiler_params=_cparams(("parallel", "parallel")),
    )(x, *([own] if own is not None else []))


def _assemble(name, land, own, dm, tag):
    me = _me()
    if name == "w_in":
        return permute_in_blocks([jnp.where(me == d, own, land[d]) for d in range(N_DEV)], dm)
    if name in COL_SHARDED:
        n = land.shape[-1]
        whole = regroup(land.reshape(N_DEV, -1, n), True, tag + "_regroup", own=own.reshape(-1, n))
        return whole.reshape(land.shape[1:-1] + (N_DEV * n,))
    slot = lax.broadcasted_iota(jnp.int32, (N_DEV,) + (1,) * own.ndim, 0)
    return jnp.where(slot == me, own[None], land).reshape((land.shape[0] * land.shape[1],) + land.shape[2:])


def _split(name, g, dm, tag):
    if name == "w_in":
        return unpermute_to_blocks(g, dm)
    if name in COL_SHARDED:
        n = g.shape[-1] // N_DEV
        return regroup(g.reshape(-1, g.shape[-1]), False, tag + "_regroup").reshape((N_DEV,) + g.shape[:-1] + (n,))
    return g.reshape((N_DEV, g.shape[0] // N_DEV) + g.shape[1:])


def _gather_small_weights(shards):
    names = list(shards)
    like = [shards[n] for n in names]
    g = all_gather(_pack(like), "gather_small_weights")
    per_dev = [_unpack(g[d], like) for d in range(N_DEV)]
    return {n: jnp.concatenate([per_dev[d][i] for d in range(N_DEV)], axis=-1) for i, n in enumerate(names)}


def _scatter_grad(name, g):
    if name in COL_SHARDED:
        g = g.reshape(g.shape[:-1] + (N_DEV, g.shape[-1] // N_DEV))
        return jnp.moveaxis(g, -2, 0)
    g = g.reshape(g.shape[:1] + (N_DEV, g.shape[1] // N_DEV) + g.shape[2:])
    return jnp.moveaxis(g, 1, 0)


def _pack(arrs):
    flat = jnp.concatenate([a.reshape(-1) for a in arrs])
    rows = -(-flat.shape[0] // (8 * LANES)) * 8
    return jnp.pad(flat, (0, rows * LANES - flat.shape[0])).reshape(rows, LANES)


def _unpack(packed, like):
    flat, out, o = packed.reshape(-1), [], 0
    for a in like:
        out.append(flat[o:o + a.size].reshape(a.shape))
        o += a.size
    return out


def kernel(x, mem, mix_norm_g, w_in, fox_fb, fox_q_norm_g, fox_k_norm_g, gdn_conv_w, gdn_a_log, gdn_dt_bias, gdn_out_norm_g, conf_dw_w, conf_dw_b, conf_ln_g, conf_ln_b, w_branch, w_out, mem_norm_g, xattn_norm_g, xattn_wq, xattn_wkv, xattn_q_norm_g, xattn_k_norm_g, xattn_wo, ffn_norm_g, ffn_w_up, ffn_conv_w, ffn_conv_b, ffn_w_down, loss_target, m_mix_norm_g, m_w_in, m_fox_fb, m_fox_q_norm_g, m_fox_k_norm_g, m_gdn_conv_w, m_gdn_a_log, m_gdn_dt_bias, m_gdn_out_norm_g, m_conf_dw_w, m_conf_dw_b, m_conf_ln_g, m_conf_ln_b, m_w_branch, m_w_out, m_mem_norm_g, m_xattn_norm_g, m_xattn_wq, m_xattn_wkv, m_xattn_q_norm_g, m_xattn_k_norm_g, m_xattn_wo, m_ffn_norm_g, m_ffn_w_up, m_ffn_conv_w, m_ffn_conv_b, m_ffn_w_down, v_mix_norm_g, v_w_in, v_fox_fb, v_fox_q_norm_g, v_fox_k_norm_g, v_gdn_conv_w, v_gdn_a_log, v_gdn_dt_bias, v_gdn_out_norm_g, v_conf_dw_w, v_conf_dw_b, v_conf_ln_g, v_conf_ln_b, v_w_branch, v_w_out, v_mem_norm_g, v_xattn_norm_g, v_xattn_wq, v_xattn_wkv, v_xattn_q_norm_g, v_xattn_k_norm_g, v_xattn_wo, v_ffn_norm_g, v_ffn_w_up, v_ffn_conv_w, v_ffn_conv_b, v_ffn_w_down):
    args = locals()
    W = {n: args[n] for n in WEIGHTS}
    Mo = {n: args["m_" + n] for n in WEIGHTS}
    Vo = {n: args["v_" + n] for n in WEIGHTS}
    L = mix_norm_g.shape[0]
    H, hd = fox_fb.shape[1], fox_q_norm_g.shape[1]
    dm = Dims(D=x.shape[-1], H=H, hd=hd, MW=H * hd, XH=xattn_wq.shape[-1] // hd, FF=ffn_conv_b.shape[-1])

    gathers, sent, tokens = {}, {}, []

    def take_tokens():
        got = list(tokens)
        tokens.clear()
        return got

    def start_gather(l, n, after):
        blk = W[n][l].astype(BF16)
        handle, token = exchange_start(blk, f"gather{l}_{n}_start", True, after)
        gathers[(l, n)] = (handle, blk)
        tokens.append(token)
        return token

    def fetch(l, n, after):
        handle, blk = gathers.pop((l, n))
        land = exchange_wait(handle, f"gather{l}_{n}_wait", True, blk if after is None else after)
        return _assemble(n, land, blk, dm, f"gather{l}_{n}")

    def emit(l, n, g):
        send = _split(n, g, dm, f"grad{l}_{n}")
        handle, token = exchange_start(send, f"grad{l}_{n}_start", False)
        sent[(l, n)] = (handle, send)
        tokens.append(token)

    small_full = _gather_small_weights({n: W[n] for n in SMALL_SHARDED})
    layers, follow = [], small_full[SMALL_SHARDED[0]]
    for l in range(L):
        small = {n: (small_full[n][l] if n in SMALL_SHARDED else W[n][l]) for n in WEIGHTS
                 if n not in MATMUL_WEIGHTS and n != "mem_norm_g"}
        layers.append(LayerWeights(small, functools.partial(fetch, l), functools.partial(emit, l), take_tokens))
        for n in MATMUL_WEIGHTS:
            follow = start_gather(l, n, follow)

    loss, dx, grads, d_mem_g = local_step(x[0], mem[0], loss_target[0], layers, mem_norm_g, dm)
    loss = lax.psum(loss, ("x", "y", "c"))

    def whole(n):
        return d_mem_g if n == "mem_norm_g" else jnp.stack([g[n] for g in grads])

    out, after = {}, dx
    for l in reversed(range(L)):
        for n in reversed(MATMUL_WEIGHTS):
            handle, send = sent.pop((l, n))
            land = exchange_wait(handle, f"grad{l}_{n}_wait", False, after)
            out[n] = adamw(land, W[n], Mo[n], Vo[n], f"adamw{l}_{n}", layer=l, into=out.get(n), sent=send)
            after = out[n][0]
    for n in SMALL_SHARDED:
        parts = all_to_all(_scatter_grad(n, whole(n)), "exchange_" + n)
        out[n] = adamw(parts, W[n], Mo[n], Vo[n], "adamw_" + n)
    rep = [whole(n) for n in REPLICATED]
    parts = all_gather(_pack(rep), "gather_small_grads")
    packed = adamw(parts, _pack([W[n] for n in REPLICATED]), _pack([Mo[n] for n in REPLICATED]),
                   _pack([Vo[n] for n in REPLICATED]), "adamw_small")
    unpacked = [_unpack(pk, rep) for pk in packed]
    for i, n in enumerate(REPLICATED):
        out[n] = tuple(u[i] for u in unpacked)

    return (loss, dx[None], *[out[n][0] for n in WEIGHTS], *[out[n][1] for n in WEIGHTS],
            *[out[n][2] for n in WEIGHTS], *[out[n][3] for n in WEIGHTS])
```

```python
import functools
import math
from typing import NamedTuple

import jax
import jax.numpy as jnp
from jax import lax
from jax.experimental import pallas as pl
from jax.experimental.pallas import tpu as pltpu

F32 = jnp.float32
BF16 = jnp.bfloat16
N_DEV = 8
LANES = 128
VMEM_LIMIT = 52 << 20
HI = lax.Precision.HIGHEST

ADAM_LR = 0.001
ADAM_B1 = 0.9
ADAM_B2 = 0.999
ADAM_EPS = 1e-08
ADAM_WD = 0.01
ADAM_STEP = 10

GDN_CHUNK = 64
CONF_HALO = 32
SHORT_HALO = 8
ADAMW_BLOCK_ELEMS = 128 * 1024


def _cparams(sem):
    return pltpu.CompilerParams(dimension_semantics=sem, vmem_limit_bytes=VMEM_LIMIT)


def _pick(n, cands):
    for c in cands:
        if c <= n and n % c == 0:
            return c
    return n


def _peer(k):
    x, y, c = lax.axis_index("x"), lax.axis_index("y"), lax.axis_index("c")
    return (x ^ ((k >> 2) & 1), y ^ ((k >> 1) & 1), c ^ (k & 1))


def _me():
    return 4 * lax.axis_index("x") + 2 * lax.axis_index("y") + lax.axis_index("c")


_HBM = pl.BlockSpec(memory_space=pltpu.HBM)


def _exchange(x, name, gather):
    shape = x.shape if not gather else (N_DEV,) + x.shape

    def body(x_ref, out_ref, send_sems, recv_sems, local_sem):
        me = _me()
        own = pltpu.make_async_copy(x_ref if gather else x_ref.at[me], out_ref.at[me], local_sem)
        own.start()
        sends = []
        for k in range(1, N_DEV):
            peer = me ^ k
            cp = pltpu.make_async_remote_copy(
                src_ref=x_ref if gather else x_ref.at[peer], dst_ref=out_ref.at[me],
                send_sem=send_sems.at[k - 1], recv_sem=recv_sems.at[k - 1],
                device_id=_peer(k), device_id_type=pl.DeviceIdType.MESH)
            cp.start()
            sends.append(cp)
        for cp in sends:
            cp.wait_send()
        for k in range(1, N_DEV):
            peer = me ^ k
            pltpu.make_async_remote_copy(
                src_ref=x_ref if gather else x_ref.at[peer], dst_ref=out_ref.at[peer],
                send_sem=send_sems.at[k - 1], recv_sem=recv_sems.at[k - 1],
                device_id=_peer(k), device_id_type=pl.DeviceIdType.MESH).wait_recv()
        own.wait()

    return pl.pallas_call(
        body, name=name, out_shape=jax.ShapeDtypeStruct(shape, x.dtype),
        in_specs=[_HBM], out_specs=_HBM,
        scratch_shapes=[pltpu.SemaphoreType.DMA((N_DEV - 1,)), pltpu.SemaphoreType.DMA((N_DEV - 1,)),
                        pltpu.SemaphoreType.DMA(())],
    )(x)


def all_gather(x, name):
    return _exchange(x, name, True)


def all_to_all(x, name):
    return _exchange(x, name, False)


_SEM = pl.BlockSpec(memory_space=pltpu.SEMAPHORE)
_ANY = pl.BlockSpec(memory_space=pl.ANY)
_DATAFLOW = pltpu.SideEffectType.DATAFLOW_SIDE_EFFECTING


def _split_copy(k, gather, x_ref, land_ref, send_sems, recv_sems, incoming):
    me = _me()
    peer = me ^ k
    return pltpu.make_async_remote_copy(
        src_ref=x_ref if gather else x_ref.at[peer], dst_ref=land_ref.at[peer if incoming else me],
        send_sem=send_sems.at[k - 1], recv_sem=recv_sems.at[k - 1],
        device_id=_peer(k), device_id_type=pl.DeviceIdType.MESH)


def exchange_start(x, name, gather, after=None):
    land_shape = ((N_DEV,) + x.shape) if gather else x.shape

    def body(x_ref, land_ref, *rest):
        send_sems, recv_sems, _, _, token = rest[-5:]
        for k in range(1, N_DEV):
            _split_copy(k, gather, x_ref, land_ref, send_sems, recv_sems, False).start()
        token[...] = jnp.zeros_like(token)

    sems = pltpu.SemaphoreType.DMA((N_DEV - 1,))
    operands = [pltpu.with_memory_space_constraint(x, pltpu.HBM),
                pltpu.with_memory_space_constraint(lax.empty(land_shape, x.dtype), pltpu.HBM)]
    outs = pl.pallas_call(
        body, name=name,
        out_shape=(sems, sems, pltpu.HBM(x.shape, x.dtype), pltpu.HBM(land_shape, x.dtype),
                   jax.ShapeDtypeStruct((8, LANES), F32)),
        in_specs=[_HBM, _HBM] + ([_ANY] if after is not None else []),
        out_specs=(_SEM, _SEM, _HBM, _HBM, pl.BlockSpec(memory_space=pltpu.VMEM)),
        input_output_aliases={0: 2, 1: 3},
        compiler_params=pltpu.CompilerParams(has_side_effects=_DATAFLOW),
    )(*operands, *([after] if after is not None else []))
    return tuple(outs[:4]), outs[4]


def exchange_wait(handle, name, gather, after):
    send_sems, recv_sems, x_thru, land_thru = handle

    def body(x_ref, land_ref, send_sems, recv_sems, after_ref, x_out, land_out):
        for k in range(1, N_DEV):
            _split_copy(k, gather, x_ref, land_ref, send_sems, recv_sems, False).wait_send()
        for k in range(1, N_DEV):
            _split_copy(k, gather, x_ref, land_ref, send_sems, recv_sems, True).wait_recv()

    return pl.pallas_call(
        body, name=name,
        out_shape=(pltpu.HBM(x_thru.shape, x_thru.dtype), pltpu.HBM(land_thru.shape, land_thru.dtype)),
        in_specs=[_HBM, _HBM, _SEM, _SEM, _ANY], out_specs=(_HBM, _HBM), input_output_aliases={0: 0, 1: 1},
        compiler_params=pltpu.CompilerParams(has_side_effects=_DATAFLOW),
    )(x_thru, land_thru, send_sems, recv_sems, after)[1]


_DIMS = {"nn": (((1,), (0,)), ((), ())), "nt": (((1,), (1,)), ((), ())), "tn": (((0,), (0,)), ((), ()))}


V7X_MXU_FLOPS = 9.0e14
V7X_HBM_BYTES_PER_S = 3.0e12
V7X_VMEM_STORE_BYTES_PER_S = 4.0e12
GRID_STEP_S = 0.35e-6
MATMUL_VMEM_BUDGET = 40 << 20


def _matmul_tiles(M, N, K, a_bytes, b_bytes, has_add, tm_on_lanes):
    def divisors(n, cands):
        got = [c for c in cands if c <= n and n % c == 0]
        return got or [n]

    best = None
    for tn in divisors(N, (2048, 1408, 1024, 512, 256, 128)):
        for tm in divisors(M, (2048, 1408, 1024, 512, 256, 128) + (() if tm_on_lanes else (64, 32, 16, 8))):
            for tk in divisors(K, (2816, 2048, 1408, 1024, 512, 256, 128)):
                nk = K // tk
                vmem = 2 * (tm * tk * a_bytes + tk * tn * b_bytes + tm * tn * 4 * (2 if has_add else 1))
                vmem += tm * tn * 4 * (2 if nk > 1 else 1)
                if vmem > MATMUL_VMEM_BUDGET:
                    continue
                a_reads = (N // tn) if nk > 1 else 1
                hbm = M * K * a_bytes * a_reads + K * N * b_bytes * (M // tm) + M * N * 4 * (2 if has_add else 1)
                t_mxu = 2.0 * M * N * K / V7X_MXU_FLOPS + (M * N * nk * 8 / V7X_VMEM_STORE_BYTES_PER_S if nk > 1 else 0.0)
                t = max(t_mxu, hbm / V7X_HBM_BYTES_PER_S) + GRID_STEP_S * (M // tm) * (N // tn) * nk
                if best is None or t < best[0]:
                    best = (t, tm, tn, tk)
    return best[1:]


def matmul(a, b, mode, name, add=None, out_dtype=F32, deps=()):
    if mode == "nn":
        (M, K), N = a.shape, b.shape[1]
    elif mode == "nt":
        (M, K), N = a.shape, b.shape[0]
    else:
        (K, M), N = a.shape, b.shape[1]
    tm, tn, tk = _matmul_tiles(M, N, K, a.dtype.itemsize, b.dtype.itemsize, add is not None, mode == "tn")
    nk = K // tk
    if mode == "nn":
        a_spec = pl.BlockSpec((tm, tk), lambda i, j, k: (i, k))
        b_spec = pl.BlockSpec((tk, tn), lambda i, j, k: (k, j))
    elif mode == "nt":
        a_spec = pl.BlockSpec((tm, tk), lambda i, j, k: (i, k))
        b_spec = pl.BlockSpec((tn, tk), lambda i, j, k: (j, k))
    else:
        a_spec = pl.BlockSpec((tk, tm), lambda i, j, k: (k, i))
        b_spec = pl.BlockSpec((tk, tn), lambda i, j, k: (k, j))
    o_spec = pl.BlockSpec((tm, tn), lambda i, j, k: (i, j))
    dims = _DIMS[mode]

    def body(*refs):
        a_ref, b_ref = refs[:2]
        add_ref = refs[2] if add is not None else None
        o_ref = refs[(3 if add is not None else 2) + len(deps)]
        part = lax.dot_general(a_ref[...].astype(BF16), b_ref[...].astype(BF16), dims, preferred_element_type=F32)

        def finish(r):
            if add is not None:
                r = r + add_ref[...]
            o_ref[...] = r.astype(o_ref.dtype)

        if nk == 1:
            finish(part)
            return
        acc_ref = refs[-1]
        k = pl.program_id(2)

        @pl.when(k == 0)
        def _():
            acc_ref[...] = part

        @pl.when(k > 0)
        def _():
            acc_ref[...] += part

        @pl.when(k == nk - 1)
        def _():
            finish(acc_ref[...])

    ins = ([a, b] if add is None else [a, b, add]) + list(deps)
    specs = ([a_spec, b_spec] if add is None else [a_spec, b_spec, o_spec]) + [_ANY] * len(deps)
    return pl.pallas_call(
        body, name=name, grid=(M // tm, N // tn, nk), in_specs=specs, out_specs=o_spec,
        out_shape=jax.ShapeDtypeStruct((M, N), out_dtype),
        scratch_shapes=[pltpu.VMEM((tm, tn), F32)] if nk > 1 else [],
        compiler_params=_cparams(("parallel", "parallel", "arbitrary")),
    )(*ins)


class Row(NamedTuple):
    arr: jax.Array
    cb: int
    cw: int
    halo: int = 0


def _row_vals(refs, rows, first):
    vals, it = [], iter(refs)
    for r in rows:
        cur = next(it)[...].astype(F32)
        if r.halo:
            prev = next(it)[...].astype(F32)
            prev = jnp.where(first, jnp.zeros_like(prev), prev)
            cur = jnp.concatenate([prev, cur], axis=0)
        vals.append(cur)
    return vals


def _row_specs(rows, tr, rev_nt=None):
    specs = []
    for r in rows:
        def cur_map(c, i, r=r):
            return ((rev_nt - 1 - i) if rev_nt else i, r.cb + c)
        specs.append(pl.BlockSpec((tr, r.cw), cur_map))
        if r.halo:
            q = tr // r.halo

            def prev_map(c, i, r=r, q=q):
                t = (rev_nt - 1 - i) if rev_nt else i
                return (jnp.maximum(t * q - 1, 0), r.cb + c)
            specs.append(pl.BlockSpec((r.halo, r.cw), prev_map))
    return specs


def _row_args(rows):
    args = []
    for r in rows:
        args.append(r.arr)
        if r.halo:
            args.append(r.arr)
    return args


def _param_specs(params, ncb):
    return [pl.BlockSpec((p.shape[0], p.shape[1] // ncb), lambda c, i: (0, c)) for p in params]


def tile_fwd(name, f, rows, params, outs, tr, ncb=1):
    S = rows[0].arr.shape[0]
    nt = S // tr
    n_in = sum(2 if r.halo else 1 for r in rows)

    def body(*refs):
        first = pl.program_id(1) == 0
        rv = _row_vals(refs[:n_in], rows, first)
        pv = [p[...] for p in refs[n_in:n_in + len(params)]]
        res = f(rv, pv)
        for o_ref, o in zip(refs[n_in + len(params):], res):
            o_ref[...] = o.astype(o_ref.dtype)

    return pl.pallas_call(
        body, name=name, grid=(ncb, nt),
        in_specs=_row_specs(rows, tr) + _param_specs(params, ncb),
        out_specs=[pl.BlockSpec((tr, cw), lambda c, i: (i, c)) for cw, _ in outs],
        out_shape=[jax.ShapeDtypeStruct((S, cw * ncb), dt) for cw, dt in outs],
        compiler_params=_cparams(("parallel", "parallel")),
    )(*_row_args(rows), *params)


def tile_bwd(name, f, rows, params, couts, tr, ncb=1, adds=None):
    S = rows[0].arr.shape[0]
    nt = S // tr
    n_in = sum(2 if r.halo else 1 for r in rows)
    adds = adds or [None] * len(rows)
    add_list = [a for a in adds if a is not None]
    n_p, n_c, n_a, n_r = len(params), len(couts), len(add_list), len(rows)
    halos = [r for r in rows if r.halo]

    def body(*refs):
        i = pl.program_id(1)
        first = i == nt - 1
        pos = 0
        in_refs = refs[pos:pos + n_in]; pos += n_in
        p_refs = refs[pos:pos + n_p]; pos += n_p
        c_refs = refs[pos:pos + n_c]; pos += n_c
        a_refs = list(refs[pos:pos + n_a]); pos += n_a
        dr_refs = refs[pos:pos + n_r]; pos += n_r
        dp_refs = refs[pos:pos + n_p]; pos += n_p
        carry_refs = list(refs[pos:])
        rv = _row_vals(in_refs, rows, first)
        pv = [p[...] for p in p_refs]
        _, vjp = jax.vjp(lambda rv_, pv_: f(rv_, pv_), rv, pv)
        drv, dpv = vjp([c[...].astype(F32) for c in c_refs])
        for r, d, d_ref, a in zip(rows, drv, dr_refs, adds):
            a_val = a_refs.pop(0)[...] if a is not None else None
            if r.halo:
                carry = carry_refs.pop(0)
                cur = d[r.halo:]
                if a_val is not None:
                    cur = cur + a_val
                d_ref[...] = cur

                @pl.when(i > 0)
                def _(d_ref=d_ref, carry=carry, r=r):
                    d_ref[pl.ds(tr - r.halo, r.halo), :] += carry[...]

                carry[...] = d[:r.halo]
            else:
                d_ref[...] = d if a_val is None else d + a_val

        for dp_ref, dp in zip(dp_refs, dpv):
            @pl.when(i == 0)
            def _(dp_ref=dp_ref):
                dp_ref[...] = jnp.zeros_like(dp_ref)

            dp_ref[...] += dp

    rev = lambda c, i: (nt - 1 - i, c)
    return_vals = pl.pallas_call(
        body, name=name, grid=(ncb, nt),
        in_specs=(_row_specs(rows, tr, rev_nt=nt) + _param_specs(params, ncb)
                  + [pl.BlockSpec((tr, c.shape[1] // ncb), rev) for c in couts]
                  + [pl.BlockSpec((tr, a.shape[1] // ncb), rev) for a in add_list]),
        out_specs=([pl.BlockSpec((tr, r.cw), rev) for r in rows] + _param_specs(params, ncb)),
        out_shape=([jax.ShapeDtypeStruct((S, r.cw * ncb), F32) for r in rows]
                   + [jax.ShapeDtypeStruct(p.shape, F32) for p in params]),
        scratch_shapes=[pltpu.VMEM((r.halo, r.cw), F32) for r in halos],
        compiler_params=_cparams(("parallel", "arbitrary")),
    )(*_row_args(rows), *params, *couts, *add_list)
    return list(return_vals[:n_r]), list(return_vals[n_r:])


def adamw(parts, w, m, v, name, layer=None, into=None, sent=None):
    shape = w.shape
    C = shape[-1]
    R = math.prod(shape[:-1])
    rows = R if layer is None else R // shape[0]
    parts2, w2, m2, v2 = parts.reshape(N_DEV, rows, C), w.reshape(R, C), m.reshape(R, C), v.reshape(R, C)
    lanes = -(-C // LANES) * LANES
    tr = _pick(rows, [t for t in (1024, 512, 256, 128, 64, 32, 16, 8) if t * lanes <= ADAMW_BLOCK_ELEMS])
    first = 0 if layer is None else layer * (rows // tr)

    def body(p_ref, w_ref, m_ref, v_ref, *rest):
        g_out, d_out, m_out, v_out = rest[-4:]

        def part(d):
            if sent is None:
                return p_ref[d].astype(F32)
            return jnp.where(_me() == d, rest[0][d], p_ref[d]).astype(F32)

        g = part(0)
        for d in range(1, N_DEV):
            g = g + part(d)
        mm = ADAM_B1 * m_ref[...] + (1.0 - ADAM_B1) * g
        vv = ADAM_B2 * v_ref[...] + (1.0 - ADAM_B2) * jnp.square(g)
        m_hat = mm / (1.0 - ADAM_B1 ** ADAM_STEP)
        v_hat = vv / (1.0 - ADAM_B2 ** ADAM_STEP)
        g_out[...] = g
        d_out[...] = -ADAM_LR * (m_hat / (jnp.sqrt(v_hat) + ADAM_EPS) + ADAM_WD * w_ref[...])
        m_out[...] = mm
        v_out[...] = vv

    blk = pl.BlockSpec((tr, C), lambda i: (first + i, 0))
    prev = [] if into is None else [a.reshape(R, C) for a in into]
    mine = [] if sent is None else [sent.reshape(N_DEV, rows, C)]
    eight = pl.BlockSpec((N_DEV, tr, C), lambda i: (0, i, 0))
    outs = pl.pallas_call(
        body, name=name, grid=(rows // tr,),
        in_specs=[eight, blk, blk, blk] + [eight] * len(mine) + [_ANY] * len(prev),
        out_specs=[blk] * 4, out_shape=[jax.ShapeDtypeStruct((R, C), F32)] * 4,
        input_output_aliases={4 + len(mine) + j: j for j in range(len(prev))},
        compiler_params=_cparams(("parallel",)),
    )(parts2, w2, m2, v2, *mine, *prev)
    return tuple(o.reshape(shape) for o in outs)


def _dot(a, b, mode):
    return lax.dot_general(a.astype(BF16), b.astype(BF16), _DIMS[mode], preferred_element_type=F32)


@jax.custom_vjp
def mm_nn(a, b):
    return _dot(a, b, "nn")


@jax.custom_vjp
def mm_nt(a, b):
    return _dot(a, b, "nt")


@jax.custom_vjp
def mm_tn(a, b):
    return _dot(a, b, "tn")


mm_nn.defvjp(lambda a, b: (_dot(a, b, "nn"), (a, b)), lambda r, g: (mm_nt(g, r[1]), mm_tn(r[0], g)))
mm_nt.defvjp(lambda a, b: (_dot(a, b, "nt"), (a, b)), lambda r, g: (mm_nn(g, r[1]), mm_tn(g, r[0])))
mm_tn.defvjp(lambda a, b: (_dot(a, b, "tn"), (a, b)), lambda r, g: (mm_nt(r[1], g), mm_nn(r[0], g)))


def _mmh(a, b):
    return jnp.dot(a, b, precision=HI, preferred_element_type=F32)


def _dot3(a, b, mode):
    ah, bh = a.astype(BF16), b.astype(BF16)
    al, bl = (a - ah.astype(F32)).astype(BF16), (b - bh.astype(F32)).astype(BF16)

    def d(x, y):
        return lax.dot_general(x, y, _DIMS[mode], preferred_element_type=F32)
    return d(ah, bh) + (d(ah, bl) + d(al, bh))


@jax.custom_vjp
def mm3_nn(a, b):
    return _dot3(a, b, "nn")


mm3_nn.defvjp(lambda a, b: (_dot3(a, b, "nn"), (a, b)), lambda r, g: (_dot3(g, r[1], "nt"), _dot3(r[0], g, "tn")))


def _sigmoid(x):
    return jax.nn.sigmoid(x)


def _silu(x):
    return x * jax.nn.sigmoid(x)


def _softplus(x):
    return jnp.maximum(x, 0.0) + jnp.log(1.0 + jnp.exp(-jnp.abs(x)))


def _log_sigmoid(x):
    return jnp.minimum(x, 0.0) - jnp.log(1.0 + jnp.exp(-jnp.abs(x)))


def _rms(x, g, eps=1e-6):
    return x * lax.rsqrt(jnp.mean(x * x, axis=-1, keepdims=True) + eps) * g


def _heads(fn, x, hd):
    return jnp.concatenate([fn(x[:, h * hd:(h + 1) * hd]) for h in range(x.shape[1] // hd)], axis=1)


def _causal_conv(x, w, halo, tr):
    K = w.shape[0]
    acc = jnp.zeros((tr, x.shape[1]), F32)
    for k in range(K):
        o = halo - (K - 1) + k
        acc = acc + w[k:k + 1] * x[o:o + tr]
    return acc


def _lanes(shape):
    return lax.broadcasted_iota(jnp.int32, shape, len(shape) - 1)


def f_rms(rv, pv):
    return [_rms(rv[0], pv[0])]


def make_f_fox_prep(H, hd):
    def f(rv, pv):
        fq, fk, fv, small = rv
        fb, gq, gk = pv
        qn = _heads(lambda t: _rms(t, gq), fq, hd) * (hd ** -0.5)
        kn = _heads(lambda t: _rms(t, gk), fk, hd)
        logf = jnp.where(_lanes(small.shape) < H, _log_sigmoid(small + fb), 0.0)
        return [qn, kn, fv, logf]
    return f


def make_f_short_conv(tr, act):
    def f(rv, pv):
        y = _causal_conv(rv[0], pv[0], SHORT_HALO, tr)
        return [_silu(y) if act else y]
    return f


def make_f_gdn_prep(H, hd):
    def l2(t):
        return t * lax.rsqrt(jnp.sum(t * t, axis=-1, keepdims=True) + 1e-6)

    def f(rv, pv):
        q, k, small = rv
        alog, dtb = pv
        qn = _heads(l2, q, hd) * (hd ** -0.5)
        kn = _heads(l2, k, hd)
        ln = _lanes(small.shape)
        g = -jnp.exp(alog) * _softplus(small + dtb)
        gates = jnp.where((ln >= H) & (ln < 2 * H), g, jnp.where((ln >= 2 * H) & (ln < 3 * H), _sigmoid(small), 0.0))
        return [qn, kn, gates]
    return f


def _tri(n, strict=False, upper=False):
    ii = lax.broadcasted_iota(jnp.int32, (n, n), 0)
    jj = lax.broadcasted_iota(jnp.int32, (n, n), 1)
    if upper:
        ii, jj = jj, ii
    return ii > jj if strict else ii >= jj


def _decay(gcol, mask):
    C = gcol.shape[0]
    G = jnp.broadcast_to(gcol, (C, C))
    return jnp.where(mask, jnp.exp(jnp.where(mask, G - G.T, 0.0)), 0.0)


def _nilpotent_inverses(Xs):
    C = Xs[0].shape[0]
    eye = (lax.broadcasted_iota(jnp.int32, (C, C), 0) == lax.broadcasted_iota(jnp.int32, (C, C), 1)).astype(F32)
    Ts, Ps = [eye + X for X in Xs], list(Xs)
    for _ in range(int(math.log2(C)) - 1):
        Ps = [mm3_nn(P, P) for P in Ps]
        Ts = [T + mm3_nn(T, P) for T, P in zip(Ts, Ps)]
    return Ts


@jax.custom_vjp
def _saved_inverse(X, T):
    return T


_saved_inverse.defvjp(lambda X, T: (T, T),
                      lambda T, g: (_dot3(T, _dot3(g, T, "nt"), "tn"), jnp.zeros_like(T)))


def make_f_gdn_solve(H, hd, saved):
    C = GDN_CHUNK

    def f(rv, pv):
        k, v, gates = rv[:3]
        gc = _mmh(_tri(C).astype(F32), gates)
        strict = _tri(C, strict=True)
        heads = range(H)
        ks = [k[:, h * hd:(h + 1) * hd] for h in heads]
        gcols = [gc[:, H + h:H + h + 1] for h in heads]
        betas = [gates[:, 2 * H + h:2 * H + h + 1] for h in heads]
        kbs = [kh * b for kh, b in zip(ks, betas)]
        vbs = [v[:, h * hd:(h + 1) * hd] * betas[h] for h in heads]
        Xs = [-(mm_nt(kb, kh) * _decay(g, strict)) for kb, kh, g in zip(kbs, ks, gcols)]
        if saved:
            Ts = [_saved_inverse(X, rv[3][:, h * C:(h + 1) * C]) for h, X in enumerate(Xs)]
        else:
            Ts = _nilpotent_inverses(Xs)
        us = [mm3_nn(T, vb) for T, vb in zip(Ts, vbs)]
        ws = [mm3_nn(T, kb * jnp.exp(g)) for T, kb, g in zip(Ts, kbs, gcols)]
        out = [jnp.concatenate(us, axis=1), jnp.concatenate(ws, axis=1), gc]
        return out if saved else out + [jnp.concatenate(Ts, axis=1)]
    return f


def _gdn_steps(Ss, qs, ks, us, ws, gcols):
    C = qs[0].shape[0]
    causal = _tri(C)
    rows = lax.broadcasted_iota(jnp.int32, gcols[0].shape, 0)
    attn = [mm_nt(q, k) * _decay(g, causal) for q, k, g in zip(qs, ks, gcols)]
    glast = [jnp.sum(jnp.where(rows == C - 1, g, 0.0), axis=0, keepdims=True) for g in gcols]
    v_new = [u - mm_nn(w, S) for u, w, S in zip(us, ws, Ss)]
    o_state = [mm_nn(q * jnp.exp(g), S) for q, g, S in zip(qs, gcols, Ss)]
    o_chunk = [mm_nn(a, vn) for a, vn in zip(attn, v_new)]
    update = [mm_tn(k * jnp.exp(gl - g), vn) for k, gl, g, vn in zip(ks, glast, gcols, v_new)]
    S_new = [S * jnp.exp(gl) + d for S, gl, d in zip(Ss, glast, update)]
    return [a + b for a, b in zip(o_state, o_chunk)], S_new


def make_f_gdn_out(hd):
    def f(rv, pv):
        o, gz = rv
        return [_heads(lambda t: _rms(t, pv[0]), o, hd) * _silu(gz)]
    return f


def make_f_conf(tr):
    def f(rv, pv):
        a, g = rv
        w, b, lg, lb = pv
        y = _causal_conv(a * _sigmoid(g), w, CONF_HALO, tr) + b
        xc = y - jnp.mean(y, axis=-1, keepdims=True)
        y = xc * lax.rsqrt(jnp.mean(xc * xc, axis=-1, keepdims=True) + 1e-5) * lg + lb
        return [_silu(y)]
    return f


def f_merge(rv, pv):
    n = len(rv) // 2
    y = _sigmoid(rv[0]) * rv[n]
    for j in range(1, n):
        y = y + _sigmoid(rv[j]) * rv[n + j]
    return [y]


def make_f_xattn(XH, hd):
    XW = XH * hd

    def f(rv, pv):
        q = rv[0]
        kv, gq, gk = pv
        outs = []
        for h in range(XH):
            qh = _rms(q[:, h * hd:(h + 1) * hd], gq)
            kh = _rms(kv[:, h * hd:(h + 1) * hd], gk)
            s = mm_nt(qh, kh) * (hd ** -0.5)
            e = jnp.exp(s - jnp.max(s, axis=-1, keepdims=True))
            outs.append(mm_nn(e / jnp.sum(e, axis=-1, keepdims=True), kv[:, XW + h * hd:XW + (h + 1) * hd]))
        return [jnp.concatenate(outs, axis=1)]
    return f


def make_f_ffn_act(tr):
    def f(rv, pv):
        a, v = rv
        return [_silu(_causal_conv(a, pv[0], SHORT_HALO, tr) + pv[1]) * v]
    return f


def make_f_dprep(H, hd):
    def f(rv, pv):
        do, o = rv
        ind = (lax.broadcasted_iota(jnp.int32, (H * hd, LANES), 0) // hd
               == lax.broadcasted_iota(jnp.int32, (H * hd, LANES), 1)).astype(F32)
        return [do, _mmh(do * o, ind)]
    return f


def cumsum_rows(x, name, reverse=False):
    S, C = x.shape
    tr = _pick(S, (256, 128, 64))
    nt = S // tr

    def body(x_ref, o_ref, carry):
        @pl.when(pl.program_id(0) == 0)
        def _():
            carry[...] = jnp.zeros_like(carry)

        ii = lax.broadcasted_iota(jnp.int32, (tr, tr), 0)
        jj = lax.broadcasted_iota(jnp.int32, (tr, tr), 1)
        tri = (ii <= jj) if reverse else (ii >= jj)
        y = _mmh(tri.astype(F32), x_ref[...]) + carry[...]
        o_ref[...] = y
        rows = lax.broadcasted_iota(jnp.int32, y.shape, 0)
        carry[...] = jnp.sum(jnp.where(rows == (0 if reverse else tr - 1), y, 0.0), axis=0, keepdims=True)

    spec = pl.BlockSpec((tr, C), (lambda i: (nt - 1 - i, 0)) if reverse else (lambda i: (i, 0)))
    return pl.pallas_call(
        body, name=name, grid=(nt,), in_specs=[spec], out_specs=spec,
        out_shape=jax.ShapeDtypeStruct((S, C), F32), scratch_shapes=[pltpu.VMEM((1, C), F32)],
        compiler_params=_cparams(("arbitrary",)),
    )(x)


def _fox_block(S):
    return _pick(S, (1024, 512, 256, 128))


def fox_fwd(q, k, v, frow, H, hd, name):
    S = q.shape[0]
    bq = _fox_block(S)
    nq = S // bq

    def body(q_ref, k_ref, v_ref, fr_ref, o_ref, lse_ref):
        i = pl.program_id(1)
        qv = q_ref[...]

        def step(j, carry, diag):
            m, l, acc = carry
            cols = pl.ds(pl.multiple_of(j * bq, bq), bq)
            kj, vj = k_ref[cols, :], v_ref[cols, :]
            s = lax.dot_general(qv, kj, _DIMS["nt"], preferred_element_type=F32) - fr_ref[0, :, cols]
            if diag:
                s = jnp.where(_tri(bq), s, -jnp.inf)
            m_new = jnp.maximum(m, jnp.max(s, axis=-1, keepdims=True))
            alpha = jnp.exp(m - m_new)
            p = jnp.exp(s - m_new)
            l = alpha * l + jnp.sum(p, axis=-1, keepdims=True)
            acc = alpha * acc + lax.dot_general(p.astype(BF16), vj, _DIMS["nn"], preferred_element_type=F32)
            return m_new, l, acc

        init = (jnp.full((bq, 1), -jnp.inf, F32), jnp.zeros((bq, 1), F32), jnp.zeros((bq, hd), F32))
        carry = lax.fori_loop(0, i, lambda j, c: step(j, c, False), init)
        m, l, acc = step(i, carry, True)
        o_ref[...] = acc / l
        lse_ref[0] = m + jnp.log(l)

    return pl.pallas_call(
        body, name=name, grid=(H, nq),
        in_specs=[pl.BlockSpec((bq, hd), lambda h, i: (i, h)),
                  pl.BlockSpec((S, hd), lambda h, i: (0, h)), pl.BlockSpec((S, hd), lambda h, i: (0, h)),
                  pl.BlockSpec((1, 1, S), lambda h, i: (h, 0, 0))],
        out_specs=[pl.BlockSpec((bq, hd), lambda h, i: (i, h)), pl.BlockSpec((1, bq, 1), lambda h, i: (h, i, 0))],
        out_shape=[jax.ShapeDtypeStruct((S, H * hd), F32), jax.ShapeDtypeStruct((H, S, 1), F32)],
        compiler_params=_cparams(("parallel", "parallel")),
    )(q, k, v, frow)


def fox_bwd(q, k, v, do, fcol, lse_row, delta_row, H, hd, name):
    S = q.shape[0]
    bk = _fox_block(S)
    nk = S // bk

    def body(q_ref, do_ref, k_ref, v_ref, fc_ref, lse_ref, dl_ref, dq_ref, dfq_ref, dk_ref, dv_ref, df_ref):
        j = pl.program_id(1)

        @pl.when(j == 0)
        def _():
            dq_ref[...] = jnp.zeros_like(dq_ref)
            dfq_ref[...] = jnp.zeros_like(dfq_ref)

        kj, vj = k_ref[...], v_ref[...]
        fk = fc_ref[0]

        def step(i, carry, diag):
            dk, dv, df = carry
            rows = pl.ds(pl.multiple_of(i * bk, bk), bk)
            qi, doi = q_ref[rows, :], do_ref[rows, :]
            st = lax.dot_general(kj, qi, _DIMS["nt"], preferred_element_type=F32) - fk - lse_ref[0, :, rows]
            if diag:
                st = jnp.where(_tri(bk, upper=True), st, -jnp.inf)
            pt = jnp.exp(st)
            dv = dv + lax.dot_general(pt.astype(BF16), doi, _DIMS["nn"], preferred_element_type=F32)
            dpt = lax.dot_general(vj, doi, _DIMS["nt"], preferred_element_type=F32)
            dst = pt * (dpt - dl_ref[0, :, rows])
            df = df - jnp.sum(dst, axis=-1, keepdims=True)
            dfq_ref[0, :, rows] += jnp.sum(dst, axis=0, keepdims=True)
            dsb = dst.astype(BF16)
            dk = dk + lax.dot_general(dsb, qi, _DIMS["nn"], preferred_element_type=F32)
            dq_ref[rows, :] += lax.dot_general(dsb, kj, _DIMS["tn"], preferred_element_type=F32)
            return dk, dv, df

        init = (jnp.zeros((bk, hd), F32), jnp.zeros((bk, hd), F32), jnp.zeros((bk, 1), F32))
        carry = step(j, init, True)
        dk, dv, df = lax.fori_loop(j + 1, nk, lambda i, c: step(i, c, False), carry)
        dk_ref[...] = dk
        dv_ref[...] = dv
        df_ref[0] = df

    whole = pl.BlockSpec((S, hd), lambda h, j: (0, h))
    blk = pl.BlockSpec((bk, hd), lambda h, j: (j, h))
    row = pl.BlockSpec((1, 1, S), lambda h, j: (h, 0, 0))
    col = pl.BlockSpec((1, bk, 1), lambda h, j: (h, j, 0))
    return pl.pallas_call(
        body, name=name, grid=(H, nk),
        in_specs=[whole, whole, blk, blk, col, row, row],
        out_specs=[whole, row, blk, blk, col],
        out_shape=[jax.ShapeDtypeStruct((S, H * hd), F32), jax.ShapeDtypeStruct((H, 1, S), F32)]
        + [jax.ShapeDtypeStruct((S, H * hd), F32)] * 2 + [jax.ShapeDtypeStruct((H, S, 1), F32)],
        compiler_params=_cparams(("parallel", "arbitrary")),
    )(q, do, k, v, fcol, lse_row, delta_row)


def gdn_scan_fwd(q, k, u, w, gc, H, hd, name):
    S = q.shape[0]
    C = GDN_CHUNK
    NC = S // C

    def body(q_ref, k_ref, u_ref, w_ref, gc_ref, o_ref, sin_ref, state):
        @pl.when(pl.program_id(0) == 0)
        def _():
            state[...] = jnp.zeros_like(state)

        gcv = gc_ref[...]
        sl = [slice(h * hd, (h + 1) * hd) for h in range(H)]
        Ss = [state[h] for h in range(H)]
        for h in range(H):
            sin_ref[0, h] = Ss[h]
        outs, S_new = _gdn_steps(Ss, [q_ref[:, s] for s in sl], [k_ref[:, s] for s in sl], [u_ref[:, s] for s in sl],
                                 [w_ref[:, s] for s in sl], [gcv[:, H + h:H + h + 1] for h in range(H)])
        for h in range(H):
            state[h] = S_new[h]
        o_ref[...] = jnp.concatenate(outs, axis=1)

    wide = pl.BlockSpec((C, H * hd), lambda i: (i, 0))
    return pl.pallas_call(
        body, name=name, grid=(NC,),
        in_specs=[wide] * 4 + [pl.BlockSpec((C, LANES), lambda i: (i, 0))],
        out_specs=[wide, pl.BlockSpec((1, H, hd, hd), lambda i: (i, 0, 0, 0))],
        out_shape=[jax.ShapeDtypeStruct((S, H * hd), F32), jax.ShapeDtypeStruct((NC, H, hd, hd), F32)],
        scratch_shapes=[pltpu.VMEM((H, hd, hd), F32)],
        compiler_params=_cparams(("arbitrary",)),
    )(q, k, u, w, gc)


def gdn_scan_bwd(q, k, u, w, gc, sin, do, H, hd, name):
    S = q.shape[0]
    C = GDN_CHUNK
    NC = S // C

    def body(q_ref, k_ref, u_ref, w_ref, gc_ref, sin_ref, do_ref, dq_ref, dk_ref, du_ref, dw_ref, dgc_ref, dstate):
        @pl.when(pl.program_id(0) == 0)
        def _():
            dstate[...] = jnp.zeros_like(dstate)

        gcv = gc_ref[...]
        ln = _lanes(gcv.shape)
        dgc = jnp.zeros_like(gcv)
        sl = [slice(h * hd, (h + 1) * hd) for h in range(H)]
        _, vjp = jax.vjp(_gdn_steps, [sin_ref[0, h] for h in range(H)], [q_ref[:, s] for s in sl],
                         [k_ref[:, s] for s in sl], [u_ref[:, s] for s in sl], [w_ref[:, s] for s in sl],
                         [gcv[:, H + h:H + h + 1] for h in range(H)])
        dS, dq, dk, du, dw, dg = vjp(([do_ref[:, s] for s in sl], [dstate[h] for h in range(H)]))
        for h in range(H):
            dstate[h] = dS[h]
            dgc = dgc + jnp.where(ln == H + h, dg[h], 0.0)
        for ref, lst in zip((dq_ref, dk_ref, du_ref, dw_ref), (dq, dk, du, dw)):
            ref[...] = jnp.concatenate(lst, axis=1)
        dgc_ref[...] = dgc

    wide = pl.BlockSpec((C, H * hd), lambda i: (NC - 1 - i, 0))
    narrow = pl.BlockSpec((C, LANES), lambda i: (NC - 1 - i, 0))
    return pl.pallas_call(
        body, name=name, grid=(NC,),
        in_specs=[wide] * 4 + [narrow, pl.BlockSpec((1, H, hd, hd), lambda i: (NC - 1 - i, 0, 0, 0)), wide],
        out_specs=[wide] * 4 + [narrow],
        out_shape=[jax.ShapeDtypeStruct((S, H * hd), F32)] * 4 + [jax.ShapeDtypeStruct((S, LANES), F32)],
        scratch_shapes=[pltpu.VMEM((H, hd, hd), F32)],
        compiler_params=_cparams(("arbitrary",)),
    )(q, k, u, w, gc, sin, do)


def loss_head(y, t, name):
    S, D = y.shape
    tr = _pick(S, (256, 128, 64))

    def body(y_ref, t_ref, dy_ref, acc_ref):
        @pl.when(pl.program_id(0) == 0)
        def _():
            acc_ref[...] = jnp.zeros_like(acc_ref)

        err = y_ref[...] - t_ref[...]
        dy_ref[...] = err / D
        acc_ref[...] += jnp.sum(jnp.mean(err * err, axis=-1, keepdims=True), axis=0, keepdims=True)

    blk = pl.BlockSpec((tr, D), lambda i: (i, 0))
    dy, acc = pl.pallas_call(
        body, name=name, grid=(S // tr,), in_specs=[blk, blk],
        out_specs=[blk, pl.BlockSpec((8, LANES), lambda i: (0, 0))],
        out_shape=[jax.ShapeDtypeStruct((S, D), F32), jax.ShapeDtypeStruct((8, LANES), F32)],
        compiler_params=_cparams(("arbitrary",)),
    )(y, t)
    return 0.5 * acc[0, 0], dy


class Dims(NamedTuple):
    D: int
    H: int
    hd: int
    MW: int
    XH: int
    FF: int

    @property
    def n_in(self):
        return 9 * self.MW + 3 * self.H + 3 * self.D

    @property
    def n_in_padded(self):
        return 3 * self.D + 9 * self.MW + LANES


def _in_pieces(dm):
    MW, H, D = dm.MW, dm.H, dm.D
    fq, fk, fv, ff = 0, MW, 2 * MW, 3 * MW
    gq = ff + H
    gk, gv = gq + MW, gq + 2 * MW
    ga = gv + MW
    gb, gz = ga + H, ga + 2 * H
    cu = gz + MW
    gl = cu + 2 * MW
    return [(gl, 3 * D), (fq, MW), (fk, MW), (fv, MW), (gq, MW), (gk, MW), (gv, MW), (gz, MW), (cu, 2 * MW),
            (ff, H), (ga, H), (gb, H)]


def permute_in_blocks(blocks, dm):
    n = blocks[0].shape[-1]
    parts = []
    for s, size in _in_pieces(dm):
        e = s + size
        while s < e:
            d = s // n
            hi = min(e, (d + 1) * n)
            parts.append(blocks[d][:, s - d * n:hi - d * n])
            s = hi
    parts.append(jnp.zeros((blocks[0].shape[0], LANES - 3 * dm.H), blocks[0].dtype))
    return jnp.concatenate(parts, axis=1)


def unpermute_to_blocks(wp, dm):
    segs, off = [], 0
    for s, size in _in_pieces(dm):
        segs.append((s, size, off))
        off += size
    segs.sort()
    n = dm.n_in // N_DEV
    chunks = []
    for d in range(N_DEV):
        lo, hi = d * n, (d + 1) * n
        parts = [wp[:, off + max(s, lo) - s:off + min(s + size, hi) - s] for s, size, off in segs
                 if max(s, lo) < min(s + size, hi)]
        chunks.append(jnp.concatenate(parts, axis=1))
    return jnp.stack(chunks, axis=0)


def _pad_lanes(v, at):
    return jnp.pad(v, (at, LANES - at - v.shape[0]))[None]


def _ops(x, m, P, dm, t):
    S = x.shape[0]
    tr = _pick(S, (256, 128, 64))
    return dict(tr=tr, trm=_pick(S, (128, 64)), trx=_pick(S, (512, 256, 128, 64)),
                cbq=3 * dm.D // dm.MW, cbs=(3 * dm.D + 9 * dm.MW) // LANES,
                cwf=_pick(dm.FF, (1408, 512, 256, 128)))


def layer_fwd(x, m, P, dm, t):
    D, H, hd, MW, XH, FF = dm
    XW = XH * hd
    c = _ops(x, m, P, dm, t)
    tr, cbq, cbs = c["tr"], c["cbq"], c["cbs"]
    (h,) = tile_fwd(t + "mix_norm", f_rms, [Row(x, 0, D)], [P["mix_norm_g"][None]], [(D, BF16)], tr)
    p = matmul(h, P.big("w_in", h), "nn", t + "in_proj", deps=P.deps())
    fox_rows = [Row(p, cbq, MW), Row(p, cbq + 1, MW), Row(p, cbq + 2, MW), Row(p, cbs, LANES)]
    fox_params = [_pad_lanes(P["fox_fb"], 0), P["fox_q_norm_g"][None], P["fox_k_norm_g"][None]]
    qn, kn, vb, logf = tile_fwd(t + "fox_prep", make_f_fox_prep(H, hd), fox_rows, fox_params,
                                [(MW, BF16)] * 3 + [(LANES, F32)], tr)
    Ft = cumsum_rows(logf, t + "fox_cumsum")[:, :H].T
    fcol, frow = Ft[:, :, None], Ft[:, None, :]
    out_a, lse = fox_fwd(qn, kn, vb, frow, H, hd, t + "fox_attn")
    (qkv,) = tile_fwd(t + "gdn_conv", make_f_short_conv(tr, True), [Row(p, cbq + 3, MW, SHORT_HALO)],
                      [P["gdn_conv_w"]], [(MW, F32)], tr, ncb=3)
    gp_rows = [Row(qkv, 0, MW), Row(qkv, 1, MW), Row(p, cbs, LANES)]
    gp_params = [_pad_lanes(P["gdn_a_log"], H), _pad_lanes(P["gdn_dt_bias"], H)]
    gq, gk, gates = tile_fwd(t + "gdn_prep", make_f_gdn_prep(H, hd), gp_rows, gp_params,
                             [(MW, F32), (MW, F32), (LANES, F32)], tr)
    u, w, gc, tinv = tile_fwd(t + "gdn_solve", make_f_gdn_solve(H, hd, False),
                              [Row(gk, 0, MW), Row(qkv, 2, MW), Row(gates, 0, LANES)], [],
                              [(MW, F32), (MW, F32), (LANES, F32), (H * GDN_CHUNK, F32)], GDN_CHUNK)
    o_g, sin = gdn_scan_fwd(gq, gk, u, w, gc, H, hd, t + "gdn_scan")
    (out_b,) = tile_fwd(t + "gdn_out", make_f_gdn_out(hd), [Row(o_g, 0, MW), Row(p, cbq + 6, MW)],
                        [P["gdn_out_norm_g"][None]], [(MW, F32)], tr)
    conf_params = [P["conf_dw_w"], P["conf_dw_b"][None], P["conf_ln_g"][None], P["conf_ln_b"][None]]
    (out_c,) = tile_fwd(t + "conf", make_f_conf(tr), [Row(p, cbq + 7, MW, CONF_HALO), Row(p, cbq + 8, MW, CONF_HALO)],
                        conf_params, [(MW, F32)], tr)
    branches = [out_a, out_b, out_c]
    proj = [matmul(b, P.big("w_branch", out_a)[n], "nn", t + f"branch{n}") for n, b in enumerate(branches)]
    (y,) = tile_fwd(t + "merge", f_merge, [Row(p, n, D) for n in range(3)] + [Row(pr, 0, D) for pr in proj], [],
                    [(D, BF16)], c["trm"])
    x1 = matmul(y, P.big("w_out", y), "nn", t + "out_proj", add=x)
    (h2,) = tile_fwd(t + "xa_norm", f_rms, [Row(x1, 0, D)], [P["xattn_norm_g"][None]], [(D, BF16)], tr)
    q = matmul(h2, P.big("xattn_wq", h2), "nn", t + "xa_q")
    kv = matmul(m, P.big("xattn_wkv", h2), "nn", t + "xa_kv")
    xa_params = [kv, P["xattn_q_norm_g"][None], P["xattn_k_norm_g"][None]]
    (o_x,) = tile_fwd(t + "xa_attn", make_f_xattn(XH, hd), [Row(q, 0, XW)], xa_params, [(XW, F32)], c["trx"])
    x2 = matmul(o_x, P.big("xattn_wo", o_x), "nn", t + "xa_o", add=x1)
    (h3,) = tile_fwd(t + "ffn_norm", f_rms, [Row(x2, 0, D)], [P["ffn_norm_g"][None]], [(D, BF16)], tr)
    av = matmul(h3, P.big("ffn_w_up", h3), "nn", t + "ffn_up")
    cwf = c["cwf"]
    (uf,) = tile_fwd(t + "ffn_act", make_f_ffn_act(tr), [Row(av, 0, cwf, SHORT_HALO), Row(av, FF // cwf, cwf)],
                     [P["ffn_conv_w"], P["ffn_conv_b"][None]], [(cwf, BF16)], tr, ncb=FF // cwf)
    x3 = matmul(uf, P.big("ffn_w_down", uf), "nn", t + "ffn_down", add=x2)
    res = dict(x=x, h=h, p=p, qn=qn, kn=kn, vb=vb, fcol=fcol, frow=frow, out_a=out_a, lse=lse, qkv=qkv, gq=gq, gk=gk,
               gates=gates, u=u, w=w, gc=gc, tinv=tinv, sin=sin, o_g=o_g, out_b=out_b, out_c=out_c, proj=proj, y=y, x1=x1, h2=h2,
               q=q, kv=kv, o_x=o_x, x2=x2, h3=h3, av=av, uf=uf)
    return x3, res


def layer_bwd(dx3, m, P, R, dm, t):
    D, H, hd, MW, XH, FF = dm
    XW = XH * hd
    c = _ops(R["x"], m, P, dm, t)
    tr, cbq, cbs = c["tr"], c["cbq"], c["cbs"]
    p = R["p"]
    G = {}
    P.emit("ffn_w_down", matmul(R["uf"], dx3, "tn", t + "ffn_down_dw", out_dtype=BF16))
    du = matmul(dx3, P.big("ffn_w_down", None), "nt", t + "ffn_down_dx", deps=P.deps())
    cwf = c["cwf"]
    (da, dv), (G["ffn_conv_w"], dcb) = tile_bwd(
        t + "ffn_act_b", make_f_ffn_act(tr), [Row(R["av"], 0, cwf, SHORT_HALO), Row(R["av"], FF // cwf, cwf)],
        [P["ffn_conv_w"], P["ffn_conv_b"][None]], [du], tr, ncb=FF // cwf)
    G["ffn_conv_b"] = dcb[0]
    dav = jnp.concatenate([da, dv], axis=1).astype(BF16)
    P.emit("ffn_w_up", matmul(R["h3"], dav, "tn", t + "ffn_up_dw", out_dtype=BF16))
    dh3 = matmul(dav, P.big("ffn_w_up", None), "nt", t + "ffn_up_dx", deps=P.deps())
    (dx2,), (dg,) = tile_bwd(t + "ffn_norm_b", f_rms, [Row(R["x2"], 0, D)], [P["ffn_norm_g"][None]], [dh3], tr, adds=[dx3])
    G["ffn_norm_g"] = dg[0]
    P.emit("xattn_wo", matmul(R["o_x"], dx2, "tn", t + "xa_o_dw", out_dtype=BF16))
    do_x = matmul(dx2, P.big("xattn_wo", None), "nt", t + "xa_o_dx", deps=P.deps())
    xa_params = [R["kv"], P["xattn_q_norm_g"][None], P["xattn_k_norm_g"][None]]
    (dq,), (dkv, dgq, dgk) = tile_bwd(t + "xa_attn_b", make_f_xattn(XH, hd), [Row(R["q"], 0, XW)], xa_params, [do_x], c["trx"])
    G["xattn_q_norm_g"], G["xattn_k_norm_g"] = dgq[0], dgk[0]
    P.emit("xattn_wq", matmul(R["h2"], dq, "tn", t + "xa_q_dw", out_dtype=BF16))
    P.emit("xattn_wkv", matmul(m, dkv, "tn", t + "xa_kv_dw", out_dtype=BF16))
    dh2 = matmul(dq, P.big("xattn_wq", None), "nt", t + "xa_q_dx", deps=P.deps())
    dm_l = matmul(dkv, P.big("xattn_wkv", None), "nt", t + "xa_kv_dx")
    (dx1,), (dg,) = tile_bwd(t + "xa_norm_b", f_rms, [Row(R["x1"], 0, D)], [P["xattn_norm_g"][None]], [dh2], tr, adds=[dx2])
    G["xattn_norm_g"] = dg[0]
    P.emit("w_out", matmul(R["y"], dx1, "tn", t + "out_proj_dw", out_dtype=BF16))
    dy = matmul(dx1, P.big("w_out", None), "nt", t + "out_proj_dx", deps=P.deps())
    merge_rows = [Row(p, n, D) for n in range(3)] + [Row(pr, 0, D) for pr in R["proj"]]
    dmerge, _ = tile_bwd(t + "merge_b", f_merge, merge_rows, [], [dy], c["trm"])
    dgl, dpr = dmerge[:3], dmerge[3:]
    branches = [R["out_a"], R["out_b"], R["out_c"]]
    P.emit("w_branch", jnp.stack([matmul(branches[n], dpr[n], "tn", t + f"branch{n}_dw", out_dtype=BF16)
                                  for n in range(3)]))
    dbr = [matmul(dpr[n], P.big("w_branch", None)[n], "nt", t + f"branch{n}_dx", deps=P.deps()) for n in range(3)]
    conf_params = [P["conf_dw_w"], P["conf_dw_b"][None], P["conf_ln_g"][None], P["conf_ln_b"][None]]
    (dcu_a, dcu_g), (G["conf_dw_w"], db, dlg, dlb) = tile_bwd(
        t + "conf_b", make_f_conf(tr), [Row(p, cbq + 7, MW, CONF_HALO), Row(p, cbq + 8, MW, CONF_HALO)],
        conf_params, [dbr[2]], tr)
    G["conf_dw_b"], G["conf_ln_g"], G["conf_ln_b"] = db[0], dlg[0], dlb[0]
    (do_g, dgz), (dg,) = tile_bwd(t + "gdn_out_b", make_f_gdn_out(hd), [Row(R["o_g"], 0, MW), Row(p, cbq + 6, MW)],
                                  [P["gdn_out_norm_g"][None]], [dbr[1]], tr)
    G["gdn_out_norm_g"] = dg[0]
    dgq, dgk2, du_, dw_, dgc = gdn_scan_bwd(R["gq"], R["gk"], R["u"], R["w"], R["gc"], R["sin"], do_g, H, hd, t + "gdn_scan_b")
    (dgk1, dgv, dgates, _), _ = tile_bwd(
        t + "gdn_solve_b", make_f_gdn_solve(H, hd, True),
        [Row(R["gk"], 0, MW), Row(R["qkv"], 2, MW), Row(R["gates"], 0, LANES), Row(R["tinv"], 0, H * GDN_CHUNK)],
        [], [du_, dw_, dgc], GDN_CHUNK)
    gp_rows = [Row(R["qkv"], 0, MW), Row(R["qkv"], 1, MW), Row(p, cbs, LANES)]
    gp_params = [_pad_lanes(P["gdn_a_log"], H), _pad_lanes(P["gdn_dt_bias"], H)]
    (dqa, dka, dsmall_g), (dal, ddt) = tile_bwd(t + "gdn_prep_b", make_f_gdn_prep(H, hd), gp_rows, gp_params,
                                               [dgq, dgk1 + dgk2, dgates], tr)
    G["gdn_a_log"], G["gdn_dt_bias"] = dal[0, H:2 * H], ddt[0, H:2 * H]
    (dgqkv,), (G["gdn_conv_w"],) = tile_bwd(
        t + "gdn_conv_b", make_f_short_conv(tr, True), [Row(p, cbq + 3, MW, SHORT_HALO)], [P["gdn_conv_w"]],
        [jnp.concatenate([dqa, dka, dgv], axis=1)], tr, ncb=3)
    do_b, delta = tile_fwd(t + "fox_dprep", make_f_dprep(H, hd), [Row(dbr[0], 0, MW), Row(R["out_a"], 0, MW)], [],
                           [(MW, BF16), (LANES, F32)], tr)
    S = p.shape[0]
    delta_row = delta[:, :H].T[:, None, :]
    lse_row = R["lse"].reshape(H, 1, S)
    dqn, dfq_, dkn, dvf, dfk = fox_bwd(R["qn"], R["kn"], R["vb"], do_b, R["fcol"], lse_row, delta_row, H, hd,
                                       t + "fox_attn_b")
    dF = jnp.pad((dfk.reshape(H, S) + dfq_.reshape(H, S)).T, ((0, 0), (0, LANES - H)))
    dlogf = cumsum_rows(dF, t + "fox_cumsum_b", reverse=True)
    fox_rows = [Row(p, cbq, MW), Row(p, cbq + 1, MW), Row(p, cbq + 2, MW), Row(p, cbs, LANES)]
    fox_params = [_pad_lanes(P["fox_fb"], 0), P["fox_q_norm_g"][None], P["fox_k_norm_g"][None]]
    (dfq, dfk_, dfv, dsmall_f), (dfb, dgq_, dgk_) = tile_bwd(t + "fox_prep_b", make_f_fox_prep(H, hd), fox_rows, fox_params,
                                                          [dqn, dkn, dvf, dlogf], tr)
    G["fox_fb"], G["fox_q_norm_g"], G["fox_k_norm_g"] = dfb[0, :H], dgq_[0], dgk_[0]
    dp = jnp.concatenate(dgl + [dfq, dfk_, dfv, dgqkv, dgz, dcu_a, dcu_g, dsmall_f + dsmall_g], axis=1).astype(BF16)
    P.emit("w_in", matmul(R["h"], dp, "tn", t + "in_proj_dw", out_dtype=BF16))
    dh = matmul(dp, P.big("w_in", None), "nt", t + "in_proj_dx", deps=P.deps())
    (dx,), (dg,) = tile_bwd(t + "mix_norm_b", f_rms, [Row(R["x"], 0, D)], [P["mix_norm_g"][None]], [dh], tr, adds=[dx1])
    G["mix_norm_g"] = dg[0]
    return dx, dm_l, G


def local_step(x, mem, target, layers, mem_norm_g, dm):
    trm = _pick(mem.shape[0], (256, 128, 64, 32, 16, 8))
    (m,) = tile_fwd("mem_norm", f_rms, [Row(mem, 0, dm.D)], [mem_norm_g[None]], [(dm.D, F32)], trm)
    res = []
    for l, P in enumerate(layers):
        x, R = layer_fwd(x, m, P, dm, f"l{l}_")
        res.append(R)
    loss, dx = loss_head(x, target, "loss_head")
    grads, dm_sum = [None] * len(layers), None
    for l in reversed(range(len(layers))):
        dx, dm_l, grads[l] = layer_bwd(dx, m, layers[l], res[l], dm, f"l{l}_")
        dm_sum = dm_l if dm_sum is None else dm_sum + dm_l
    _, (dg,) = tile_bwd("mem_norm_b", f_rms, [Row(mem, 0, dm.D)], [mem_norm_g[None]], [dm_sum], trm)
    return loss, dx, grads, dg[0]


ARG_NAMES = ["x", "mem", "mix_norm_g", "w_in", "fox_fb", "fox_q_norm_g", "fox_k_norm_g", "gdn_conv_w", "gdn_a_log",
             "gdn_dt_bias", "gdn_out_norm_g", "conf_dw_w", "conf_dw_b", "conf_ln_g", "conf_ln_b", "w_branch", "w_out",
             "mem_norm_g", "xattn_norm_g", "xattn_wq", "xattn_wkv", "xattn_q_norm_g", "xattn_k_norm_g", "xattn_wo",
             "ffn_norm_g", "ffn_w_up", "ffn_conv_w", "ffn_conv_b", "ffn_w_down"]
WEIGHTS = ARG_NAMES[2:]
COL_SHARDED = ["w_in", "gdn_conv_w", "conf_dw_w", "w_branch", "xattn_wo", "ffn_w_up", "ffn_conv_w"]
ROW_SHARDED = ["w_out", "xattn_wq", "xattn_wkv", "ffn_w_down"]
MATMUL_WEIGHTS = ["w_in", "w_branch", "w_out", "xattn_wq", "xattn_wkv", "xattn_wo", "ffn_w_up", "ffn_w_down"]
REPLICATED = [n for n in WEIGHTS if n not in COL_SHARDED + ROW_SHARDED]
SMALL_SHARDED = [n for n in COL_SHARDED + ROW_SHARDED if n not in MATMUL_WEIGHTS]


class LayerWeights:
    def __init__(self, small, fetch, emit, deps):
        self.small, self._fetch, self.emit, self.deps, self._cache = small, fetch, emit, deps, {}

    def __getitem__(self, name):
        return self.small[name]

    def preset(self, name, value):
        self._cache[name] = value

    def big(self, name, after):
        if name not in self._cache:
            self._cache[name] = self._fetch(name, after)
        return self._cache[name]


def regroup(x, blocked_in, name, own=None):
    if blocked_in:
        _, R, n = x.shape
    else:
        R, n = x.shape[0], x.shape[1] // N_DEV
    tr = _pick(R, (512, 256, 128, 64, 32, 16))
    blocked = pl.BlockSpec((None, tr, n), lambda d, i: (d, i, 0))
    flat = pl.BlockSpec((tr, n), lambda d, i: (i, d))

    def body(x_ref, *rest):
        o_ref = rest[-1]
        if own is None:
            o_ref[...] = x_ref[...]
        else:
            o_ref[...] = jnp.where(pl.program_id(0) == _me(), rest[0][...], x_ref[...])

    return pl.pallas_call(
        body, name=name, grid=(N_DEV, R // tr),
        in_specs=[blocked if blocked_in else flat] + ([pl.BlockSpec((tr, n), lambda d, i: (i, 0))] if own is not None else []),
        out_specs=flat if blocked_in else blocked,
        out_shape=jax.ShapeDtypeStruct((R, N_DEV * n) if blocked_in else (N_DEV, R, n), x.dtype),
        compiler_params=_cparams(("parallel", "parallel")),
    )(x, *([own] if own is not None else []))


def _assemble(name, land, own, dm, tag):
    me = _me()
    if name == "w_in":
        return permute_in_blocks([jnp.where(me == d, own, land[d]) for d in range(N_DEV)], dm)
    if name in COL_SHARDED:
        n = land.shape[-1]
        whole = regroup(land.reshape(N_DEV, -1, n), True, tag + "_regroup", own=own.reshape(-1, n))
        return whole.reshape(land.shape[1:-1] + (N_DEV * n,))
    slot = lax.broadcasted_iota(jnp.int32, (N_DEV,) + (1,) * own.ndim, 0)
    return jnp.where(slot == me, own[None], land).reshape((land.shape[0] * land.shape[1],) + land.shape[2:])


def _split(name, g, dm, tag):
    if name == "w_in":
        return unpermute_to_blocks(g, dm)
    if name in COL_SHARDED:
        n = g.shape[-1] // N_DEV
        return regroup(g.reshape(-1, g.shape[-1]), False, tag + "_regroup").reshape((N_DEV,) + g.shape[:-1] + (n,))
    return g.reshape((N_DEV, g.shape[0] // N_DEV) + g.shape[1:])


def _gather_small_weights(shards):
    names = list(shards)
    like = [shards[n] for n in names]
    g = all_gather(_pack(like), "gather_small_weights")
    per_dev = [_unpack(g[d], like) for d in range(N_DEV)]
    return {n: jnp.concatenate([per_dev[d][i] for d in range(N_DEV)], axis=-1) for i, n in enumerate(names)}


def _scatter_grad(name, g):
    if name in COL_SHARDED:
        g = g.reshape(g.shape[:-1] + (N_DEV, g.shape[-1] // N_DEV))
        return jnp.moveaxis(g, -2, 0)
    g = g.reshape(g.shape[:1] + (N_DEV, g.shape[1] // N_DEV) + g.shape[2:])
    return jnp.moveaxis(g, 1, 0)


def _pack(arrs):
    flat = jnp.concatenate([a.reshape(-1) for a in arrs])
    rows = -(-flat.shape[0] // (8 * LANES)) * 8
    return jnp.pad(flat, (0, rows * LANES - flat.shape[0])).reshape(rows, LANES)


def _unpack(packed, like):
    flat, out, o = packed.reshape(-1), [], 0
    for a in like:
        out.append(flat[o:o + a.size].reshape(a.shape))
        o += a.size
    return out


def kernel(x, mem, mix_norm_g, w_in, fox_fb, fox_q_norm_g, fox_k_norm_g, gdn_conv_w, gdn_a_log, gdn_dt_bias, gdn_out_norm_g, conf_dw_w, conf_dw_b, conf_ln_g, conf_ln_b, w_branch, w_out, mem_norm_g, xattn_norm_g, xattn_wq, xattn_wkv, xattn_q_norm_g, xattn_k_norm_g, xattn_wo, ffn_norm_g, ffn_w_up, ffn_conv_w, ffn_conv_b, ffn_w_down, loss_target, m_mix_norm_g, m_w_in, m_fox_fb, m_fox_q_norm_g, m_fox_k_norm_g, m_gdn_conv_w, m_gdn_a_log, m_gdn_dt_bias, m_gdn_out_norm_g, m_conf_dw_w, m_conf_dw_b, m_conf_ln_g, m_conf_ln_b, m_w_branch, m_w_out, m_mem_norm_g, m_xattn_norm_g, m_xattn_wq, m_xattn_wkv, m_xattn_q_norm_g, m_xattn_k_norm_g, m_xattn_wo, m_ffn_norm_g, m_ffn_w_up, m_ffn_conv_w, m_ffn_conv_b, m_ffn_w_down, v_mix_norm_g, v_w_in, v_fox_fb, v_fox_q_norm_g, v_fox_k_norm_g, v_gdn_conv_w, v_gdn_a_log, v_gdn_dt_bias, v_gdn_out_norm_g, v_conf_dw_w, v_conf_dw_b, v_conf_ln_g, v_conf_ln_b, v_w_branch, v_w_out, v_mem_norm_g, v_xattn_norm_g, v_xattn_wq, v_xattn_wkv, v_xattn_q_norm_g, v_xattn_k_norm_g, v_xattn_wo, v_ffn_norm_g, v_ffn_w_up, v_ffn_conv_w, v_ffn_conv_b, v_ffn_w_down):
    args = locals()
    W = {n: args[n] for n in WEIGHTS}
    Mo = {n: args["m_" + n] for n in WEIGHTS}
    Vo = {n: args["v_" + n] for n in WEIGHTS}
    L = mix_norm_g.shape[0]
    H, hd = fox_fb.shape[1], fox_q_norm_g.shape[1]
    dm = Dims(D=x.shape[-1], H=H, hd=hd, MW=H * hd, XH=xattn_wq.shape[-1] // hd, FF=ffn_conv_b.shape[-1])

    gathers, sent, tokens = {}, {}, []

    def take_tokens():
        got = list(tokens)
        tokens.clear()
        return got

    def start_gather(l, n, after):
        blk = W[n][l].astype(BF16)
        handle, token = exchange_start(blk, f"gather{l}_{n}_start", True, after)
        gathers[(l, n)] = (handle, blk)
        tokens.append(token)
        return token

    def fetch(l, n, after):
        handle, blk = gathers.pop((l, n))
        land = exchange_wait(handle, f"gather{l}_{n}_wait", True, blk if after is None else after)
        return _assemble(n, land, blk, dm, f"gather{l}_{n}")

    def emit(l, n, g):
        send = _split(n, g, dm, f"grad{l}_{n}")
        handle, token = exchange_start(send, f"grad{l}_{n}_start", False)
        sent[(l, n)] = (handle, send)
        tokens.append(token)

    small_full = _gather_small_weights({n: W[n] for n in SMALL_SHARDED})
    layers, follow = [], small_full[SMALL_SHARDED[0]]
    for l in range(L):
        small = {n: (small_full[n][l] if n in SMALL_SHARDED else W[n][l]) for n in WEIGHTS
                 if n not in MATMUL_WEIGHTS and n != "mem_norm_g"}
        layers.append(LayerWeights(small, functools.partial(fetch, l), functools.partial(emit, l), take_tokens))
        for n in MATMUL_WEIGHTS:
            follow = start_gather(l, n, follow)

    loss, dx, grads, d_mem_g = local_step(x[0], mem[0], loss_target[0], layers, mem_norm_g, dm)
    loss = lax.psum(loss, ("x", "y", "c"))

    def whole(n):
        return d_mem_g if n == "mem_norm_g" else jnp.stack([g[n] for g in grads])

    out, after = {}, dx
    for l in reversed(range(L)):
        for n in reversed(MATMUL_WEIGHTS):
            handle, send = sent.pop((l, n))
            land = exchange_wait(handle, f"grad{l}_{n}_wait", False, after)
            out[n] = adamw(land, W[n], Mo[n], Vo[n], f"adamw{l}_{n}", layer=l, into=out.get(n), sent=send)
            after = out[n][0]
    for n in SMALL_SHARDED:
        parts = all_to_all(_scatter_grad(n, whole(n)), "exchange_" + n)
        out[n] = adamw(parts, W[n], Mo[n], Vo[n], "adamw_" + n)
    rep = [whole(n) for n in REPLICATED]
    parts = all_gather(_pack(rep), "gather_small_grads")
    packed = adamw(parts, _pack([W[n] for n in REPLICATED]), _pack([Mo[n] for n in REPLICATED]),
                   _pack([Vo[n] for n in REPLICATED]), "adamw_small")
    unpacked = [_unpack(pk, rep) for pk in packed]
    for i, n in enumerate(REPLICATED):
        out[n] = tuple(u[i] for u in unpacked)

    return (loss, dx[None], *[out[n][0] for n in WEIGHTS], *[out[n][1] for n in WEIGHTS],
            *[out[n][2] for n in WEIGHTS], *[out[n][3] for n in WEIGHTS])
```
